```python
import math
import numpy as np
import jax
import jax.numpy as jnp
from jax import lax

D_MODEL = 2048
BATCH = 16
SEQ = 2048
DEPTH = 2

GRID_W = 64
CTX_LEN = 256
D_MIX = D_MODEL
GLA_WIDTH = D_MIX // 4
HY_WIDTH = D_MIX // 4
FN_WIDTH = D_MIX // 4
NA_WIDTH = D_MIX - GLA_WIDTH - HY_WIDTH - FN_WIDTH

GLA_HEADS = 4
GLA_DK = GLA_WIDTH // (2 * GLA_HEADS)
GLA_DV = GLA_WIDTH // GLA_HEADS
GLA_GATE_RANK = 16
GLA_TAU = 16.0
GLA_CHUNK = 64

HY_SHORT = 3
HY_BANDS = 16
HY_EMB = 2 * HY_BANDS + 1
HY_FILT_HIDDEN = 64
HY_DECAY_TARGET = 1e-2
HY_FAST_PCT = 0.3
HY_SLOW_PCT = 1.5
HY_DECAY_SHIFT = 0.05

FN_GROUPS = 4
FN_DG = FN_WIDTH // FN_GROUPS

NA_HEADS = 4
NA_DH = NA_WIDTH // NA_HEADS
NA_KR_MAX = 8
NA_KC = 16

ROPE_THETA = 10000.0
D_FF = 5632
FFN_CONV = 3
N_MOD = 6
EPS = 1e-6
F32 = jnp.float32

IN_SPLITS = (GLA_HEADS * GLA_DK, GLA_HEADS * GLA_DK, GLA_WIDTH, GLA_WIDTH, 2 * GLA_GATE_RANK,
             3 * HY_WIDTH, FN_WIDTH, NA_WIDTH, NA_WIDTH, NA_WIDTH)
D_IN = sum(IN_SPLITS)

kernel_name = 'hybrid_parallel_group_diffusion_block'


def rms_norm(x, gain):
    xf = x.astype(F32)
    y = xf * lax.rsqrt(jnp.mean(xf * xf, axis=-1, keepdims=True) + EPS)
    return (y * gain.astype(F32)).astype(x.dtype)


def modulate(h, shift, scale):
    return h * (1 + scale) + shift


def dwconv(u, w, b):
    k_w = w.shape[0]
    left = k_w // 2
    l = u.shape[1]
    up = jnp.pad(u, ((0, 0), (left, k_w - 1 - left), (0, 0)))
    y = b
    for i in range(k_w):
        y = y + up[:, i:i + l] * w[i]
    return y


def to_heads(t, n_heads):
    b, l, _ = t.shape
    return t.reshape(b, l, n_heads, -1).transpose(0, 2, 1, 3)


def from_heads(t):
    b, h, l, d = t.shape
    return t.transpose(0, 2, 1, 3).reshape(b, l, h * d)


def axial_rope(t, rows, cols):
    half = t.shape[-1] // 2
    nf = half // 2
    inv = ROPE_THETA ** (-jnp.arange(nf, dtype=F32) / nf)

    def rotate(u, pos):
        ang = pos.astype(F32)[:, None] * inv
        cos = jnp.cos(ang).astype(u.dtype)
        sin = jnp.sin(ang).astype(u.dtype)
        u1, u2 = u[..., :nf], u[..., nf:]
        return jnp.concatenate([u1 * cos - u2 * sin, u1 * sin + u2 * cos], axis=-1)

    return jnp.concatenate([rotate(t[..., :half], rows), rotate(t[..., half:], cols)], axis=-1)


def gla_scan(q, k, v, log_a, s0):
    b, h, l, dk = q.shape
    dv = v.shape[-1]
    n = l // GLA_CHUNK
    qc = q.reshape(b, h, n, GLA_CHUNK, dk)
    kc = k.reshape(b, h, n, GLA_CHUNK, dk)
    vc = v.reshape(b, h, n, GLA_CHUNK, dv)
    cum = jnp.cumsum(log_a.reshape(b, h, n, GLA_CHUNK, dk), axis=3)
    cum_end = cum[:, :, :, -1:, :]
    q_dec = qc * jnp.exp(cum)
    k_inv = kc * jnp.exp(-cum)
    k_end = kc * jnp.exp(cum_end - cum)
    tril = jnp.tril(jnp.ones((GLA_CHUNK, GLA_CHUNK), dtype=bool))
    scores = jnp.where(tril, jnp.einsum('bhnid,bhnjd->bhnij', q_dec, k_inv), 0.0)
    o_intra = jnp.einsum('bhnij,bhnjv->bhniv', scores, vc)
    ds = jnp.einsum('bhncd,bhncv->nbhdv', k_end, vc)
    decay = jnp.moveaxis(jnp.exp(cum_end[:, :, :, 0, :]), 2, 0)

    def step(s, inp):
        g, d = inp
        return g[..., None] * s + d, s

    s_fin, s_prev = lax.scan(step, s0, (decay, ds))
    o_inter = jnp.einsum('bhncd,nbhdv->bhncv', q_dec, s_prev)
    return (o_intra + o_inter).reshape(b, h, l, dv), s_fin


def gla_bidir(q, k, v, la_f, la_b, s0_f, s0_b):
    o_f, s_f = gla_scan(q, k, v, la_f, s0_f)
    o_b, s_b = gla_scan(jnp.flip(q, 2), jnp.flip(k, 2), jnp.flip(v, 2), jnp.flip(la_b, 2), s0_b)
    return o_f + jnp.flip(o_b, 2), s_f, s_b


def gla_inputs(gq, gk, gv, gz, w_gate, b_gate, rows=None, cols=None):
    b, l, _ = gq.shape
    q = to_heads(gq, GLA_HEADS).astype(F32) * GLA_DK ** -0.5
    k = to_heads(gk, GLA_HEADS).astype(F32)
    if rows is not None:
        q = axial_rope(q, rows, cols)
        k = axial_rope(k, rows, cols)
    v = to_heads(gv, GLA_HEADS).astype(F32)
    z = gz.astype(F32).reshape(b, l, 2, GLA_GATE_RANK)
    la = jax.nn.log_sigmoid(jnp.einsum('blur,urk->ublk', z, w_gate.astype(F32))
                            + b_gate.astype(F32)[:, None, None, :]) / GLA_TAU
    return q, k, v, to_heads(la[0], GLA_HEADS), to_heads(la[1], GLA_HEADS)


def gla_output(o, gr, gain):
    on = o * lax.rsqrt(jnp.mean(o * o, axis=-1, keepdims=True) + EPS) * gain.astype(F32)
    return from_heads(on) * jax.nn.silu(gr.astype(F32))


def hyena_filter_freq(l, w1, b1, w2, b2, w3, freq):
    t = jnp.linspace(0.0, 1.0, l, dtype=F32)[:, None]
    w = (2.0 * math.pi / l) * jnp.arange(l, dtype=F32)[:, None]
    f = jnp.linspace(1e-4, HY_BANDS - 1, HY_BANDS, dtype=F32)[None, :]
    z = jnp.concatenate([t, jnp.cos(f * w), -jnp.sin(f * w)], axis=-1)
    freq = freq.astype(F32)
    h = jnp.sin(freq[0] * (z @ w1.astype(F32) + b1.astype(F32)))
    h = jnp.sin(freq[1] * (h @ w2.astype(F32) + b2.astype(F32)))
    h = (h @ w3.astype(F32)).reshape(l, 2, HY_WIDTH)
    deltas = jnp.abs(jnp.linspace(math.log(HY_DECAY_TARGET) / HY_SLOW_PCT,
                                  math.log(HY_DECAY_TARGET) / HY_FAST_PCT, HY_WIDTH, dtype=F32))
    h = h * (jnp.exp(-t * deltas) + HY_DECAY_SHIFT)[:, None, :]
    h = h / (jnp.sum(jnp.abs(h), axis=(0, 1), keepdims=True) + EPS)
    fwd, bwd = h[:, 0], h[:, 1]
    kern = jnp.concatenate([fwd[:1] + bwd[:1], fwd[1:], jnp.zeros((1, HY_WIDTH), F32),
                            jnp.flip(bwd[1:], axis=0)], axis=0)
    return jnp.fft.rfft(kern, axis=0)


def hyena_mixer(hy, conv_w, conv_b, w1, b1, w2, b2, w3, freq, bias):
    l = hy.shape[1]
    u = dwconv(hy, conv_w, conv_b).astype(F32)
    x0, x1, v = jnp.split(u, 3, axis=-1)
    kf = hyena_filter_freq(l, w1, b1, w2, b2, w3, freq)
    s = v * x1
    sf = jnp.fft.rfft(s, n=2 * l, axis=1)
    y = jnp.fft.irfft(sf * kf[None], n=2 * l, axis=1)[:, :l] + s * bias.astype(F32)
    return y * x0


def fourier_mixer(u, w):
    b, l, _ = u.shape
    ug = u.astype(F32).reshape(b, l, FN_GROUPS, FN_DG)
    f = jnp.fft.fft2(ug, axes=(1, 3), norm='ortho').real
    return jnp.einsum('blgd,gde->blge', f, w.astype(F32)).reshape(b, l, FN_WIDTH)


def natten_latent(q, k, v, k_ctx, v_ctx, rpb):
    b, h, l, dh = q.shape
    n_rows = l // GRID_W
    kr = min(NA_KR_MAX, n_rows)
    scale = dh ** -0.5
    qg = q.reshape(b, h, n_rows, GRID_W, dh)
    kg = k.reshape(b, h, n_rows, GRID_W, dh)
    vg = v.reshape(b, h, n_rows, GRID_W, dh)
    r = jnp.arange(n_rows)
    col = jnp.arange(GRID_W)
    row_idx = jnp.clip(r - kr // 2, 0, n_rows - kr)[:, None] + jnp.arange(kr)[None, :]
    col0 = jnp.clip(col - NA_KC // 2, 0, GRID_W - NA_KC)
    k_rows = kg[:, :, row_idx]
    v_rows = vg[:, :, row_idx]
    s_loc = jnp.einsum('bhrqd,bhrikd->bhrqik', qg, k_rows).astype(F32) * scale
    d_row = row_idx - r[:, None]
    d_col = col[None, :] - col[:, None]
    in_win = (col[None, :] >= col0[:, None]) & (col[None, :] < col0[:, None] + NA_KC)
    bias = rpb.astype(F32)[:, (d_row + NA_KR_MAX - 1)[:, None, :, None],
                           jnp.clip(d_col + NA_KC - 1, 0, 2 * NA_KC - 2)[None, :, None, :]]
    s_loc = jnp.where(in_win[:, None, :], s_loc + bias[None], -jnp.inf)
    s_ctx = jnp.einsum('bhrqd,bhcd->bhrqc', qg, k_ctx).astype(F32) * scale
    n_loc = kr * GRID_W
    p = jax.nn.softmax(jnp.concatenate([s_loc.reshape(b, h, n_rows, GRID_W, n_loc), s_ctx], axis=-1),
                       axis=-1).astype(v.dtype)
    p_loc = p[..., :n_loc].reshape(b, h, n_rows, GRID_W, kr, GRID_W)
    out = (jnp.einsum('bhrqik,bhrikd->bhrqd', p_loc, v_rows)
           + jnp.einsum('bhrqc,bhcd->bhrqd', p[..., n_loc:], v_ctx))
    return out.reshape(b, h, l, dh)


def ctx_attention(q, k, v):
    s = jnp.einsum('bhqd,bhkd->bhqk', q, k).astype(F32) * q.shape[-1] ** -0.5
    p = jax.nn.softmax(s, axis=-1).astype(v.dtype)
    return jnp.einsum('bhqk,bhkd->bhqd', p, v)


def conv_ffn(h, w_up, conv_w, conv_b, w_down):
    u = dwconv(h @ w_up, conv_w, conv_b)
    a, g = jnp.split(u, 2, axis=-1)
    return (a * jax.nn.silu(g)) @ w_down


def setup_inputs(seed: int = 0) -> dict:
    key = jax.random.key(seed)
    ks = iter(jax.random.split(key, 40))

    def nrm(shape, scale):
        return scale * jax.random.normal(next(ks), shape, F32)

    def gain(shape):
        return 1.0 + 0.02 * jax.random.normal(next(ks), shape, F32)

    return {
        'x': nrm((BATCH, SEQ, D_MODEL), 1.0),
        'c': nrm((BATCH, D_MODEL), 1.0),
        'ctx': nrm((BATCH, CTX_LEN, D_MODEL), 1.0),
        'c_ctx': nrm((D_MODEL,), 1.0),
        'w_mod': nrm((DEPTH, D_MODEL, N_MOD * D_MODEL), 0.5 * D_MODEL ** -0.5),
        'b_mod': nrm((DEPTH, N_MOD * D_MODEL), 0.01),
        'g_mix': gain((DEPTH, D_MODEL)),
        'w_in': nrm((DEPTH, D_MODEL, D_IN), D_MODEL ** -0.5),
        'gla_gate_w': nrm((DEPTH, 2, GLA_GATE_RANK, GLA_HEADS * GLA_DK), GLA_GATE_RANK ** -0.5),
        'gla_gate_b': nrm((DEPTH, 2, GLA_HEADS * GLA_DK), 0.1),
        'gla_out_g': gain((DEPTH, GLA_DV)),
        'hy_conv_w': nrm((DEPTH, HY_SHORT, 3 * HY_WIDTH), HY_SHORT ** -0.5),
        'hy_conv_b': nrm((DEPTH, 3 * HY_WIDTH), 0.01),
        'hy_w1': nrm((DEPTH, HY_EMB, HY_FILT_HIDDEN), HY_EMB ** -0.5),
        'hy_b1': nrm((DEPTH, HY_FILT_HIDDEN), 0.1),
        'hy_w2': nrm((DEPTH, HY_FILT_HIDDEN, HY_FILT_HIDDEN), HY_FILT_HIDDEN ** -0.5),
        'hy_b2': nrm((DEPTH, HY_FILT_HIDDEN), 0.1),
        'hy_w3': nrm((DEPTH, HY_FILT_HIDDEN, 2 * HY_WIDTH), HY_FILT_HIDDEN ** -0.5),
        'hy_freq': gain((DEPTH, 2, HY_FILT_HIDDEN)),
        'hy_bias': nrm((DEPTH, HY_WIDTH), 0.5),
        'fn_w': nrm((DEPTH, FN_GROUPS, FN_DG, FN_DG), FN_DG ** -0.5),
        'na_q_g': gain((DEPTH, NA_DH)),
        'na_k_g': gain((DEPTH, NA_DH)),
        'na_rpb': nrm((DEPTH, NA_HEADS, 2 * NA_KR_MAX - 1, 2 * NA_KC - 1), 0.1),
        'w_out': nrm((DEPTH, D_MIX, D_MODEL), D_MIX ** -0.5),
        'g_ffn': gain((DEPTH, D_MODEL)),
        'ffn_w_up': nrm((DEPTH, D_MODEL, 2 * D_FF), D_MODEL ** -0.5),
        'ffn_conv_w': nrm((DEPTH, FFN_CONV, 2 * D_FF), FFN_CONV ** -0.5),
        'ffn_conv_b': nrm((DEPTH, 2 * D_FF), 0.01),
        'ffn_w_down': nrm((DEPTH, D_FF, D_MODEL), D_FF ** -0.5),
    }


def reference(x, c, ctx, c_ctx, w_mod, b_mod, g_mix, w_in, gla_gate_w, gla_gate_b, gla_out_g,
              hy_conv_w, hy_conv_b, hy_w1, hy_b1, hy_w2, hy_b2, hy_w3, hy_freq, hy_bias, fn_w,
              na_q_g, na_k_g, na_rpb, w_out, g_ffn, ffn_w_up, ffn_conv_w, ffn_conv_b, ffn_w_down):
    l_lat = x.shape[1]
    t = jnp.arange(l_lat)
    rows, cols = t // GRID_W, t % GRID_W
    split_at = np.cumsum(IN_SPLITS)[:-1].tolist()
    s_c = jax.nn.silu(c)
    s_cc = jax.nn.silu(c_ctx)
    for layer in range(DEPTH):
        last = layer == DEPTH - 1
        mod_x = jnp.split((s_c @ w_mod[layer] + b_mod[layer])[:, None, :], N_MOD, axis=-1)
        mod_c = jnp.split((s_cc @ w_mod[layer] + b_mod[layer])[None, None, :], N_MOD, axis=-1)
        hx = modulate(rms_norm(x, g_mix[layer]), mod_x[0], mod_x[1])
        hc = modulate(rms_norm(ctx, g_mix[layer]), mod_c[0], mod_c[1])
        gq, gk, gv, gr, gz, hy, fnu, nq, nk, nv = jnp.split(hx @ w_in[layer], split_at, axis=-1)
        cq, ck, cv, cr, cz, chy, cfn, cnq, cnk, cnv = jnp.split(hc @ w_in[layer], split_at, axis=-1)

        s0 = jnp.zeros((ctx.shape[0], GLA_HEADS, GLA_DK, GLA_DV), F32)
        o_ac, s_f, s_b = gla_bidir(*gla_inputs(cq, ck, cv, cz, gla_gate_w[layer], gla_gate_b[layer]), s0, s0)
        o_ax, _, _ = gla_bidir(*gla_inputs(gq, gk, gv, gz, gla_gate_w[layer], gla_gate_b[layer], rows, cols),
                               s_f, s_b)
        y_a = gla_output(o_ax, gr, gla_out_g[layer])
        y_b = hyena_mixer(hy, hy_conv_w[layer], hy_conv_b[layer], hy_w1[layer], hy_b1[layer], hy_w2[layer],
                          hy_b2[layer], hy_w3[layer], hy_freq[layer], hy_bias[layer])
        y_c = fourier_mixer(fnu, fn_w[layer])
        k_ctx = rms_norm(to_heads(cnk, NA_HEADS), na_k_g[layer])
        v_ctx = to_heads(cnv, NA_HEADS)
        y_d = from_heads(natten_latent(rms_norm(to_heads(nq, NA_HEADS), na_q_g[layer]),
                                       rms_norm(to_heads(nk, NA_HEADS), na_k_g[layer]),
                                       to_heads(nv, NA_HEADS), k_ctx, v_ctx, na_rpb[layer]))
        y = jnp.concatenate([y_a.astype(x.dtype), y_b.astype(x.dtype), y_c.astype(x.dtype),
                             y_d.astype(x.dtype)], axis=-1) @ w_out[layer]
        x = x + mod_x[2] * y
        x = x + mod_x[5] * conv_ffn(modulate(rms_norm(x, g_ffn[layer]), mod_x[3], mod_x[4]),
                                    ffn_w_up[layer], ffn_conv_w[layer], ffn_conv_b[layer], ffn_w_down[layer])

        if not last:
            yc_a = gla_output(o_ac, cr, gla_out_g[layer])
            yc_b = hyena_mixer(chy, hy_conv_w[layer], hy_conv_b[layer], hy_w1[layer], hy_b1[layer],
                               hy_w2[layer], hy_b2[layer], hy_w3[layer], hy_freq[layer], hy_bias[layer])
            yc_c = fourier_mixer(cfn, fn_w[layer])
            yc_d = from_heads(ctx_attention(rms_norm(to_heads(cnq, NA_HEADS), na_q_g[layer]), k_ctx, v_ctx))
            yc = jnp.concatenate([yc_a.astype(ctx.dtype), yc_b.astype(ctx.dtype), yc_c.astype(ctx.dtype),
                                  yc_d.astype(ctx.dtype)], axis=-1) @ w_out[layer]
            ctx = ctx + mod_c[2] * yc
            ctx = ctx + mod_c[5] * conv_ffn(modulate(rms_norm(ctx, g_ffn[layer]), mod_c[3], mod_c[4]),
                                            ffn_w_up[layer], ffn_conv_w[layer], ffn_conv_b[layer],
                                            ffn_w_down[layer])
    return x
```

```python
import functools
import math

import numpy as np
import jax
import jax.numpy as jnp
from jax import lax
from jax.experimental import pallas as pl
from jax.experimental.pallas import tpu as pltpu

F32 = jnp.float32
BF16 = jnp.bfloat16

GRID_W = 64
GLA_HEADS = 4
GLA_GATE_RANK = 16
GLA_TAU = 16.0
GLA_CHUNK = 64
HY_BANDS = 16
HY_DECAY_TARGET = 1e-2
HY_FAST_PCT = 0.3
HY_SLOW_PCT = 1.5
HY_DECAY_SHIFT = 0.05
FN_GROUPS = 4
NA_HEADS = 4
NA_KR_MAX = 8
NA_KC = 16
ROPE_THETA = 10000.0
N_MOD = 6
EPS = 1e-6

V7X_VMEM_LIMIT = 56 * 1024 * 1024
SUBLANES = 8
MOD_ROWS = 8
FFN_HALO = 16


def _cparams(sem):
    return pltpu.CompilerParams(dimension_semantics=sem, vmem_limit_bytes=V7X_VMEM_LIMIT)


def _mod_kernel(s_ref, w_ref, b_ref, o_ref):
    s = s_ref[...]
    s = s * jax.nn.sigmoid(s)
    o_ref[...] = jnp.dot(s.astype(BF16), w_ref[...].astype(BF16),
                         preferred_element_type=F32) + b_ref[...]


def mod_vectors(cond, w_mod, b_mod, tn=1024):
    depth, d, n = w_mod.shape
    r = cond.shape[0]
    return pl.pallas_call(
        _mod_kernel,
        grid=(depth, n // tn),
        in_specs=[pl.BlockSpec((r, d), lambda l, j: (0, 0)),
                  pl.BlockSpec((None, d, tn), lambda l, j: (l, 0, j)),
                  pl.BlockSpec((None, 1, tn), lambda l, j: (l, 0, j))],
        out_specs=pl.BlockSpec((None, r, tn), lambda l, j: (l, 0, j)),
        out_shape=jax.ShapeDtypeStruct((depth, r, n), F32),
        compiler_params=_cparams(("arbitrary", "arbitrary")),
        name="mod_vectors",
    )(cond, w_mod, b_mod.reshape(depth, 1, n))


def _norm_mod(x, gain, shift, scale):
    ms = jnp.mean(x * x, axis=-1, keepdims=True)
    return (x * lax.rsqrt(ms + EPS) * gain) * (1.0 + scale) + shift


def _inproj_kernel(x_ref, mod_ref, g_ref, w_ref, wz_ref, o_ref, oz_ref, h_ref):
    @pl.when(pl.program_id(2) == 0)
    def _():
        h = _norm_mod(x_ref[...], g_ref[...], mod_ref[0:1, :], mod_ref[1:2, :]).astype(BF16)
        h_ref[...] = h
        oz_ref[...] = jnp.dot(h, wz_ref[...], preferred_element_type=F32)

    o_ref[...] = jnp.dot(h_ref[...], w_ref[...], preferred_element_type=F32)


def in_proj(x, mod, gain, w_main, w_z, tm, tn=1024):
    g, l, d = x.shape
    n = w_main.shape[1]
    nz = w_z.shape[1]
    return pl.pallas_call(
        _inproj_kernel,
        grid=(g, l // tm, n // tn),
        in_specs=[pl.BlockSpec((None, tm, d), lambda b, i, j: (b, i, 0)),
                  pl.BlockSpec((None, MOD_ROWS, d), lambda b, i, j: (b, 0, 0)),
                  pl.BlockSpec((1, d), lambda b, i, j: (0, 0)),
                  pl.BlockSpec((d, tn), lambda b, i, j: (0, j)),
                  pl.BlockSpec((d, nz), lambda b, i, j: (0, 0))],
        out_specs=[pl.BlockSpec((None, tm, tn), lambda b, i, j: (b, i, j)),
                   pl.BlockSpec((None, tm, nz), lambda b, i, j: (b, i, 0))],
        out_shape=[jax.ShapeDtypeStruct((g, l, n), F32),
                   jax.ShapeDtypeStruct((g, l, nz), F32)],
        scratch_shapes=[pltpu.VMEM((tm, d), BF16)],
        compiler_params=_cparams(("arbitrary", "arbitrary", "arbitrary")),
        name="in_proj",
    )(x, mod, gain, w_main, w_z)


def _outproj_kernel(ya_ref, yb_ref, yc_ref, yd_ref, w_ref, x_ref, mod_ref, o_ref):
    kw = ya_ref.shape[-1]
    acc = jnp.dot(ya_ref[...], w_ref[0 * kw:1 * kw, :], preferred_element_type=F32)
    acc += jnp.dot(yb_ref[...], w_ref[1 * kw:2 * kw, :], preferred_element_type=F32)
    acc += jnp.dot(yc_ref[...], w_ref[2 * kw:3 * kw, :], preferred_element_type=F32)
    acc += jnp.dot(yd_ref[...], w_ref[3 * kw:4 * kw, :], preferred_element_type=F32)
    o_ref[...] = x_ref[...] + mod_ref[2:3, :] * acc


def out_proj(ys, w_out, x, mod, tm):
    g, l, d = x.shape
    kw = ys[0].shape[-1]
    yspec = pl.BlockSpec((None, tm, kw), lambda b, i: (b, i, 0))
    return pl.pallas_call(
        _outproj_kernel,
        grid=(g, l // tm),
        in_specs=[yspec, yspec, yspec, yspec,
                  pl.BlockSpec((4 * kw, d), lambda b, i: (0, 0)),
                  pl.BlockSpec((None, tm, d), lambda b, i: (b, i, 0)),
                  pl.BlockSpec((None, MOD_ROWS, d), lambda b, i: (b, 0, 0))],
        out_specs=pl.BlockSpec((None, tm, d), lambda b, i: (b, i, 0)),
        out_shape=jax.ShapeDtypeStruct((g, l, d), F32),
        compiler_params=_cparams(("arbitrary", "arbitrary")),
        name="out_proj",
    )(*ys, w_out, x, mod)


def _ffn_kernel(x_ref, xp_ref, xn_ref, mod_ref, g_ref, wa_ref, wg_ref, cw_ref, cb_ref, wd_ref,
                o_ref, h_ref):
    i = pl.program_id(1)
    j = pl.program_id(2)
    tm = x_ref.shape[0]
    tf = wa_ref.shape[1]
    hs = FFN_HALO

    @pl.when(j == 0)
    def _():
        gain, shift, scale = g_ref[...], mod_ref[3:4, :], mod_ref[4:5, :]
        h_ref[hs:hs + tm, :] = _norm_mod(x_ref[...], gain, shift, scale).astype(BF16)
        hp = jnp.where(i > 0, _norm_mod(xp_ref[...], gain, shift, scale), 0.0)
        hn = jnp.where(i < pl.num_programs(1) - 1, _norm_mod(xn_ref[...], gain, shift, scale), 0.0)
        zero = jnp.zeros_like(hp)
        h_ref[0:hs, :] = jnp.concatenate([zero, hp], axis=0).astype(BF16)
        h_ref[hs + tm:2 * hs + tm, :] = jnp.concatenate([hn, zero], axis=0).astype(BF16)
        o_ref[...] = x_ref[...]

    h = h_ref[...]

    def conv(u, k):
        w = cw_ref[:, k * tf:(k + 1) * tf]
        return (cb_ref[:, k * tf:(k + 1) * tf] + u[hs - 1:hs - 1 + tm] * w[0:1]
                + u[hs:hs + tm] * w[1:2] + u[hs + 1:hs + 1 + tm] * w[2:3])

    a = conv(jnp.dot(h, wa_ref[...], preferred_element_type=F32), 0)
    gt = conv(jnp.dot(h, wg_ref[...], preferred_element_type=F32), 1)
    act = (a * (gt * jax.nn.sigmoid(gt))).astype(BF16)
    o_ref[...] += mod_ref[5:6, :] * jnp.dot(act, wd_ref[...], preferred_element_type=F32)


def conv_ffn(x, mod, gain, w_up, conv_w, conv_b, w_down, tm, tf=512):
    g, l, d = x.shape
    f = w_down.shape[0]
    nf = f // tf
    nb = tm // SUBLANES
    last = l // SUBLANES - 1
    cw = conv_w.reshape(3, 2, nf, tf).transpose(2, 0, 1, 3).reshape(nf, 3, 2 * tf)
    cb = conv_b.reshape(1, 2, nf, tf).transpose(2, 0, 1, 3).reshape(nf, 1, 2 * tf)
    return pl.pallas_call(
        _ffn_kernel,
        grid=(g, l // tm, nf),
        in_specs=[pl.BlockSpec((None, tm, d), lambda b, i, j: (b, i, 0)),
                  pl.BlockSpec((None, SUBLANES, d), lambda b, i, j: (b, jnp.maximum(i * nb - 1, 0), 0)),
                  pl.BlockSpec((None, SUBLANES, d), lambda b, i, j: (b, jnp.minimum((i + 1) * nb, last), 0)),
                  pl.BlockSpec((None, MOD_ROWS, d), lambda b, i, j: (b, 0, 0)),
                  pl.BlockSpec((1, d), lambda b, i, j: (0, 0)),
                  pl.BlockSpec((d, tf), lambda b, i, j: (0, j)),
                  pl.BlockSpec((d, tf), lambda b, i, j: (0, nf + j)),
                  pl.BlockSpec((None, 3, 2 * tf), lambda b, i, j: (j, 0, 0)),
                  pl.BlockSpec((None, 1, 2 * tf), lambda b, i, j: (j, 0, 0)),
                  pl.BlockSpec((tf, d), lambda b, i, j: (j, 0))],
        out_specs=pl.BlockSpec((None, tm, d), lambda b, i, j: (b, i, 0)),
        out_shape=jax.ShapeDtypeStruct((g, l, d), F32),
        scratch_shapes=[pltpu.VMEM((tm + 2 * FFN_HALO, d), BF16)],
        compiler_params=_cparams(("arbitrary", "arbitrary", "arbitrary")),
        name="conv_ffn",
    )(x, x, x, mod, gain, w_up, w_up, cw, cb, w_down)


def rms_norm(x, gain):
    xf = x.astype(F32)
    y = xf * lax.rsqrt(jnp.mean(xf * xf, axis=-1, keepdims=True) + EPS)
    return (y * gain.astype(F32)).astype(x.dtype)


def dwconv(u, w, b):
    k_w = w.shape[0]
    left = k_w // 2
    l = u.shape[1]
    up = jnp.pad(u, ((0, 0), (left, k_w - 1 - left), (0, 0)))
    y = b
    for i in range(k_w):
        y = y + up[:, i:i + l] * w[i]
    return y


def to_heads(t, n_heads):
    b, l, _ = t.shape
    return t.reshape(b, l, n_heads, -1).transpose(0, 2, 1, 3)


def from_heads(t):
    b, h, l, d = t.shape
    return t.transpose(0, 2, 1, 3).reshape(b, l, h * d)


def axial_rope(t, rows, cols):
    half = t.shape[-1] // 2
    nf = half // 2
    inv = ROPE_THETA ** (-jnp.arange(nf, dtype=F32) / nf)

    def rotate(u, pos):
        ang = pos.astype(F32)[:, None] * inv
        cos = jnp.cos(ang).astype(u.dtype)
        sin = jnp.sin(ang).astype(u.dtype)
        u1, u2 = u[..., :nf], u[..., nf:]
        return jnp.concatenate([u1 * cos - u2 * sin, u1 * sin + u2 * cos], axis=-1)

    return jnp.concatenate([rotate(t[..., :half], rows), rotate(t[..., half:], cols)], axis=-1)


def gla_scan(q, k, v, log_a, s0):
    b, h, l, dk = q.shape
    dv = v.shape[-1]
    n = l // GLA_CHUNK
    qc = q.reshape(b, h, n, GLA_CHUNK, dk)
    kc = k.reshape(b, h, n, GLA_CHUNK, dk)
    vc = v.reshape(b, h, n, GLA_CHUNK, dv)
    cum = jnp.cumsum(log_a.reshape(b, h, n, GLA_CHUNK, dk), axis=3)
    cum_end = cum[:, :, :, -1:, :]
    q_dec = qc * jnp.exp(cum)
    k_inv = kc * jnp.exp(-cum)
    k_end = kc * jnp.exp(cum_end - cum)
    tril = jnp.tril(jnp.ones((GLA_CHUNK, GLA_CHUNK), dtype=bool))
    scores = jnp.where(tril, jnp.einsum('bhnid,bhnjd->bhnij', q_dec, k_inv), 0.0)
    o_intra = jnp.einsum('bhnij,bhnjv->bhniv', scores, vc)
    ds = jnp.einsum('bhncd,bhncv->nbhdv', k_end, vc)
    decay = jnp.moveaxis(jnp.exp(cum_end[:, :, :, 0, :]), 2, 0)

    def step(s, inp):
        g, d = inp
        return g[..., None] * s + d, s

    s_fin, s_prev = lax.scan(step, s0, (decay, ds))
    o_inter = jnp.einsum('bhncd,nbhdv->bhncv', q_dec, s_prev)
    return (o_intra + o_inter).reshape(b, h, l, dv), s_fin


def gla_bidir(q, k, v, la_f, la_b, s0_f, s0_b):
    o_f, s_f = gla_scan(q, k, v, la_f, s0_f)
    o_b, s_b = gla_scan(jnp.flip(q, 2), jnp.flip(k, 2), jnp.flip(v, 2), jnp.flip(la_b, 2), s0_b)
    return o_f + jnp.flip(o_b, 2), s_f, s_b


def gla_inputs(gq, gk, gv, gz, w_gate, b_gate, rows=None, cols=None):
    b, l, _ = gq.shape
    dk = gq.shape[-1] // GLA_HEADS
    q = to_heads(gq, GLA_HEADS).astype(F32) * dk ** -0.5
    k = to_heads(gk, GLA_HEADS).astype(F32)
    if rows is not None:
        q = axial_rope(q, rows, cols)
        k = axial_rope(k, rows, cols)
    v = to_heads(gv, GLA_HEADS).astype(F32)
    z = gz.astype(F32).reshape(b, l, 2, GLA_GATE_RANK)
    la = jax.nn.log_sigmoid(jnp.einsum('blur,urk->ublk', z, w_gate.astype(F32))
                            + b_gate.astype(F32)[:, None, None, :]) / GLA_TAU
    return q, k, v, to_heads(la[0], GLA_HEADS), to_heads(la[1], GLA_HEADS)


def gla_output(o, gr, gain):
    on = o * lax.rsqrt(jnp.mean(o * o, axis=-1, keepdims=True) + EPS) * gain.astype(F32)
    return from_heads(on) * jax.nn.silu(gr.astype(F32))


def hyena_filter_freq(l, w1, b1, w2, b2, w3, freq):
    c = w3.shape[1] // 2
    t = jnp.linspace(0.0, 1.0, l, dtype=F32)[:, None]
    w = (2.0 * math.pi / l) * jnp.arange(l, dtype=F32)[:, None]
    f = jnp.linspace(1e-4, HY_BANDS - 1, HY_BANDS, dtype=F32)[None, :]
    z = jnp.concatenate([t, jnp.cos(f * w), -jnp.sin(f * w)], axis=-1)
    freq = freq.astype(F32)
    h = jnp.sin(freq[0] * (z @ w1.astype(F32) + b1.astype(F32)))
    h = jnp.sin(freq[1] * (h @ w2.astype(F32) + b2.astype(F32)))
    h = (h @ w3.astype(F32)).reshape(l, 2, c)
    deltas = jnp.abs(jnp.linspace(math.log(HY_DECAY_TARGET) / HY_SLOW_PCT,
                                  math.log(HY_DECAY_TARGET) / HY_FAST_PCT, c, dtype=F32))
    h = h * (jnp.exp(-t * deltas) + HY_DECAY_SHIFT)[:, None, :]
    h = h / (jnp.sum(jnp.abs(h), axis=(0, 1), keepdims=True) + EPS)
    fwd, bwd = h[:, 0], h[:, 1]
    kern = jnp.concatenate([fwd[:1] + bwd[:1], fwd[1:], jnp.zeros((1, c), F32),
                            jnp.flip(bwd[1:], axis=0)], axis=0)
    return jnp.fft.rfft(kern, axis=0)


def hyena_mixer(hy, conv_w, conv_b, w1, b1, w2, b2, w3, freq, bias):
    l = hy.shape[1]
    u = dwconv(hy, conv_w, conv_b).astype(F32)
    x0, x1, v = jnp.split(u, 3, axis=-1)
    kf = hyena_filter_freq(l, w1, b1, w2, b2, w3, freq)
    s = v * x1
    sf = jnp.fft.rfft(s, n=2 * l, axis=1)
    y = jnp.fft.irfft(sf * kf[None], n=2 * l, axis=1)[:, :l] + s * bias.astype(F32)
    return y * x0


def fourier_mixer(u, w):
    b, l, c = u.shape
    ug = u.astype(F32).reshape(b, l, FN_GROUPS, c // FN_GROUPS)
    f = jnp.fft.fft2(ug, axes=(1, 3), norm='ortho').real
    return jnp.einsum('blgd,gde->blge', f, w.astype(F32)).reshape(b, l, c)


def natten_latent(q, k, v, k_ctx, v_ctx, rpb):
    b, h, l, dh = q.shape
    n_rows = l // GRID_W
    kr = min(NA_KR_MAX, n_rows)
    scale = dh ** -0.5
    qg = q.reshape(b, h, n_rows, GRID_W, dh)
    kg = k.reshape(b, h, n_rows, GRID_W, dh)
    vg = v.reshape(b, h, n_rows, GRID_W, dh)
    r = jnp.arange(n_rows)
    col = jnp.arange(GRID_W)
    row_idx = jnp.clip(r - kr // 2, 0, n_rows - kr)[:, None] + jnp.arange(kr)[None, :]
    col0 = jnp.clip(col - NA_KC // 2, 0, GRID_W - NA_KC)
    k_rows = kg[:, :, row_idx]
    v_rows = vg[:, :, row_idx]
    s_loc = jnp.einsum('bhrqd,bhrikd->bhrqik', qg, k_rows).astype(F32) * scale
    d_row = row_idx - r[:, None]
    d_col = col[None, :] - col[:, None]
    in_win = (col[None, :] >= col0[:, None]) & (col[None, :] < col0[:, None] + NA_KC)
    bias = rpb.astype(F32)[:, (d_row + NA_KR_MAX - 1)[:, None, :, None],
                           jnp.clip(d_col + NA_KC - 1, 0, 2 * NA_KC - 2)[None, :, None, :]]
    s_loc = jnp.where(in_win[:, None, :], s_loc + bias[None], -jnp.inf)
    s_ctx = jnp.einsum('bhrqd,bhcd->bhrqc', qg, k_ctx).astype(F32) * scale
    n_loc = kr * GRID_W
    p = jax.nn.softmax(jnp.concatenate([s_loc.reshape(b, h, n_rows, GRID_W, n_loc), s_ctx], axis=-1),
                       axis=-1).astype(v.dtype)
    p_loc = p[..., :n_loc].reshape(b, h, n_rows, GRID_W, kr, GRID_W)
    out = (jnp.einsum('bhrqik,bhrikd->bhrqd', p_loc, v_rows)
           + jnp.einsum('bhrqc,bhcd->bhrqd', p[..., n_loc:], v_ctx))
    return out.reshape(b, h, l, dh)


def ctx_attention(q, k, v):
    s = jnp.einsum('bhqd,bhkd->bhqk', q, k).astype(F32) * q.shape[-1] ** -0.5
    p = jax.nn.softmax(s, axis=-1).astype(v.dtype)
    return jnp.einsum('bhqk,bhkd->bhqd', p, v)


def _split_proj(p, pz, d):
    w = d // 4
    o = [0, w // 2, w, 2 * w, 3 * w, 6 * w, 7 * w, 8 * w, 9 * w, 10 * w]
    parts = [p[..., o[i]:o[i + 1]] for i in range(9)]
    gq, gk, gv, gr, hy, fnu, nq, nk, nv = parts
    return gq, gk, gv, gr, pz[..., :2 * GLA_GATE_RANK], hy, fnu, nq, nk, nv


def kernel(x, c, ctx, c_ctx, w_mod, b_mod, g_mix, w_in, gla_gate_w, gla_gate_b, gla_out_g,
           hy_conv_w, hy_conv_b, hy_w1, hy_b1, hy_w2, hy_b2, hy_w3, hy_freq, hy_bias, fn_w,
           na_q_g, na_k_g, na_rpb, w_out, g_ffn, ffn_w_up, ffn_conv_w, ffn_conv_b, ffn_w_down):
    bsz, l_lat, d = x.shape
    l_ctx = ctx.shape[1]
    depth = w_mod.shape[0]
    w = d // 4
    t = jnp.arange(l_lat)
    rows, cols = t // GRID_W, t % GRID_W

    n_cond = -(-(bsz + 1) // SUBLANES) * SUBLANES
    cond = jnp.zeros((n_cond, d), F32).at[:bsz].set(c).at[bsz].set(c_ctx)
    mods = mod_vectors(cond, w_mod, b_mod).reshape(depth, n_cond, N_MOD, d)
    mods = jnp.pad(mods, ((0, 0), (0, 0), (0, MOD_ROWS - N_MOD), (0, 0)))

    z0 = 3 * w
    zw = 2 * GLA_GATE_RANK
    w_main = jnp.concatenate([w_in[:, :, :z0], w_in[:, :, z0 + zw:]], axis=-1).astype(BF16)
    w_z = jnp.pad(w_in[:, :, z0:z0 + zw], ((0, 0), (0, 0), (0, 128 - zw))).astype(BF16)
    w_out_b = w_out.astype(BF16)
    w_up_b = ffn_w_up.astype(BF16)
    w_down_b = ffn_w_down.astype(BF16)

    ctx_flat = ctx.reshape(1, bsz * l_ctx, d)
    for layer in range(depth):
        last = layer == depth - 1
        mod_x = mods[layer, :bsz]
        mod_c = mods[layer, bsz:bsz + 1]
        g_mix_l = g_mix[layer].reshape(1, d)
        g_ffn_l = g_ffn[layer].reshape(1, d)

        px, pxz = in_proj(x, mod_x, g_mix_l, w_main[layer], w_z[layer], tm=1024)
        pc, pcz = in_proj(ctx_flat, mod_c, g_mix_l, w_main[layer], w_z[layer], tm=1024)
        pc = pc.reshape(bsz, l_ctx, -1)
        pcz = pcz.reshape(bsz, l_ctx, -1)
        gq, gk, gv, gr, gz, hy, fnu, nq, nk, nv = _split_proj(px, pxz, d)
        cq, ck, cv, cr, cz, chy, cfn, cnq, cnk, cnv = _split_proj(pc, pcz, d)

        dk = gq.shape[-1] // GLA_HEADS
        dv = gv.shape[-1] // GLA_HEADS
        s0 = jnp.zeros((bsz, GLA_HEADS, dk, dv), F32)
        o_ac, s_f, s_b = gla_bidir(*gla_inputs(cq, ck, cv, cz, gla_gate_w[layer], gla_gate_b[layer]), s0, s0)
        o_ax, _, _ = gla_bidir(*gla_inputs(gq, gk, gv, gz, gla_gate_w[layer], gla_gate_b[layer], rows, cols),
                               s_f, s_b)
        y_a = gla_output(o_ax, gr, gla_out_g[layer])
        hy_args = (hy_conv_w[layer], hy_conv_b[layer], hy_w1[layer], hy_b1[layer], hy_w2[layer],
                   hy_b2[layer], hy_w3[layer], hy_freq[layer], hy_bias[layer])
        y_b = hyena_mixer(hy, *hy_args)
        y_c = fourier_mixer(fnu, fn_w[layer])
        k_ctx = rms_norm(to_heads(cnk, NA_HEADS), na_k_g[layer])
        v_ctx = to_heads(cnv, NA_HEADS)
        y_d = from_heads(natten_latent(rms_norm(to_heads(nq, NA_HEADS), na_q_g[layer]),
                                       rms_norm(to_heads(nk, NA_HEADS), na_k_g[layer]),
                                       to_heads(nv, NA_HEADS), k_ctx, v_ctx, na_rpb[layer]))
        ys = [y.astype(BF16) for y in (y_a, y_b, y_c, y_d)]
        x = out_proj(ys, w_out_b[layer], x, mod_x, tm=512)
        cw = ffn_conv_w[layer]
        cb = ffn_conv_b[layer].reshape(1, -1)
        x = conv_ffn(x, mod_x, g_ffn_l, w_up_b[layer], cw, cb, w_down_b[layer], tm=512)

        if not last:
            yc_a = gla_output(o_ac, cr, gla_out_g[layer])
            yc_b = hyena_mixer(chy, *hy_args)
            yc_c = fourier_mixer(cfn, fn_w[layer])
            yc_d = from_heads(ctx_attention(rms_norm(to_heads(cnq, NA_HEADS), na_q_g[layer]), k_ctx, v_ctx))
            ycs = [y.astype(BF16).reshape(1, bsz * l_ctx, w) for y in (yc_a, yc_b, yc_c, yc_d)]
            ctx_flat = out_proj(ycs, w_out_b[layer], ctx_flat, mod_c, tm=512)
            mod_cb = jnp.broadcast_to(mod_c, (bsz, MOD_ROWS, d))
            ctx_flat = conv_ffn(ctx_flat.reshape(bsz, l_ctx, d), mod_cb, g_ffn_l, w_up_b[layer], cw, cb,
                                w_down_b[layer], tm=l_ctx).reshape(1, bsz * l_ctx, d)
    return x
```

```python
import functools
import math

import numpy as np
import jax
import jax.numpy as jnp
from jax import lax
from jax.experimental import pallas as pl
from jax.experimental.pallas import tpu as pltpu

F32 = jnp.float32
BF16 = jnp.bfloat16

GRID_W = 64
GLA_HEADS = 4
GLA_GATE_RANK = 16
GLA_TAU = 16.0
GLA_CHUNK = 64
HY_BANDS = 16
HY_DECAY_TARGET = 1e-2
HY_FAST_PCT = 0.3
HY_SLOW_PCT = 1.5
HY_DECAY_SHIFT = 0.05
FN_GROUPS = 4
NA_HEADS = 4
NA_KR_MAX = 8
NA_KC = 16
ROPE_THETA = 10000.0
N_MOD = 6
EPS = 1e-6

V7X_VMEM_LIMIT = 56 * 1024 * 1024
SUBLANES = 8
MOD_ROWS = 8
FFN_HALO = 16


def _cparams(sem):
    return pltpu.CompilerParams(dimension_semantics=sem, vmem_limit_bytes=V7X_VMEM_LIMIT)


def _mod_kernel(s_ref, w_ref, b_ref, o_ref):
    s = s_ref[...]
    s = s * jax.nn.sigmoid(s)
    o_ref[...] = jnp.dot(s.astype(BF16), w_ref[...].astype(BF16),
                         preferred_element_type=F32) + b_ref[...]


def mod_vectors(cond, w_mod, b_mod, tn=1024):
    depth, d, n = w_mod.shape
    r = cond.shape[0]
    return pl.pallas_call(
        _mod_kernel,
        grid=(depth, n // tn),
        in_specs=[pl.BlockSpec((r, d), lambda l, j: (0, 0)),
                  pl.BlockSpec((None, d, tn), lambda l, j: (l, 0, j)),
                  pl.BlockSpec((None, 1, tn), lambda l, j: (l, 0, j))],
        out_specs=pl.BlockSpec((None, r, tn), lambda l, j: (l, 0, j)),
        out_shape=jax.ShapeDtypeStruct((depth, r, n), F32),
        compiler_params=_cparams(("arbitrary", "arbitrary")),
        name="mod_vectors",
    )(cond, w_mod, b_mod.reshape(depth, 1, n))


def _norm_mod(x, gain, shift, scale):
    ms = jnp.mean(x * x, axis=-1, keepdims=True)
    return (x * lax.rsqrt(ms + EPS) * gain) * (1.0 + scale) + shift


def _inproj_kernel(x_ref, mod_ref, g_ref, w_ref, wz_ref, o_ref, oz_ref, h_ref):
    @pl.when(pl.program_id(2) == 0)
    def _():
        h = _norm_mod(x_ref[...], g_ref[...], mod_ref[0:1, :], mod_ref[1:2, :]).astype(BF16)
        h_ref[...] = h
        oz_ref[...] = jnp.dot(h, wz_ref[...], preferred_element_type=F32)

    o_ref[...] = jnp.dot(h_ref[...], w_ref[...], preferred_element_type=F32)


def in_proj(x, mod, gain, w_main, w_z, tm, tn=1024):
    g, l, d = x.shape
    n = w_main.shape[1]
    nz = w_z.shape[1]
    return pl.pallas_call(
        _inproj_kernel,
        grid=(g, l // tm, n // tn),
        in_specs=[pl.BlockSpec((None, tm, d), lambda b, i, j: (b, i, 0)),
                  pl.BlockSpec((None, MOD_ROWS, d), lambda b, i, j: (b, 0, 0)),
                  pl.BlockSpec((1, d), lambda b, i, j: (0, 0)),
                  pl.BlockSpec((d, tn), lambda b, i, j: (0, j)),
                  pl.BlockSpec((d, nz), lambda b, i, j: (0, 0))],
        out_specs=[pl.BlockSpec((None, tm, tn), lambda b, i, j: (b, i, j)),
                   pl.BlockSpec((None, tm, nz), lambda b, i, j: (b, i, 0))],
        out_shape=[jax.ShapeDtypeStruct((g, l, n), F32),
                   jax.ShapeDtypeStruct((g, l, nz), F32)],
        scratch_shapes=[pltpu.VMEM((tm, d), BF16)],
        compiler_params=_cparams(("arbitrary", "arbitrary", "arbitrary")),
        name="in_proj",
    )(x, mod, gain, w_main, w_z)


def _outproj_kernel(ya_ref, yb_ref, yc_ref, yd_ref, w_ref, x_ref, mod_ref, o_ref):
    kw = ya_ref.shape[-1]
    acc = jnp.dot(ya_ref[...], w_ref[0 * kw:1 * kw, :], preferred_element_type=F32)
    acc += jnp.dot(yb_ref[...], w_ref[1 * kw:2 * kw, :], preferred_element_type=F32)
    acc += jnp.dot(yc_ref[...], w_ref[2 * kw:3 * kw, :], preferred_element_type=F32)
    acc += jnp.dot(yd_ref[...], w_ref[3 * kw:4 * kw, :], preferred_element_type=F32)
    o_ref[...] = x_ref[...] + mod_ref[2:3, :] * acc


def out_proj(ys, w_out, x, mod, tm):
    g, l, d = x.shape
    kw = ys[0].shape[-1]
    yspec = pl.BlockSpec((None, tm, kw), lambda b, i: (b, i, 0))
    return pl.pallas_call(
        _outproj_kernel,
        grid=(g, l // tm),
        in_specs=[yspec, yspec, yspec, yspec,
                  pl.BlockSpec((4 * kw, d), lambda b, i: (0, 0)),
                  pl.BlockSpec((None, tm, d), lambda b, i: (b, i, 0)),
                  pl.BlockSpec((None, MOD_ROWS, d), lambda b, i: (b, 0, 0))],
        out_specs=pl.BlockSpec((None, tm, d), lambda b, i: (b, i, 0)),
        out_shape=jax.ShapeDtypeStruct((g, l, d), F32),
        compiler_params=_cparams(("arbitrary", "arbitrary")),
        name="out_proj",
    )(*ys, w_out, x, mod)


def _ffn_kernel(x_ref, xp_ref, xn_ref, mod_ref, g_ref, wa_ref, wg_ref, cw_ref, cb_ref, wd_ref,
                o_ref, h_ref):
    i = pl.program_id(1)
    j = pl.program_id(2)
    tm = x_ref.shape[0]
    tf = wa_ref.shape[1]
    hs = FFN_HALO

    @pl.when(j == 0)
    def _():
        gain, shift, scale = g_ref[...], mod_ref[3:4, :], mod_ref[4:5, :]
        h_ref[hs:hs + tm, :] = _norm_mod(x_ref[...], gain, shift, scale).astype(BF16)
        hp = jnp.where(i > 0, _norm_mod(xp_ref[...], gain, shift, scale), 0.0)
        hn = jnp.where(i < pl.num_programs(1) - 1, _norm_mod(xn_ref[...], gain, shift, scale), 0.0)
        zero = jnp.zeros_like(hp)
        h_ref[0:hs, :] = jnp.concatenate([zero, hp], axis=0).astype(BF16)
        h_ref[hs + tm:2 * hs + tm, :] = jnp.concatenate([hn, zero], axis=0).astype(BF16)
        o_ref[...] = x_ref[...]

    h = h_ref[...]

    def conv(u, k):
        w = cw_ref[:, k * tf:(k + 1) * tf]
        return (cb_ref[:, k * tf:(k + 1) * tf] + u[hs - 1:hs - 1 + tm] * w[0:1]
                + u[hs:hs + tm] * w[1:2] + u[hs + 1:hs + 1 + tm] * w[2:3])

    a = conv(jnp.dot(h, wa_ref[...], preferred_element_type=F32), 0)
    gt = conv(jnp.dot(h, wg_ref[...], preferred_element_type=F32), 1)
    act = (a * (gt * jax.nn.sigmoid(gt))).astype(BF16)
    o_ref[...] += mod_ref[5:6, :] * jnp.dot(act, wd_ref[...], preferred_element_type=F32)


def conv_ffn(x, mod, gain, w_up, conv_w, conv_b, w_down, tm, tf=512):
    g, l, d = x.shape
    f = w_down.shape[0]
    nf = f // tf
    nb = tm // SUBLANES
    last = l // SUBLANES - 1
    cw = conv_w.reshape(3, 2, nf, tf).transpose(2, 0, 1, 3).reshape(nf, 3, 2 * tf)
    cb = conv_b.reshape(1, 2, nf, tf).transpose(2, 0, 1, 3).reshape(nf, 1, 2 * tf)
    return pl.pallas_call(
        _ffn_kernel,
        grid=(g, l // tm, nf),
        in_specs=[pl.BlockSpec((None, tm, d), lambda b, i, j: (b, i, 0)),
                  pl.BlockSpec((None, SUBLANES, d), lambda b, i, j: (b, jnp.maximum(i * nb - 1, 0), 0)),
                  pl.BlockSpec((None, SUBLANES, d), lambda b, i, j: (b, jnp.minimum((i + 1) * nb, last), 0)),
                  pl.BlockSpec((None, MOD_ROWS, d), lambda b, i, j: (b, 0, 0)),
                  pl.BlockSpec((1, d), lambda b, i, j: (0, 0)),
                  pl.BlockSpec((d, tf), lambda b, i, j: (0, j)),
                  pl.BlockSpec((d, tf), lambda b, i, j: (0, nf + j)),
                  pl.BlockSpec((None, 3, 2 * tf), lambda b, i, j: (j, 0, 0)),
                  pl.BlockSpec((None, 1, 2 * tf), lambda b, i, j: (j, 0, 0)),
                  pl.BlockSpec((tf, d), lambda b, i, j: (j, 0))],
        out_specs=pl.BlockSpec((None, tm, d), lambda b, i, j: (b, i, 0)),
        out_shape=jax.ShapeDtypeStruct((g, l, d), F32),
        scratch_shapes=[pltpu.VMEM((tm + 2 * FFN_HALO, d), BF16)],
        compiler_params=_cparams(("arbitrary", "arbitrary", "arbitrary")),
        name="conv_ffn",
    )(x, x, x, mod, gain, w_up, w_up, cw, cb, w_down)


def natten_bias_table(rpb, n_rows):
    kr = min(NA_KR_MAX, n_rows)
    col = np.arange(GRID_W)
    col0 = np.clip(col - NA_KC // 2, 0, GRID_W - NA_KC)
    in_win = (col[None, :] >= col0[:, None]) & (col[None, :] < col0[:, None] + NA_KC)
    c_idx = np.clip(col[None, :] - col[:, None] + NA_KC - 1, 0, 2 * NA_KC - 2)
    r_idx = np.arange(kr)[None, :] - np.arange(kr)[:, None] + NA_KR_MAX - 1
    tab = rpb.astype(F32)[:, r_idx[:, None, :, None], c_idx[None, :, None, :]]
    tab = jnp.where(in_win[None, None, :, None, :], tab, -jnp.inf)
    return tab.reshape(rpb.shape[0], kr, GRID_W, kr * GRID_W)


def _rms(x, gain):
    return x * lax.rsqrt(jnp.mean(x * x, axis=-1, keepdims=True) + EPS) * gain


def _dot_nt(a, b):
    return lax.dot_general(a, b, (((1,), (1,)), ((), ())), preferred_element_type=F32)


def _natten_kernel(*refs, n_rows, with_ctx_out):
    if with_ctx_out:
        (q_ref, k_ref, v_ref, kc_ref, vc_ref, gq_ref, gk_ref, bias_ref, qc_ref,
         o_ref, oc_ref, qs, ks, vs) = refs
    else:
        q_ref, k_ref, v_ref, kc_ref, vc_ref, gq_ref, gk_ref, bias_ref, o_ref, qs, ks, vs = refs
    dh = q_ref.shape[-1]
    kr = min(NA_KR_MAX, n_rows)
    scale = dh ** -0.5
    qs[...] = (_rms(q_ref[...], gq_ref[...]) * scale).astype(BF16)
    ks[...] = _rms(k_ref[...], gk_ref[...]).astype(BF16)
    vs[...] = v_ref[...].astype(BF16)
    kc = _rms(kc_ref[...], gk_ref[...]).astype(BF16)
    vc = vc_ref[...].astype(BF16)

    def row(r, carry):
        ws = jnp.clip(r - kr // 2, 0, n_rows - kr)
        q = qs[pl.ds(pl.multiple_of(r * GRID_W, GRID_W), GRID_W), :]
        k0 = pl.multiple_of(ws * GRID_W, GRID_W)
        s_loc = _dot_nt(q, ks[pl.ds(k0, kr * GRID_W), :]) + bias_ref[r - ws]
        s_ctx = _dot_nt(q, kc)
        m = jnp.maximum(jnp.max(s_loc, axis=-1, keepdims=True), jnp.max(s_ctx, axis=-1, keepdims=True))
        p_loc = jnp.exp(s_loc - m)
        p_ctx = jnp.exp(s_ctx - m)
        den = jnp.sum(p_loc, axis=-1, keepdims=True) + jnp.sum(p_ctx, axis=-1, keepdims=True)
        o = (jnp.dot(p_loc.astype(BF16), vs[pl.ds(k0, kr * GRID_W), :], preferred_element_type=F32)
             + jnp.dot(p_ctx.astype(BF16), vc, preferred_element_type=F32))
        o_ref[pl.ds(pl.multiple_of(r * GRID_W, GRID_W), GRID_W), :] = (o / den).astype(o_ref.dtype)
        return carry

    lax.fori_loop(0, n_rows, row, 0)

    if with_ctx_out:
        qc = (_rms(qc_ref[...], gq_ref[...]) * scale).astype(BF16)
        s = _dot_nt(qc, kc)
        p = jnp.exp(s - jnp.max(s, axis=-1, keepdims=True))
        o = jnp.dot(p.astype(BF16), vc, preferred_element_type=F32) / jnp.sum(p, axis=-1, keepdims=True)
        oc_ref[...] = o.astype(oc_ref.dtype)


def natten(px, pc, col_q, gq, gk, bias, with_ctx_out):
    bsz, l, _ = px.shape
    lc = pc.shape[1]
    h, kr, _, nk = bias.shape
    dh = gq.shape[-1]
    cb = col_q // dh

    def head_spec(rows, which):
        return pl.BlockSpec((None, rows, dh), lambda b, hh: (b, 0, cb + which * h + hh))

    in_specs = [head_spec(l, 0), head_spec(l, 1), head_spec(l, 2), head_spec(lc, 1), head_spec(lc, 2),
                pl.BlockSpec((1, dh), lambda b, hh: (0, 0)), pl.BlockSpec((1, dh), lambda b, hh: (0, 0)),
                pl.BlockSpec((None, kr, GRID_W, nk), lambda b, hh: (hh, 0, 0, 0))]
    args = [px, px, px, pc, pc, gq, gk, bias]
    out_specs = [pl.BlockSpec((None, l, dh), lambda b, hh: (b, 0, hh))]
    out_shape = [jax.ShapeDtypeStruct((bsz, l, h * dh), BF16)]
    if with_ctx_out:
        in_specs.append(head_spec(lc, 0))
        args.append(pc)
        out_specs.append(pl.BlockSpec((None, lc, dh), lambda b, hh: (b, 0, hh)))
        out_shape.append(jax.ShapeDtypeStruct((bsz, lc, h * dh), BF16))
    outs = pl.pallas_call(
        functools.partial(_natten_kernel, n_rows=l // GRID_W, with_ctx_out=with_ctx_out),
        grid=(bsz, h),
        in_specs=in_specs, out_specs=out_specs, out_shape=out_shape,
        scratch_shapes=[pltpu.VMEM((l, dh), BF16)] * 3,
        compiler_params=_cparams(("arbitrary", "arbitrary")),
        name="natten",
    )(*args)
    return outs if with_ctx_out else (outs[0], None)


def _cos_sin(n, period):
    k = np.arange(n, dtype=np.int64)
    ang = (2.0 * np.pi / period) * ((k[:, None] * k[None, :]) % period)
    return np.cos(ang), np.sin(ang)


def _fourier_kernel(u_ref, w_ref, cd_ref, sd_ref, cs_ref, o_ref, wc_ref, ws_ref, v_ref, *, scale):
    l, c = u_ref.shape
    dg = cd_ref.shape[0]

    @pl.when(pl.program_id(0) == 0)
    def _():
        wc_ref[...] = jnp.zeros_like(wc_ref)
        ws_ref[...] = jnp.zeros_like(ws_ref)
        for g in range(c // dg):
            sl = slice(g * dg, (g + 1) * dg)
            wg = w_ref[g]
            wc_ref[sl, sl] = jnp.dot(cd_ref[...], wg, precision=lax.Precision.HIGHEST,
                                     preferred_element_type=F32).astype(BF16)
            ws_ref[sl, sl] = jnp.dot(sd_ref[...], wg, precision=lax.Precision.HIGHEST,
                                     preferred_element_type=F32).astype(BF16)

    u = u_ref[...].astype(BF16)
    v_ref[0:l, :] = jnp.dot(u, wc_ref[...], preferred_element_type=F32).astype(BF16)
    v_ref[l:2 * l, :] = jnp.dot(u, ws_ref[...], preferred_element_type=F32).astype(BF16)
    y = jnp.dot(cs_ref[...], v_ref[...], preferred_element_type=F32)
    o_ref[...] = (y * scale).astype(o_ref.dtype)


def fourier_mix(p, col, w):
    bsz, l, _ = p.shape
    g, dg, _ = w.shape
    c = g * dg
    cl, sl = _cos_sin(l, l)
    cd, sd = _cos_sin(dg, dg)
    cs = jnp.asarray(np.concatenate([cl, -sl], axis=1), BF16)
    return pl.pallas_call(
        functools.partial(_fourier_kernel, scale=float((l * dg) ** -0.5)),
        grid=(bsz,),
        in_specs=[pl.BlockSpec((None, l, c), lambda b: (b, 0, col // c)),
                  pl.BlockSpec((g, dg, dg), lambda b: (0, 0, 0)),
                  pl.BlockSpec((dg, dg), lambda b: (0, 0)),
                  pl.BlockSpec((dg, dg), lambda b: (0, 0)),
                  pl.BlockSpec((l, 2 * l), lambda b: (0, 0), pipeline_mode=pl.Buffered(1))],
        out_specs=pl.BlockSpec((None, l, c), lambda b: (b, 0, 0)),
        out_shape=jax.ShapeDtypeStruct((bsz, l, c), BF16),
        scratch_shapes=[pltpu.VMEM((c, c), BF16), pltpu.VMEM((c, c), BF16), pltpu.VMEM((2 * l, c), BF16)],
        compiler_params=_cparams(("arbitrary",)),
        name="fourier_mix",
    )(p, w, jnp.asarray(cd, F32), jnp.asarray(sd, F32), cs)


HY_CBLK = 256


def _hyena_dft_matrix(l):
    k = np.arange(l, dtype=np.int64)
    ang = (np.pi / l) * ((k[:, None] * k[None, :]) % (2 * l))
    sn = np.sin(ang)
    sn[0, :] = 1.0 - 2.0 * (k % 2)
    return jnp.asarray(np.concatenate([np.cos(ang), sn], axis=0), BF16)


def _hyena_pos_features(l):
    t = np.linspace(0.0, 1.0, l)[:, None]
    w = (2.0 * np.pi / l) * np.arange(l)[:, None]
    f = np.linspace(1e-4, HY_BANDS - 1, HY_BANDS)[None, :]
    z = np.concatenate([t, np.cos(f * w), -np.sin(f * w)], axis=-1)
    return np.pad(z, ((0, 0), (0, 128 - z.shape[1])))


def _split_bf16(x):
    hi = x.astype(BF16)
    return hi, (x - hi.astype(F32)).astype(BF16)


def _hyena_filter_kernel(z_ref, w1_ref, b1_ref, w2_ref, b2_ref, fr_ref, w3f_ref, w3b_ref, dl_ref, m_ref,
                         p_ref, q_ref):
    l = z_ref.shape[0]
    hp = lax.Precision.HIGHEST
    z = z_ref[...]
    h = jnp.sin(fr_ref[0:1, :] * (jnp.dot(z, w1_ref[...], precision=hp, preferred_element_type=F32)
                                  + b1_ref[...]))
    h = jnp.sin(fr_ref[1:2, :] * (jnp.dot(h, w2_ref[...], precision=hp, preferred_element_type=F32)
                                  + b2_ref[...]))
    window = jnp.exp(-z[:, 0:1] * dl_ref[...]) + HY_DECAY_SHIFT
    hf = jnp.dot(h, w3f_ref[...], precision=hp, preferred_element_type=F32) * window
    hb = jnp.dot(h, w3b_ref[...], precision=hp, preferred_element_type=F32) * window
    norm = jnp.sum(jnp.abs(hf) + jnp.abs(hb), axis=0, keepdims=True) + EPS
    hf = hf / norm
    hb = hb / norm
    g1h, g1l = _split_bf16(hf + hb)
    g2h, g2l = _split_bf16(hb - hf)
    f1 = (jnp.dot(m_ref[...], g1h, preferred_element_type=F32)
          + jnp.dot(m_ref[...], g1l, preferred_element_type=F32))
    f2 = (jnp.dot(m_ref[l:2 * l, :], g2h, preferred_element_type=F32)
          + jnp.dot(m_ref[l:2 * l, :], g2l, preferred_element_type=F32))
    p_ref[...] = f1[0:l]
    row = lax.broadcasted_iota(jnp.int32, f2.shape, 0)
    q_ref[...] = jnp.where(row == 0, f1[l:l + 1], f2)


def hyena_filter_spectrum(l, w1, b1, w2, b2, w3, freq, m):
    c = w3.shape[1] // 2
    hid = w1.shape[1]
    z = jnp.asarray(_hyena_pos_features(l), F32)
    w1p = jnp.pad(w1, ((0, z.shape[1] - w1.shape[0]), (0, 0)))
    deltas = np.abs(np.linspace(math.log(HY_DECAY_TARGET) / HY_SLOW_PCT,
                                math.log(HY_DECAY_TARGET) / HY_FAST_PCT, c))[None, :]
    nb = c // HY_CBLK
    full = lambda shape: pl.BlockSpec(shape, lambda j: (0,) * len(shape))
    return pl.pallas_call(
        _hyena_filter_kernel,
        grid=(nb,),
        in_specs=[full(z.shape), full(w1p.shape), full((1, hid)), full(w2.shape), full((1, hid)),
                  full((2, hid)),
                  pl.BlockSpec((hid, HY_CBLK), lambda j: (0, j)),
                  pl.BlockSpec((hid, HY_CBLK), lambda j: (0, nb + j)),
                  pl.BlockSpec((1, HY_CBLK), lambda j: (0, j)),
                  pl.BlockSpec(m.shape, lambda j: (0, 0), pipeline_mode=pl.Buffered(1))],
        out_specs=[pl.BlockSpec((l, HY_CBLK), lambda j: (0, j))] * 2,
        out_shape=[jax.ShapeDtypeStruct((l, c), F32)] * 2,
        compiler_params=_cparams(("arbitrary",)),
        name="hyena_filter",
    )(z, w1p, b1.reshape(1, hid), w2, b2.reshape(1, hid), freq, w3, w3, jnp.asarray(deltas, F32), m)


def _shift_rows(u, down):
    l = u.shape[0]
    row = lax.broadcasted_iota(jnp.int32, u.shape, 0)
    if down:
        return jnp.where(row == 0, 0.0, pltpu.roll(u, 1, 0))
    return jnp.where(row == l - 1, 0.0, pltpu.roll(u, l - 1, 0))


def _dwconv3(u, taps):
    return (taps[3:4] + _shift_rows(u, True) * taps[0:1] + u * taps[1:2] + _shift_rows(u, False) * taps[2:3])


def _hyena_conv_kernel(x0_ref, x1_ref, v_ref, taps_ref, bias_ref, p_ref, q_ref, m_ref, o_ref):
    l = x0_ref.shape[0]
    x1 = _dwconv3(x1_ref[...], taps_ref[1])
    s = _dwconv3(v_ref[...], taps_ref[2]) * x1
    f = jnp.dot(m_ref[...], s.astype(BF16), preferred_element_type=F32)
    a, b = f[0:l], f[l:2 * l]
    p, q = p_ref[...], q_ref[...]
    first = lax.broadcasted_iota(jnp.int32, a.shape, 0) == 0
    bq = b * q
    y_cos = (a * p + jnp.where(first, 0.0, bq)) * jnp.where(first, 0.5 / l, 1.0 / l)
    y_sin = jnp.where(first, 0.0, (b * p - a * q) * (1.0 / l))
    y = (jnp.dot(m_ref[0:l, :], y_cos.astype(BF16), preferred_element_type=F32)
         + jnp.where(first, 0.0, jnp.dot(m_ref[l:2 * l, :], y_sin.astype(BF16), preferred_element_type=F32)))
    parity = lax.broadcasted_iota(jnp.int32, a.shape, 0) % 2
    y = y + jnp.where(parity == 0, 1.0, -1.0) * (bq[0:1] * (0.5 / l))
    x0 = _dwconv3(x0_ref[...], taps_ref[0])
    o_ref[...] = ((y + s * bias_ref[...]) * x0).astype(o_ref.dtype)


def hyena_conv(p, col, conv_w, conv_b, bias, spec_p, spec_q, m):
    bsz, l, _ = p.shape
    c = bias.shape[-1]
    nb = c // HY_CBLK
    cb0 = col // HY_CBLK
    taps = jnp.concatenate([conv_w, conv_b[None]], axis=0)
    taps = taps.reshape(4, 3, nb, HY_CBLK).transpose(2, 1, 0, 3)

    def part(k):
        return pl.BlockSpec((None, l, HY_CBLK), lambda j, b: (b, 0, cb0 + k * nb + j))

    return pl.pallas_call(
        _hyena_conv_kernel,
        grid=(nb, bsz),
        in_specs=[part(0), part(1), part(2),
                  pl.BlockSpec((None, 3, 4, HY_CBLK), lambda j, b: (j, 0, 0, 0)),
                  pl.BlockSpec((1, HY_CBLK), lambda j, b: (0, j)),
                  pl.BlockSpec((l, HY_CBLK), lambda j, b: (0, j)),
                  pl.BlockSpec((l, HY_CBLK), lambda j, b: (0, j)),
                  pl.BlockSpec(m.shape, lambda j, b: (0, 0), pipeline_mode=pl.Buffered(1))],
        out_specs=pl.BlockSpec((None, l, HY_CBLK), lambda j, b: (b, 0, j)),
        out_shape=jax.ShapeDtypeStruct((bsz, l, c), BF16),
        compiler_params=_cparams(("arbitrary", "arbitrary")),
        name="hyena_conv",
    )(p, p, p, taps, bias.reshape(1, c), spec_p, spec_q, m)


def rms_norm(x, gain):
    xf = x.astype(F32)
    y = xf * lax.rsqrt(jnp.mean(xf * xf, axis=-1, keepdims=True) + EPS)
    return (y * gain.astype(F32)).astype(x.dtype)


def dwconv(u, w, b):
    k_w = w.shape[0]
    left = k_w // 2
    l = u.shape[1]
    up = jnp.pad(u, ((0, 0), (left, k_w - 1 - left), (0, 0)))
    y = b
    for i in range(k_w):
        y = y + up[:, i:i + l] * w[i]
    return y


def to_heads(t, n_heads):
    b, l, _ = t.shape
    return t.reshape(b, l, n_heads, -1).transpose(0, 2, 1, 3)


def from_heads(t):
    b, h, l, d = t.shape
    return t.transpose(0, 2, 1, 3).reshape(b, l, h * d)


def axial_rope(t, rows, cols):
    half = t.shape[-1] // 2
    nf = half // 2
    inv = ROPE_THETA ** (-jnp.arange(nf, dtype=F32) / nf)

    def rotate(u, pos):
        ang = pos.astype(F32)[:, None] * inv
        cos = jnp.cos(ang).astype(u.dtype)
        sin = jnp.sin(ang).astype(u.dtype)
        u1, u2 = u[..., :nf], u[..., nf:]
        return jnp.concatenate([u1 * cos - u2 * sin, u1 * sin + u2 * cos], axis=-1)

    return jnp.concatenate([rotate(t[..., :half], rows), rotate(t[..., half:], cols)], axis=-1)


def gla_scan(q, k, v, log_a, s0):
    b, h, l, dk = q.shape
    dv = v.shape[-1]
    n = l // GLA_CHUNK
    qc = q.reshape(b, h, n, GLA_CHUNK, dk)
    kc = k.reshape(b, h, n, GLA_CHUNK, dk)
    vc = v.reshape(b, h, n, GLA_CHUNK, dv)
    cum = jnp.cumsum(log_a.reshape(b, h, n, GLA_CHUNK, dk), axis=3)
    cum_end = cum[:, :, :, -1:, :]
    q_dec = qc * jnp.exp(cum)
    k_inv = kc * jnp.exp(-cum)
    k_end = kc * jnp.exp(cum_end - cum)
    tril = jnp.tril(jnp.ones((GLA_CHUNK, GLA_CHUNK), dtype=bool))
    scores = jnp.where(tril, jnp.einsum('bhnid,bhnjd->bhnij', q_dec, k_inv), 0.0)
    o_intra = jnp.einsum('bhnij,bhnjv->bhniv', scores, vc)
    ds = jnp.einsum('bhncd,bhncv->nbhdv', k_end, vc)
    decay = jnp.moveaxis(jnp.exp(cum_end[:, :, :, 0, :]), 2, 0)

    def step(s, inp):
        g, d = inp
        return g[..., None] * s + d, s

    s_fin, s_prev = lax.scan(step, s0, (decay, ds))
    o_inter = jnp.einsum('bhncd,nbhdv->bhncv', q_dec, s_prev)
    return (o_intra + o_inter).reshape(b, h, l, dv), s_fin


def gla_bidir(q, k, v, la_f, la_b, s0_f, s0_b):
    o_f, s_f = gla_scan(q, k, v, la_f, s0_f)
    o_b, s_b = gla_scan(jnp.flip(q, 2), jnp.flip(k, 2), jnp.flip(v, 2), jnp.flip(la_b, 2), s0_b)
    return o_f + jnp.flip(o_b, 2), s_f, s_b


def gla_inputs(gq, gk, gv, gz, w_gate, b_gate, rows=None, cols=None):
    b, l, _ = gq.shape
    dk = gq.shape[-1] // GLA_HEADS
    q = to_heads(gq, GLA_HEADS).astype(F32) * dk ** -0.5
    k = to_heads(gk, GLA_HEADS).astype(F32)
    if rows is not None:
        q = axial_rope(q, rows, cols)
        k = axial_rope(k, rows, cols)
    v = to_heads(gv, GLA_HEADS).astype(F32)
    z = gz.astype(F32).reshape(b, l, 2, GLA_GATE_RANK)
    la = jax.nn.log_sigmoid(jnp.einsum('blur,urk->ublk', z, w_gate.astype(F32))
                            + b_gate.astype(F32)[:, None, None, :]) / GLA_TAU
    return q, k, v, to_heads(la[0], GLA_HEADS), to_heads(la[1], GLA_HEADS)


def gla_output(o, gr, gain):
    on = o * lax.rsqrt(jnp.mean(o * o, axis=-1, keepdims=True) + EPS) * gain.astype(F32)
    return from_heads(on) * jax.nn.silu(gr.astype(F32))


def hyena_filter_freq(l, w1, b1, w2, b2, w3, freq):
    c = w3.shape[1] // 2
    t = jnp.linspace(0.0, 1.0, l, dtype=F32)[:, None]
    w = (2.0 * math.pi / l) * jnp.arange(l, dtype=F32)[:, None]
    f = jnp.linspace(1e-4, HY_BANDS - 1, HY_BANDS, dtype=F32)[None, :]
    z = jnp.concatenate([t, jnp.cos(f * w), -jnp.sin(f * w)], axis=-1)
    freq = freq.astype(F32)
    h = jnp.sin(freq[0] * (z @ w1.astype(F32) + b1.astype(F32)))
    h = jnp.sin(freq[1] * (h @ w2.astype(F32) + b2.astype(F32)))
    h = (h @ w3.astype(F32)).reshape(l, 2, c)
    deltas = jnp.abs(jnp.linspace(math.log(HY_DECAY_TARGET) / HY_SLOW_PCT,
                                  math.log(HY_DECAY_TARGET) / HY_FAST_PCT, c, dtype=F32))
    h = h * (jnp.exp(-t * deltas) + HY_DECAY_SHIFT)[:, None, :]
    h = h / (jnp.sum(jnp.abs(h), axis=(0, 1), keepdims=True) + EPS)
    fwd, bwd = h[:, 0], h[:, 1]
    kern = jnp.concatenate([fwd[:1] + bwd[:1], fwd[1:], jnp.zeros((1, c), F32),
                            jnp.flip(bwd[1:], axis=0)], axis=0)
    return jnp.fft.rfft(kern, axis=0)


def hyena_mixer(hy, conv_w, conv_b, w1, b1, w2, b2, w3, freq, bias):
    l = hy.shape[1]
    u = dwconv(hy, conv_w, conv_b).astype(F32)
    x0, x1, v = jnp.split(u, 3, axis=-1)
    kf = hyena_filter_freq(l, w1, b1, w2, b2, w3, freq)
    s = v * x1
    sf = jnp.fft.rfft(s, n=2 * l, axis=1)
    y = jnp.fft.irfft(sf * kf[None], n=2 * l, axis=1)[:, :l] + s * bias.astype(F32)
    return y * x0


def fourier_mixer(u, w):
    b, l, c = u.shape
    ug = u.astype(F32).reshape(b, l, FN_GROUPS, c // FN_GROUPS)
    f = jnp.fft.fft2(ug, axes=(1, 3), norm='ortho').real
    return jnp.einsum('blgd,gde->blge', f, w.astype(F32)).reshape(b, l, c)


def natten_latent(q, k, v, k_ctx, v_ctx, rpb):
    b, h, l, dh = q.shape
    n_rows = l // GRID_W
    kr = min(NA_KR_MAX, n_rows)
    scale = dh ** -0.5
    qg = q.reshape(b, h, n_rows, GRID_W, dh)
    kg = k.reshape(b, h, n_rows, GRID_W, dh)
    vg = v.reshape(b, h, n_rows, GRID_W, dh)
    r = jnp.arange(n_rows)
    col = jnp.arange(GRID_W)
    row_idx = jnp.clip(r - kr // 2, 0, n_rows - kr)[:, None] + jnp.arange(kr)[None, :]
    col0 = jnp.clip(col - NA_KC // 2, 0, GRID_W - NA_KC)
    k_rows = kg[:, :, row_idx]
    v_rows = vg[:, :, row_idx]
    s_loc = jnp.einsum('bhrqd,bhrikd->bhrqik', qg, k_rows).astype(F32) * scale
    d_row = row_idx - r[:, None]
    d_col = col[None, :] - col[:, None]
    in_win = (col[None, :] >= col0[:, None]) & (col[None, :] < col0[:, None] + NA_KC)
    bias = rpb.astype(F32)[:, (d_row + NA_KR_MAX - 1)[:, None, :, None],
                           jnp.clip(d_col + NA_KC - 1, 0, 2 * NA_KC - 2)[None, :, None, :]]
    s_loc = jnp.where(in_win[:, None, :], s_loc + bias[None], -jnp.inf)
    s_ctx = jnp.einsum('bhrqd,bhcd->bhrqc', qg, k_ctx).astype(F32) * scale
    n_loc = kr * GRID_W
    p = jax.nn.softmax(jnp.concatenate([s_loc.reshape(b, h, n_rows, GRID_W, n_loc), s_ctx], axis=-1),
                       axis=-1).astype(v.dtype)
    p_loc = p[..., :n_loc].reshape(b, h, n_rows, GRID_W, kr, GRID_W)
    out = (jnp.einsum('bhrqik,bhrikd->bhrqd', p_loc, v_rows)
           + jnp.einsum('bhrqc,bhcd->bhrqd', p[..., n_loc:], v_ctx))
    return out.reshape(b, h, l, dh)


def ctx_attention(q, k, v):
    s = jnp.einsum('bhqd,bhkd->bhqk', q, k).astype(F32) * q.shape[-1] ** -0.5
    p = jax.nn.softmax(s, axis=-1).astype(v.dtype)
    return jnp.einsum('bhqk,bhkd->bhqd', p, v)


def _split_proj(p, pz, d):
    w = d // 4
    o = [0, w // 2, w, 2 * w, 3 * w, 6 * w, 7 * w, 8 * w, 9 * w, 10 * w]
    parts = [p[..., o[i]:o[i + 1]] for i in range(9)]
    gq, gk, gv, gr, hy, fnu, nq, nk, nv = parts
    return gq, gk, gv, gr, pz[..., :2 * GLA_GATE_RANK], hy, fnu, nq, nk, nv


def kernel(x, c, ctx, c_ctx, w_mod, b_mod, g_mix, w_in, gla_gate_w, gla_gate_b, gla_out_g,
           hy_conv_w, hy_conv_b, hy_w1, hy_b1, hy_w2, hy_b2, hy_w3, hy_freq, hy_bias, fn_w,
           na_q_g, na_k_g, na_rpb, w_out, g_ffn, ffn_w_up, ffn_conv_w, ffn_conv_b, ffn_w_down):
    bsz, l_lat, d = x.shape
    l_ctx = ctx.shape[1]
    depth = w_mod.shape[0]
    w = d // 4
    t = jnp.arange(l_lat)
    rows, cols = t // GRID_W, t % GRID_W

    n_cond = -(-(bsz + 1) // SUBLANES) * SUBLANES
    cond = jnp.zeros((n_cond, d), F32).at[:bsz].set(c).at[bsz].set(c_ctx)
    mods = mod_vectors(cond, w_mod, b_mod).reshape(depth, n_cond, N_MOD, d)
    mods = jnp.pad(mods, ((0, 0), (0, 0), (0, MOD_ROWS - N_MOD), (0, 0)))

    z0 = 3 * w
    zw = 2 * GLA_GATE_RANK
    w_main = jnp.concatenate([w_in[:, :, :z0], w_in[:, :, z0 + zw:]], axis=-1).astype(BF16)
    w_z = jnp.pad(w_in[:, :, z0:z0 + zw], ((0, 0), (0, 0), (0, 128 - zw))).astype(BF16)
    w_out_b = w_out.astype(BF16)
    w_up_b = ffn_w_up.astype(BF16)
    w_down_b = ffn_w_down.astype(BF16)

    m_lat = _hyena_dft_matrix(l_lat)
    m_ctx = _hyena_dft_matrix(l_ctx)
    ctx_flat = ctx.reshape(1, bsz * l_ctx, d)
    for layer in range(depth):
        last = layer == depth - 1
        mod_x = mods[layer, :bsz]
        mod_c = mods[layer, bsz:bsz + 1]
        g_mix_l = g_mix[layer].reshape(1, d)
        g_ffn_l = g_ffn[layer].reshape(1, d)

        px, pxz = in_proj(x, mod_x, g_mix_l, w_main[layer], w_z[layer], tm=1024)
        pc, pcz = in_proj(ctx_flat, mod_c, g_mix_l, w_main[layer], w_z[layer], tm=1024)
        pc = pc.reshape(bsz, l_ctx, -1)
        pcz = pcz.reshape(bsz, l_ctx, -1)
        gq, gk, gv, gr, gz, hy, fnu, nq, nk, nv = _split_proj(px, pxz, d)
        cq, ck, cv, cr, cz, chy, cfn, cnq, cnk, cnv = _split_proj(pc, pcz, d)

        dk = gq.shape[-1] // GLA_HEADS
        dv = gv.shape[-1] // GLA_HEADS
        s0 = jnp.zeros((bsz, GLA_HEADS, dk, dv), F32)
        o_ac, s_f, s_b = gla_bidir(*gla_inputs(cq, ck, cv, cz, gla_gate_w[layer], gla_gate_b[layer]), s0, s0)
        o_ax, _, _ = gla_bidir(*gla_inputs(gq, gk, gv, gz, gla_gate_w[layer], gla_gate_b[layer], rows, cols),
                               s_f, s_b)
        y_a = gla_output(o_ax, gr, gla_out_g[layer])
        hy_filt = (hy_w1[layer], hy_b1[layer], hy_w2[layer], hy_b2[layer], hy_w3[layer], hy_freq[layer])
        hy_p, hy_q = hyena_filter_spectrum(l_lat, *hy_filt, m_lat)
        y_b = hyena_conv(px, 3 * w, hy_conv_w[layer], hy_conv_b[layer], hy_bias[layer], hy_p, hy_q, m_lat)
        y_c = fourier_mix(px, 6 * w, fn_w[layer])
        na_bias = natten_bias_table(na_rpb[layer], l_lat // GRID_W)
        y_d, yc_d = natten(px, pc, 7 * w, na_q_g[layer].reshape(1, -1), na_k_g[layer].reshape(1, -1),
                           na_bias, with_ctx_out=not last)
        ys = [y.astype(BF16) for y in (y_a, y_b, y_c, y_d)]
        x = out_proj(ys, w_out_b[layer], x, mod_x, tm=512)
        cw = ffn_conv_w[layer]
        cb = ffn_conv_b[layer].reshape(1, -1)
        x = conv_ffn(x, mod_x, g_ffn_l, w_up_b[layer], cw, cb, w_down_b[layer], tm=512)

        if not last:
            yc_a = gla_output(o_ac, cr, gla_out_g[layer])
            hc_p, hc_q = hyena_filter_spectrum(l_ctx, *hy_filt, m_ctx)
            yc_b = hyena_conv(pc, 3 * w, hy_conv_w[layer], hy_conv_b[layer], hy_bias[layer], hc_p, hc_q, m_ctx)
            yc_c = fourier_mix(pc, 6 * w, fn_w[layer])
            ycs = [y.astype(BF16).reshape(1, bsz * l_ctx, w) for y in (yc_a, yc_b, yc_c, yc_d)]
            ctx_flat = out_proj(ycs, w_out_b[layer], ctx_flat, mod_c, tm=512)
            mod_cb = jnp.broadcast_to(mod_c, (bsz, MOD_ROWS, d))
            ctx_flat = conv_ffn(ctx_flat.reshape(bsz, l_ctx, d), mod_cb, g_ffn_l, w_up_b[layer], cw, cb,
                                w_down_b[layer], tm=l_ctx).reshape(1, bsz * l_ctx, d)
    return x
```

```python
import functools
import math

import numpy as np
import jax
import jax.numpy as jnp
from jax import lax
from jax.experimental import pallas as pl
from jax.experimental.pallas import tpu as pltpu

F32 = jnp.float32
BF16 = jnp.bfloat16

GRID_W = 64
GLA_HEADS = 4
GLA_GATE_RANK = 16
GLA_TAU = 16.0
GLA_CHUNK = 64
HY_BANDS = 16
HY_DECAY_TARGET = 1e-2
HY_FAST_PCT = 0.3
HY_SLOW_PCT = 1.5
HY_DECAY_SHIFT = 0.05
FN_GROUPS = 4
NA_HEADS = 4
NA_KR_MAX = 8
NA_KC = 16
ROPE_THETA = 10000.0
N_MOD = 6
EPS = 1e-6

V7X_VMEM_LIMIT = 56 * 1024 * 1024
SUBLANES = 8
LANES = 128
MOD_ROWS = 8
FFN_HALO = 16


def _cparams(sem):
    return pltpu.CompilerParams(dimension_semantics=sem, vmem_limit_bytes=V7X_VMEM_LIMIT)


def _mod_kernel(s_ref, w_ref, b_ref, o_ref):
    s = s_ref[...]
    s = s * jax.nn.sigmoid(s)
    o_ref[...] = jnp.dot(s.astype(BF16), w_ref[...].astype(BF16),
                         preferred_element_type=F32) + b_ref[...]


def mod_vectors(cond, w_mod, b_mod, tn=1024):
    depth, d, n = w_mod.shape
    r = cond.shape[0]
    return pl.pallas_call(
        _mod_kernel,
        grid=(depth, n // tn),
        in_specs=[pl.BlockSpec((r, d), lambda l, j: (0, 0)),
                  pl.BlockSpec((None, d, tn), lambda l, j: (l, 0, j)),
                  pl.BlockSpec((None, 1, tn), lambda l, j: (l, 0, j))],
        out_specs=pl.BlockSpec((None, r, tn), lambda l, j: (l, 0, j)),
        out_shape=jax.ShapeDtypeStruct((depth, r, n), F32),
        compiler_params=_cparams(("arbitrary", "arbitrary")),
        name="mod_vectors",
    )(cond, w_mod, b_mod.reshape(depth, 1, n))


def _norm_mod(x, gain, shift, scale):
    ms = jnp.mean(x * x, axis=-1, keepdims=True)
    return (x * lax.rsqrt(ms + EPS) * gain) * (1.0 + scale) + shift


def _inproj_kernel(x_ref, mod_ref, g_ref, w_ref, wz_ref, o_ref, oz_ref, h_ref):
    @pl.when(pl.program_id(2) == 0)
    def _():
        h = _norm_mod(x_ref[...], g_ref[...], mod_ref[0:1, :], mod_ref[1:2, :]).astype(BF16)
        h_ref[...] = h
        oz_ref[...] = jnp.dot(h, wz_ref[...], preferred_element_type=F32)

    o_ref[...] = jnp.dot(h_ref[...], w_ref[...], preferred_element_type=F32)


def in_proj(x, mod, gain, w_main, w_z, tm, tn=1024):
    g, l, d = x.shape
    n = w_main.shape[1]
    nz = w_z.shape[1]
    return pl.pallas_call(
        _inproj_kernel,
        grid=(g, l // tm, n // tn),
        in_specs=[pl.BlockSpec((None, tm, d), lambda b, i, j: (b, i, 0)),
                  pl.BlockSpec((None, MOD_ROWS, d), lambda b, i, j: (b, 0, 0)),
                  pl.BlockSpec((1, d), lambda b, i, j: (0, 0)),
                  pl.BlockSpec((d, tn), lambda b, i, j: (0, j)),
                  pl.BlockSpec((d, nz), lambda b, i, j: (0, 0))],
        out_specs=[pl.BlockSpec((None, tm, tn), lambda b, i, j: (b, i, j)),
                   pl.BlockSpec((None, tm, nz), lambda b, i, j: (b, i, 0))],
        out_shape=[jax.ShapeDtypeStruct((g, l, n), F32),
                   jax.ShapeDtypeStruct((g, l, nz), F32)],
        scratch_shapes=[pltpu.VMEM((tm, d), BF16)],
        compiler_params=_cparams(("arbitrary", "arbitrary", "arbitrary")),
        name="in_proj",
    )(x, mod, gain, w_main, w_z)


def _outproj_kernel(ya_ref, yb_ref, yc_ref, yd_ref, w_ref, x_ref, mod_ref, o_ref):
    kw = ya_ref.shape[-1]
    acc = jnp.dot(ya_ref[...], w_ref[0 * kw:1 * kw, :], preferred_element_type=F32)
    acc += jnp.dot(yb_ref[...], w_ref[1 * kw:2 * kw, :], preferred_element_type=F32)
    acc += jnp.dot(yc_ref[...], w_ref[2 * kw:3 * kw, :], preferred_element_type=F32)
    acc += jnp.dot(yd_ref[...], w_ref[3 * kw:4 * kw, :], preferred_element_type=F32)
    o_ref[...] = x_ref[...] + mod_ref[2:3, :] * acc


def out_proj(ys, w_out, x, mod, tm):
    g, l, d = x.shape
    kw = ys[0].shape[-1]
    yspec = pl.BlockSpec((None, tm, kw), lambda b, i: (b, i, 0))
    return pl.pallas_call(
        _outproj_kernel,
        grid=(g, l // tm),
        in_specs=[yspec, yspec, yspec, yspec,
                  pl.BlockSpec((4 * kw, d), lambda b, i: (0, 0)),
                  pl.BlockSpec((None, tm, d), lambda b, i: (b, i, 0)),
                  pl.BlockSpec((None, MOD_ROWS, d), lambda b, i: (b, 0, 0))],
        out_specs=pl.BlockSpec((None, tm, d), lambda b, i: (b, i, 0)),
        out_shape=jax.ShapeDtypeStruct((g, l, d), F32),
        compiler_params=_cparams(("arbitrary", "arbitrary")),
        name="out_proj",
    )(*ys, w_out, x, mod)


def _ffn_kernel(x_ref, xp_ref, xn_ref, mod_ref, g_ref, wa_ref, wg_ref, cw_ref, cb_ref, wd_ref,
                o_ref, h_ref):
    i = pl.program_id(1)
    j = pl.program_id(2)
    tm = x_ref.shape[0]
    tf = wa_ref.shape[1]
    hs = FFN_HALO

    @pl.when(j == 0)
    def _():
        gain, shift, scale = g_ref[...], mod_ref[3:4, :], mod_ref[4:5, :]
        h_ref[hs:hs + tm, :] = _norm_mod(x_ref[...], gain, shift, scale).astype(BF16)
        hp = jnp.where(i > 0, _norm_mod(xp_ref[...], gain, shift, scale), 0.0)
        hn = jnp.where(i < pl.num_programs(1) - 1, _norm_mod(xn_ref[...], gain, shift, scale), 0.0)
        zero = jnp.zeros_like(hp)
        h_ref[0:hs, :] = jnp.concatenate([zero, hp], axis=0).astype(BF16)
        h_ref[hs + tm:2 * hs + tm, :] = jnp.concatenate([hn, zero], axis=0).astype(BF16)
        o_ref[...] = x_ref[...]

    h = h_ref[...]

    def conv(u, k):
        w = cw_ref[:, k * tf:(k + 1) * tf]
        return (cb_ref[:, k * tf:(k + 1) * tf] + u[hs - 1:hs - 1 + tm] * w[0:1]
                + u[hs:hs + tm] * w[1:2] + u[hs + 1:hs + 1 + tm] * w[2:3])

    a = conv(jnp.dot(h, wa_ref[...], preferred_element_type=F32), 0)
    gt = conv(jnp.dot(h, wg_ref[...], preferred_element_type=F32), 1)
    act = (a * (gt * jax.nn.sigmoid(gt))).astype(BF16)
    o_ref[...] += mod_ref[5:6, :] * jnp.dot(act, wd_ref[...], preferred_element_type=F32)


def conv_ffn(x, mod, gain, w_up, conv_w, conv_b, w_down, tm, tf=512):
    g, l, d = x.shape
    f = w_down.shape[0]
    nf = f // tf
    nb = tm // SUBLANES
    last = l // SUBLANES - 1
    cw = conv_w.reshape(3, 2, nf, tf).transpose(2, 0, 1, 3).reshape(nf, 3, 2 * tf)
    cb = conv_b.reshape(1, 2, nf, tf).transpose(2, 0, 1, 3).reshape(nf, 1, 2 * tf)
    return pl.pallas_call(
        _ffn_kernel,
        grid=(g, l // tm, nf),
        in_specs=[pl.BlockSpec((None, tm, d), lambda b, i, j: (b, i, 0)),
                  pl.BlockSpec((None, SUBLANES, d), lambda b, i, j: (b, jnp.maximum(i * nb - 1, 0), 0)),
                  pl.BlockSpec((None, SUBLANES, d), lambda b, i, j: (b, jnp.minimum((i + 1) * nb, last), 0)),
                  pl.BlockSpec((None, MOD_ROWS, d), lambda b, i, j: (b, 0, 0)),
                  pl.BlockSpec((1, d), lambda b, i, j: (0, 0)),
                  pl.BlockSpec((d, tf), lambda b, i, j: (0, j)),
                  pl.BlockSpec((d, tf), lambda b, i, j: (0, nf + j)),
                  pl.BlockSpec((None, 3, 2 * tf), lambda b, i, j: (j, 0, 0)),
                  pl.BlockSpec((None, 1, 2 * tf), lambda b, i, j: (j, 0, 0)),
                  pl.BlockSpec((tf, d), lambda b, i, j: (j, 0))],
        out_specs=pl.BlockSpec((None, tm, d), lambda b, i, j: (b, i, 0)),
        out_shape=jax.ShapeDtypeStruct((g, l, d), F32),
        scratch_shapes=[pltpu.VMEM((tm + 2 * FFN_HALO, d), BF16)],
        compiler_params=_cparams(("arbitrary", "arbitrary", "arbitrary")),
        name="conv_ffn",
    )(x, x, x, mod, gain, w_up, w_up, cw, cb, w_down)


def natten_bias_table(rpb, n_rows):
    kr = min(NA_KR_MAX, n_rows)
    col = np.arange(GRID_W)
    col0 = np.clip(col - NA_KC // 2, 0, GRID_W - NA_KC)
    in_win = (col[None, :] >= col0[:, None]) & (col[None, :] < col0[:, None] + NA_KC)
    c_idx = np.clip(col[None, :] - col[:, None] + NA_KC - 1, 0, 2 * NA_KC - 2)
    r_idx = np.arange(kr)[None, :] - np.arange(kr)[:, None] + NA_KR_MAX - 1
    tab = rpb.astype(F32)[:, r_idx[:, None, :, None], c_idx[None, :, None, :]]
    tab = jnp.where(in_win[None, None, :, None, :], tab, -jnp.inf)
    return tab.reshape(rpb.shape[0], kr, GRID_W, kr * GRID_W)


def _rms(x, gain):
    return x * lax.rsqrt(jnp.mean(x * x, axis=-1, keepdims=True) + EPS) * gain


def _dot_nt(a, b):
    return lax.dot_general(a, b, (((1,), (1,)), ((), ())), preferred_element_type=F32)


def _natten_kernel(*refs, n_rows, with_ctx_out):
    if with_ctx_out:
        (q_ref, k_ref, v_ref, kc_ref, vc_ref, gq_ref, gk_ref, bias_ref, qc_ref,
         o_ref, oc_ref, qs, ks, vs) = refs
    else:
        q_ref, k_ref, v_ref, kc_ref, vc_ref, gq_ref, gk_ref, bias_ref, o_ref, qs, ks, vs = refs
    dh = q_ref.shape[-1]
    kr = min(NA_KR_MAX, n_rows)
    scale = dh ** -0.5
    qs[...] = (_rms(q_ref[...], gq_ref[...]) * scale).astype(BF16)
    ks[...] = _rms(k_ref[...], gk_ref[...]).astype(BF16)
    vs[...] = v_ref[...].astype(BF16)
    kc = _rms(kc_ref[...], gk_ref[...]).astype(BF16)
    vc = vc_ref[...].astype(BF16)

    def row(r, carry):
        ws = jnp.clip(r - kr // 2, 0, n_rows - kr)
        q = qs[pl.ds(pl.multiple_of(r * GRID_W, GRID_W), GRID_W), :]
        k0 = pl.multiple_of(ws * GRID_W, GRID_W)
        s_loc = _dot_nt(q, ks[pl.ds(k0, kr * GRID_W), :]) + bias_ref[r - ws]
        s_ctx = _dot_nt(q, kc)
        m = jnp.maximum(jnp.max(s_loc, axis=-1, keepdims=True), jnp.max(s_ctx, axis=-1, keepdims=True))
        p_loc = jnp.exp(s_loc - m)
        p_ctx = jnp.exp(s_ctx - m)
        den = jnp.sum(p_loc, axis=-1, keepdims=True) + jnp.sum(p_ctx, axis=-1, keepdims=True)
        o = (jnp.dot(p_loc.astype(BF16), vs[pl.ds(k0, kr * GRID_W), :], preferred_element_type=F32)
             + jnp.dot(p_ctx.astype(BF16), vc, preferred_element_type=F32))
        o_ref[pl.ds(pl.multiple_of(r * GRID_W, GRID_W), GRID_W), :] = (o / den).astype(o_ref.dtype)
        return carry

    lax.fori_loop(0, n_rows, row, 0)

    if with_ctx_out:
        qc = (_rms(qc_ref[...], gq_ref[...]) * scale).astype(BF16)
        s = _dot_nt(qc, kc)
        p = jnp.exp(s - jnp.max(s, axis=-1, keepdims=True))
        o = jnp.dot(p.astype(BF16), vc, preferred_element_type=F32) / jnp.sum(p, axis=-1, keepdims=True)
        oc_ref[...] = o.astype(oc_ref.dtype)


def natten(px, pc, col_q, gq, gk, bias, with_ctx_out):
    bsz, l, _ = px.shape
    lc = pc.shape[1]
    h, kr, _, nk = bias.shape
    dh = gq.shape[-1]
    cb = col_q // dh

    def head_spec(rows, which):
        return pl.BlockSpec((None, rows, dh), lambda b, hh: (b, 0, cb + which * h + hh))

    in_specs = [head_spec(l, 0), head_spec(l, 1), head_spec(l, 2), head_spec(lc, 1), head_spec(lc, 2),
                pl.BlockSpec((1, dh), lambda b, hh: (0, 0)), pl.BlockSpec((1, dh), lambda b, hh: (0, 0)),
                pl.BlockSpec((None, kr, GRID_W, nk), lambda b, hh: (hh, 0, 0, 0))]
    args = [px, px, px, pc, pc, gq, gk, bias]
    out_specs = [pl.BlockSpec((None, l, dh), lambda b, hh: (b, 0, hh))]
    out_shape = [jax.ShapeDtypeStruct((bsz, l, h * dh), BF16)]
    if with_ctx_out:
        in_specs.append(head_spec(lc, 0))
        args.append(pc)
        out_specs.append(pl.BlockSpec((None, lc, dh), lambda b, hh: (b, 0, hh)))
        out_shape.append(jax.ShapeDtypeStruct((bsz, lc, h * dh), BF16))
    outs = pl.pallas_call(
        functools.partial(_natten_kernel, n_rows=l // GRID_W, with_ctx_out=with_ctx_out),
        grid=(bsz, h),
        in_specs=in_specs, out_specs=out_specs, out_shape=out_shape,
        scratch_shapes=[pltpu.VMEM((l, dh), BF16)] * 3,
        compiler_params=_cparams(("arbitrary", "arbitrary")),
        name="natten",
    )(*args)
    return outs if with_ctx_out else (outs[0], None)


def _cos_sin(n, period):
    k = np.arange(n, dtype=np.int64)
    ang = (2.0 * np.pi / period) * ((k[:, None] * k[None, :]) % period)
    return np.cos(ang), np.sin(ang)


def _fourier_kernel(u_ref, w_ref, cd_ref, sd_ref, cs_ref, o_ref, wc_ref, ws_ref, v_ref, *, scale):
    l, c = u_ref.shape
    dg = cd_ref.shape[0]

    @pl.when(pl.program_id(0) == 0)
    def _():
        wc_ref[...] = jnp.zeros_like(wc_ref)
        ws_ref[...] = jnp.zeros_like(ws_ref)
        for g in range(c // dg):
            sl = slice(g * dg, (g + 1) * dg)
            wg = w_ref[g]
            wc_ref[sl, sl] = jnp.dot(cd_ref[...], wg, precision=lax.Precision.HIGHEST,
                                     preferred_element_type=F32).astype(BF16)
            ws_ref[sl, sl] = jnp.dot(sd_ref[...], wg, precision=lax.Precision.HIGHEST,
                                     preferred_element_type=F32).astype(BF16)

    u = u_ref[...].astype(BF16)
    v_ref[0:l, :] = jnp.dot(u, wc_ref[...], preferred_element_type=F32).astype(BF16)
    v_ref[l:2 * l, :] = jnp.dot(u, ws_ref[...], preferred_element_type=F32).astype(BF16)
    y = jnp.dot(cs_ref[...], v_ref[...], preferred_element_type=F32)
    o_ref[...] = (y * scale).astype(o_ref.dtype)


def fourier_mix(p, col, w):
    bsz, l, _ = p.shape
    g, dg, _ = w.shape
    c = g * dg
    cl, sl = _cos_sin(l, l)
    cd, sd = _cos_sin(dg, dg)
    cs = jnp.asarray(np.concatenate([cl, -sl], axis=1), BF16)
    return pl.pallas_call(
        functools.partial(_fourier_kernel, scale=float((l * dg) ** -0.5)),
        grid=(bsz,),
        in_specs=[pl.BlockSpec((None, l, c), lambda b: (b, 0, col // c)),
                  pl.BlockSpec((g, dg, dg), lambda b: (0, 0, 0)),
                  pl.BlockSpec((dg, dg), lambda b: (0, 0)),
                  pl.BlockSpec((dg, dg), lambda b: (0, 0)),
                  pl.BlockSpec((l, 2 * l), lambda b: (0, 0), pipeline_mode=pl.Buffered(1))],
        out_specs=pl.BlockSpec((None, l, c), lambda b: (b, 0, 0)),
        out_shape=jax.ShapeDtypeStruct((bsz, l, c), BF16),
        scratch_shapes=[pltpu.VMEM((c, c), BF16), pltpu.VMEM((c, c), BF16), pltpu.VMEM((2 * l, c), BF16)],
        compiler_params=_cparams(("arbitrary",)),
        name="fourier_mix",
    )(p, w, jnp.asarray(cd, F32), jnp.asarray(sd, F32), cs)


HY_CBLK = 256


def _hyena_dft_matrix(l):
    k = np.arange(l, dtype=np.int64)
    ang = (np.pi / l) * ((k[:, None] * k[None, :]) % (2 * l))
    sn = np.sin(ang)
    sn[0, :] = 1.0 - 2.0 * (k % 2)
    return jnp.asarray(np.concatenate([np.cos(ang), sn], axis=0), BF16)


def _hyena_pos_features(l):
    t = np.linspace(0.0, 1.0, l)[:, None]
    w = (2.0 * np.pi / l) * np.arange(l)[:, None]
    f = np.linspace(1e-4, HY_BANDS - 1, HY_BANDS)[None, :]
    z = np.concatenate([t, np.cos(f * w), -np.sin(f * w)], axis=-1)
    return np.pad(z, ((0, 0), (0, LANES - z.shape[1])))


def _split_bf16(x):
    hi = x.astype(BF16)
    return hi, (x - hi.astype(F32)).astype(BF16)


def _hyena_filter_kernel(z_ref, w1_ref, b1_ref, w2_ref, b2_ref, fr_ref, w3f_ref, w3b_ref, dl_ref, m_ref,
                         p_ref, q_ref):
    l = z_ref.shape[0]
    hp = lax.Precision.HIGHEST
    z = z_ref[...]
    h = jnp.sin(fr_ref[0:1, :] * (jnp.dot(z, w1_ref[...], precision=hp, preferred_element_type=F32)
                                  + b1_ref[...]))
    h = jnp.sin(fr_ref[1:2, :] * (jnp.dot(h, w2_ref[...], precision=hp, preferred_element_type=F32)
                                  + b2_ref[...]))
    window = jnp.exp(-z[:, 0:1] * dl_ref[...]) + HY_DECAY_SHIFT
    hf = jnp.dot(h, w3f_ref[...], precision=hp, preferred_element_type=F32) * window
    hb = jnp.dot(h, w3b_ref[...], precision=hp, preferred_element_type=F32) * window
    norm = jnp.sum(jnp.abs(hf) + jnp.abs(hb), axis=0, keepdims=True) + EPS
    hf = hf / norm
    hb = hb / norm
    g1h, g1l = _split_bf16(hf + hb)
    g2h, g2l = _split_bf16(hb - hf)
    f1 = (jnp.dot(m_ref[...], g1h, preferred_element_type=F32)
          + jnp.dot(m_ref[...], g1l, preferred_element_type=F32))
    f2 = (jnp.dot(m_ref[l:2 * l, :], g2h, preferred_element_type=F32)
          + jnp.dot(m_ref[l:2 * l, :], g2l, preferred_element_type=F32))
    p_ref[...] = f1[0:l]
    row = lax.broadcasted_iota(jnp.int32, f2.shape, 0)
    q_ref[...] = jnp.where(row == 0, f1[l:l + 1], f2)


def hyena_filter_spectrum(l, w1, b1, w2, b2, w3, freq, m):
    c = w3.shape[1] // 2
    hid = w1.shape[1]
    z = jnp.asarray(_hyena_pos_features(l), F32)
    w1p = jnp.pad(w1, ((0, z.shape[1] - w1.shape[0]), (0, 0)))
    deltas = np.abs(np.linspace(math.log(HY_DECAY_TARGET) / HY_SLOW_PCT,
                                math.log(HY_DECAY_TARGET) / HY_FAST_PCT, c))[None, :]
    nb = c // HY_CBLK
    full = lambda shape: pl.BlockSpec(shape, lambda j: (0,) * len(shape))
    return pl.pallas_call(
        _hyena_filter_kernel,
        grid=(nb,),
        in_specs=[full(z.shape), full(w1p.shape), full((1, hid)), full(w2.shape), full((1, hid)),
                  full((2, hid)),
                  pl.BlockSpec((hid, HY_CBLK), lambda j: (0, j)),
                  pl.BlockSpec((hid, HY_CBLK), lambda j: (0, nb + j)),
                  pl.BlockSpec((1, HY_CBLK), lambda j: (0, j)),
                  pl.BlockSpec(m.shape, lambda j: (0, 0), pipeline_mode=pl.Buffered(1))],
        out_specs=[pl.BlockSpec((l, HY_CBLK), lambda j: (0, j))] * 2,
        out_shape=[jax.ShapeDtypeStruct((l, c), F32)] * 2,
        compiler_params=_cparams(("arbitrary",)),
        name="hyena_filter",
    )(z, w1p, b1.reshape(1, hid), w2, b2.reshape(1, hid), freq, w3, w3, jnp.asarray(deltas, F32), m)


def _shift_rows(u, down):
    l = u.shape[0]
    row = lax.broadcasted_iota(jnp.int32, u.shape, 0)
    if down:
        return jnp.where(row == 0, 0.0, pltpu.roll(u, 1, 0))
    return jnp.where(row == l - 1, 0.0, pltpu.roll(u, l - 1, 0))


def _dwconv3(u, taps):
    return (taps[3:4] + _shift_rows(u, True) * taps[0:1] + u * taps[1:2] + _shift_rows(u, False) * taps[2:3])


def _hyena_conv_kernel(x0_ref, x1_ref, v_ref, taps_ref, bias_ref, p_ref, q_ref, m_ref, o_ref):
    l = x0_ref.shape[0]
    x1 = _dwconv3(x1_ref[...], taps_ref[1])
    s = _dwconv3(v_ref[...], taps_ref[2]) * x1
    f = jnp.dot(m_ref[...], s.astype(BF16), preferred_element_type=F32)
    a, b = f[0:l], f[l:2 * l]
    p, q = p_ref[...], q_ref[...]
    first = lax.broadcasted_iota(jnp.int32, a.shape, 0) == 0
    bq = b * q
    y_cos = (a * p + jnp.where(first, 0.0, bq)) * jnp.where(first, 0.5 / l, 1.0 / l)
    y_sin = jnp.where(first, 0.0, (b * p - a * q) * (1.0 / l))
    y = (jnp.dot(m_ref[0:l, :], y_cos.astype(BF16), preferred_element_type=F32)
         + jnp.where(first, 0.0, jnp.dot(m_ref[l:2 * l, :], y_sin.astype(BF16), preferred_element_type=F32)))
    parity = lax.broadcasted_iota(jnp.int32, a.shape, 0) % 2
    y = y + jnp.where(parity == 0, 1.0, -1.0) * (bq[0:1] * (0.5 / l))
    x0 = _dwconv3(x0_ref[...], taps_ref[0])
    o_ref[...] = ((y + s * bias_ref[...]) * x0).astype(o_ref.dtype)


def hyena_conv(p, col, conv_w, conv_b, bias, spec_p, spec_q, m):
    bsz, l, _ = p.shape
    c = bias.shape[-1]
    nb = c // HY_CBLK
    cb0 = col // HY_CBLK
    taps = jnp.concatenate([conv_w, conv_b[None]], axis=0)
    taps = taps.reshape(4, 3, nb, HY_CBLK).transpose(2, 1, 0, 3)

    def part(k):
        return pl.BlockSpec((None, l, HY_CBLK), lambda j, b: (b, 0, cb0 + k * nb + j))

    return pl.pallas_call(
        _hyena_conv_kernel,
        grid=(nb, bsz),
        in_specs=[part(0), part(1), part(2),
                  pl.BlockSpec((None, 3, 4, HY_CBLK), lambda j, b: (j, 0, 0, 0)),
                  pl.BlockSpec((1, HY_CBLK), lambda j, b: (0, j)),
                  pl.BlockSpec((l, HY_CBLK), lambda j, b: (0, j)),
                  pl.BlockSpec((l, HY_CBLK), lambda j, b: (0, j)),
                  pl.BlockSpec(m.shape, lambda j, b: (0, 0), pipeline_mode=pl.Buffered(1))],
        out_specs=pl.BlockSpec((None, l, HY_CBLK), lambda j, b: (b, 0, j)),
        out_shape=jax.ShapeDtypeStruct((bsz, l, c), BF16),
        compiler_params=_cparams(("arbitrary", "arbitrary")),
        name="hyena_conv",
    )(p, p, p, taps, bias.reshape(1, c), spec_p, spec_q, m)


GLA_HP = 2
GLA_ROPE_PAIR = 16


def _gla_rope_tables(l, dk):
    half = dk // 2
    nf = half // 2
    assert nf == GLA_ROPE_PAIR
    inv = ROPE_THETA ** (-np.arange(nf, dtype=np.float64) / nf)
    t = np.arange(l)
    ang_r = (t // GRID_W)[:, None] * inv
    ang_c = (t % GRID_W)[:, None] * inv
    cos = np.concatenate([np.cos(ang_r)] * 2 + [np.cos(ang_c)] * 2, axis=1)
    sin = np.concatenate([-np.sin(ang_r), np.sin(ang_r), -np.sin(ang_c), np.sin(ang_c)], axis=1)
    return (jnp.asarray(np.tile(cos, (1, GLA_HP)), F32), jnp.asarray(np.tile(sin, (1, GLA_HP)), F32))


def _rope(x, cos, sin):
    lane = lax.broadcasted_iota(jnp.int32, x.shape, 1)
    lanes = x.shape[1]
    partner = jnp.where(lane % (2 * GLA_ROPE_PAIR) < GLA_ROPE_PAIR,
                        pltpu.roll(x, lanes - GLA_ROPE_PAIR, 1), pltpu.roll(x, GLA_ROPE_PAIR, 1))
    return x * cos + partner * sin


def _log_sigmoid(x):
    return jnp.minimum(x, 0.0) - jnp.log(1.0 + jnp.exp(-jnp.abs(x)))


def _gla_kernel(*refs, with_ctx_out):
    (q_ref, k_ref, v_ref, r_ref, z_ref, cq_ref, ck_ref, cv_ref, cr_ref, cz_ref,
     wz_ref, bz_ref, g_ref, cos_ref, sin_ref) = refs[:15]
    if with_ctx_out:
        o_ref, oc_ref = refs[15:17]
        scratch = refs[17:]
    else:
        o_ref, oc_ref = refs[15], None
        scratch = refs[16:]
    qs, ks, las, ofs, obs, cqs, cks, clas, cofs, cobs, st_f, st_b = scratch
    dk2 = q_ref.shape[1]
    dv2 = v_ref.shape[1]
    dk, dv = dk2 // GLA_HP, dv2 // GLA_HP
    ch = GLA_CHUNK
    hp = lax.Precision.HIGHEST

    def gates(z):
        pre = jnp.dot(z, wz_ref[...], precision=hp, preferred_element_type=F32) + bz_ref[...]
        return _log_sigmoid(pre) * (1.0 / GLA_TAU)

    qs[...] = _rope(q_ref[...] * dk ** -0.5, cos_ref[...], sin_ref[...])
    ks[...] = _rope(k_ref[...], cos_ref[...], sin_ref[...])
    las[...] = gates(z_ref[...])
    cqs[...] = cq_ref[...] * dk ** -0.5
    cks[...] = ck_ref[...]
    clas[...] = gates(cz_ref[...])
    st_f[...] = jnp.zeros_like(st_f)
    st_b[...] = jnp.zeros_like(st_b)

    ri = lax.broadcasted_iota(jnp.int32, (ch, ch), 0)
    ci = lax.broadcasted_iota(jnp.int32, (ch, ch), 1)
    tri = {False: ri >= ci, True: ri <= ci}
    tri_b16 = {d: jnp.where(m, 1.0, 0.0).astype(BF16) for d, m in tri.items()}
    tri2 = {d: jnp.concatenate([m] * GLA_HP, axis=0) for d, m in tri.items()}
    lane_head = lax.broadcasted_iota(jnp.int32, (ch, dk2), 1) // dk
    st_r = lax.broadcasted_iota(jnp.int32, (dv2, dk2), 0) // dv
    st_c = lax.broadcasted_iota(jnp.int32, (dv2, dk2), 1) // dk
    st_diag = st_r == st_c

    def chunk(q_s, k_s, la_s, v_in, o_s, st, c, backward):
        rows = pl.ds(pl.multiple_of(c * ch, ch), ch)
        q_c, k_c = q_s[rows, :], k_s[rows, :]
        v_c = v_in[rows, :].astype(BF16)
        la_c = la_s[rows, dk2:2 * dk2] if backward else la_s[rows, 0:dk2]
        la_hi, la_lo = _split_bf16(la_c)
        cum = (jnp.dot(tri_b16[backward], la_hi, preferred_element_type=F32)
               + jnp.dot(tri_b16[backward], la_lo, preferred_element_type=F32))
        tot = cum[0:1] if backward else cum[ch - 1:ch]
        q_dec = q_c * jnp.exp(cum)
        k_inv = (k_c * jnp.exp(-cum)).astype(BF16)
        k_end = (k_c * jnp.exp(tot - cum)).astype(BF16)
        q_heads = jnp.concatenate([jnp.where(lane_head == h, q_dec, 0.0) for h in range(GLA_HP)], axis=0)
        sc = jnp.where(tri2[backward], _dot_nt(q_heads.astype(BF16), k_inv), 0.0)
        pv = jnp.dot(sc.astype(BF16), v_c, preferred_element_type=F32)
        o_intra = jnp.concatenate([pv[h * ch:(h + 1) * ch, h * dv:(h + 1) * dv] for h in range(GLA_HP)],
                                  axis=1)
        s_t = st[...]
        o_s[rows, :] = o_intra + _dot_nt(q_dec.astype(BF16), s_t.astype(BF16))
        ds_t = lax.dot_general(v_c, k_end, (((0,), (0,)), ((), ())), preferred_element_type=F32)
        st[...] = s_t * jnp.exp(tot) + jnp.where(st_diag, ds_t, 0.0)

    def scan(q_s, k_s, la_s, v_in, of_s, ob_s, n):
        def body(c, carry):
            chunk(q_s, k_s, la_s, v_in, of_s, st_f, c, False)
            chunk(q_s, k_s, la_s, v_in, ob_s, st_b, n - 1 - c, True)
            return carry
        lax.fori_loop(0, n, body, 0)

    def finish(of_s, ob_s, gate_ref, out_ref, n_blocks, blk):
        def body(i, carry):
            rows = pl.ds(pl.multiple_of(i * blk, blk), blk)
            o = of_s[rows, :] + ob_s[rows, :]
            gate = gate_ref[rows, :]
            parts = []
            for h in range(GLA_HP):
                oh = o[:, h * dv:(h + 1) * dv]
                gh = gate[:, h * dv:(h + 1) * dv]
                parts.append(_rms(oh, g_ref[...]) * (gh * jax.nn.sigmoid(gh)))
            out_ref[rows, :] = jnp.concatenate(parts, axis=1).astype(out_ref.dtype)
            return carry
        lax.fori_loop(0, n_blocks, body, 0)

    l, lc = q_ref.shape[0], cq_ref.shape[0]
    scan(cqs, cks, clas, cv_ref, cofs, cobs, lc // ch)
    if with_ctx_out:
        finish(cofs, cobs, cr_ref, oc_ref, 1, lc)
    scan(qs, ks, las, v_ref, ofs, obs, l // ch)
    finish(ofs, obs, r_ref, o_ref, l // lc, lc)


def gla(px, pxz, pc, pcz, w_gate, b_gate, out_gain, with_ctx_out):
    bsz, l, _ = px.shape
    lc = pc.shape[1]
    dv = out_gain.shape[-1]
    hdk = w_gate.shape[-1]
    dk = hdk // GLA_HEADS
    dk2, dv2 = GLA_HP * dk, GLA_HP * dv
    nhp = GLA_HEADS // GLA_HP
    zw = pxz.shape[-1]
    rank = w_gate.shape[1]
    wz = jnp.zeros((nhp, zw, 2 * dk2), F32)
    for u in range(2):
        blk = w_gate[u].reshape(rank, nhp, dk2).transpose(1, 0, 2)
        wz = wz.at[:, u * rank:(u + 1) * rank, u * dk2:(u + 1) * dk2].set(blk)
    bz = b_gate.reshape(2, nhp, dk2).transpose(1, 0, 2).reshape(nhp, 1, 2 * dk2)
    cos, sin = _gla_rope_tables(l, dk)
    k_cb, v_cb, r_cb = hdk // dk2, 2 * hdk // dv2, (2 * hdk + GLA_HEADS * dv) // dv2

    def col(rows, width, cb):
        return pl.BlockSpec((None, rows, width), lambda b, j: (b, 0, cb + j))

    def whole(rows, width):
        return pl.BlockSpec((None, rows, width), lambda b, j: (b, 0, 0))

    const = lambda shape: pl.BlockSpec(shape, lambda b, j: (0,) * len(shape))
    in_specs = [col(l, dk2, 0), col(l, dk2, k_cb), col(l, dv2, v_cb), col(l, dv2, r_cb), whole(l, zw),
                col(lc, dk2, 0), col(lc, dk2, k_cb), col(lc, dv2, v_cb), col(lc, dv2, r_cb), whole(lc, zw),
                pl.BlockSpec((None, zw, 2 * dk2), lambda b, j: (j, 0, 0)),
                pl.BlockSpec((None, 1, 2 * dk2), lambda b, j: (j, 0, 0)),
                const((1, dv)), const((l, dk2)), const((l, dk2))]
    out_specs = [pl.BlockSpec((None, l, dv2), lambda b, j: (b, 0, j))]
    out_shape = [jax.ShapeDtypeStruct((bsz, l, GLA_HEADS * dv), BF16)]
    if with_ctx_out:
        out_specs.append(pl.BlockSpec((None, lc, dv2), lambda b, j: (b, 0, j)))
        out_shape.append(jax.ShapeDtypeStruct((bsz, lc, GLA_HEADS * dv), BF16))
    scratch = [pltpu.VMEM((l, dk2), F32), pltpu.VMEM((l, dk2), F32), pltpu.VMEM((l, 2 * dk2), F32),
               pltpu.VMEM((l, dv2), F32), pltpu.VMEM((l, dv2), F32),
               pltpu.VMEM((lc, dk2), F32), pltpu.VMEM((lc, dk2), F32), pltpu.VMEM((lc, 2 * dk2), F32),
               pltpu.VMEM((lc, dv2), F32), pltpu.VMEM((lc, dv2), F32),
               pltpu.VMEM((dv2, dk2), F32), pltpu.VMEM((dv2, dk2), F32)]
    outs = pl.pallas_call(
        functools.partial(_gla_kernel, with_ctx_out=with_ctx_out),
        grid=(bsz, nhp),
        in_specs=in_specs, out_specs=out_specs, out_shape=out_shape, scratch_shapes=scratch,
        compiler_params=_cparams(("arbitrary", "arbitrary")),
        name="gla",
    )(px, px, px, px, pxz, pc, pc, pc, pc, pcz, wz, bz, out_gain.reshape(1, dv), cos, sin)
    return outs if with_ctx_out else (outs[0], None)


def kernel(x, c, ctx, c_ctx, w_mod, b_mod, g_mix, w_in, gla_gate_w, gla_gate_b, gla_out_g,
           hy_conv_w, hy_conv_b, hy_w1, hy_b1, hy_w2, hy_b2, hy_w3, hy_freq, hy_bias, fn_w,
           na_q_g, na_k_g, na_rpb, w_out, g_ffn, ffn_w_up, ffn_conv_w, ffn_conv_b, ffn_w_down):
    bsz, l_lat, d = x.shape
    l_ctx = ctx.shape[1]
    depth = w_mod.shape[0]
    w = d // 4

    n_cond = -(-(bsz + 1) // SUBLANES) * SUBLANES
    cond = jnp.zeros((n_cond, d), F32).at[:bsz].set(c).at[bsz].set(c_ctx)
    mods = mod_vectors(cond, w_mod, b_mod).reshape(depth, n_cond, N_MOD, d)
    mods = jnp.pad(mods, ((0, 0), (0, 0), (0, MOD_ROWS - N_MOD), (0, 0)))

    z0 = 3 * w
    zw = 2 * GLA_GATE_RANK
    w_main = jnp.concatenate([w_in[:, :, :z0], w_in[:, :, z0 + zw:]], axis=-1).astype(BF16)
    w_z = jnp.pad(w_in[:, :, z0:z0 + zw], ((0, 0), (0, 0), (0, LANES - zw))).astype(BF16)
    w_out_b = w_out.astype(BF16)
    w_up_b = ffn_w_up.astype(BF16)
    w_down_b = ffn_w_down.astype(BF16)

    m_lat = _hyena_dft_matrix(l_lat)
    m_ctx = _hyena_dft_matrix(l_ctx)
    ctx_flat = ctx.reshape(1, bsz * l_ctx, d)
    for layer in range(depth):
        last = layer == depth - 1
        mod_x = mods[layer, :bsz]
        mod_c = mods[layer, bsz:bsz + 1]
        g_mix_l = g_mix[layer].reshape(1, d)
        g_ffn_l = g_ffn[layer].reshape(1, d)

        px, pxz = in_proj(x, mod_x, g_mix_l, w_main[layer], w_z[layer], tm=1024)
        pc, pcz = in_proj(ctx_flat, mod_c, g_mix_l, w_main[layer], w_z[layer], tm=1024)
        pc = pc.reshape(bsz, l_ctx, -1)
        pcz = pcz.reshape(bsz, l_ctx, -1)

        y_a, yc_a = gla(px, pxz, pc, pcz, gla_gate_w[layer], gla_gate_b[layer], gla_out_g[layer],
                        with_ctx_out=not last)
        hy_filt = (hy_w1[layer], hy_b1[layer], hy_w2[layer], hy_b2[layer], hy_w3[layer], hy_freq[layer])
        hy_p, hy_q = hyena_filter_spectrum(l_lat, *hy_filt, m_lat)
        y_b = hyena_conv(px, 3 * w, hy_conv_w[layer], hy_conv_b[layer], hy_bias[layer], hy_p, hy_q, m_lat)
        y_c = fourier_mix(px, 6 * w, fn_w[layer])
        na_bias = natten_bias_table(na_rpb[layer], l_lat // GRID_W)
        y_d, yc_d = natten(px, pc, 7 * w, na_q_g[layer].reshape(1, -1), na_k_g[layer].reshape(1, -1),
                           na_bias, with_ctx_out=not last)
        x = out_proj([y_a, y_b, y_c, y_d], w_out_b[layer], x, mod_x, tm=512)
        cw = ffn_conv_w[layer]
        cb = ffn_conv_b[layer].reshape(1, -1)
        x = conv_ffn(x, mod_x, g_ffn_l, w_up_b[layer], cw, cb, w_down_b[layer], tm=512)

        if not last:
            hc_p, hc_q = hyena_filter_spectrum(l_ctx, *hy_filt, m_ctx)
            yc_b = hyena_conv(pc, 3 * w, hy_conv_w[layer], hy_conv_b[layer], hy_bias[layer], hc_p, hc_q, m_ctx)
            yc_c = fourier_mix(pc, 6 * w, fn_w[layer])
            ycs = [y.reshape(1, bsz * l_ctx, w) for y in (yc_a, yc_b, yc_c, yc_d)]
            ctx_flat = out_proj(ycs, w_out_b[layer], ctx_flat, mod_c, tm=512)
            mod_cb = jnp.broadcast_to(mod_c, (bsz, MOD_ROWS, d))
            ctx_flat = conv_ffn(ctx_flat.reshape(bsz, l_ctx, d), mod_cb, g_ffn_l, w_up_b[layer], cw, cb,
                                w_down_b[layer], tm=l_ctx).reshape(1, bsz * l_ctx, d)
    return x
```

```python
import functools
import math

import numpy as np
import jax
import jax.numpy as jnp
from jax import lax
from jax.experimental import pallas as pl
from jax.experimental.pallas import tpu as pltpu

F32 = jnp.float32
BF16 = jnp.bfloat16

GRID_W = 64
GLA_HEADS = 4
GLA_GATE_RANK = 16
GLA_TAU = 16.0
GLA_CHUNK = 64
HY_BANDS = 16
HY_DECAY_TARGET = 1e-2
HY_FAST_PCT = 0.3
HY_SLOW_PCT = 1.5
HY_DECAY_SHIFT = 0.05
FN_GROUPS = 4
NA_HEADS = 4
NA_KR_MAX = 8
NA_KC = 16
ROPE_THETA = 10000.0
N_MOD = 6
EPS = 1e-6

V7X_VMEM_LIMIT = 56 * 1024 * 1024
SUBLANES = 8
LANES = 128
MOD_ROWS = 8
FFN_HALO = 16


def _cparams(sem):
    return pltpu.CompilerParams(dimension_semantics=sem, vmem_limit_bytes=V7X_VMEM_LIMIT)


def _mod_kernel(s_ref, w_ref, b_ref, o_ref):
    s = s_ref[...]
    s = s * jax.nn.sigmoid(s)
    o_ref[...] = jnp.dot(s.astype(BF16), w_ref[...].astype(BF16),
                         preferred_element_type=F32) + b_ref[...]


def mod_vectors(cond, w_mod, b_mod, tn=1024):
    depth, d, n = w_mod.shape
    r = cond.shape[0]
    return pl.pallas_call(
        _mod_kernel,
        grid=(depth, n // tn),
        in_specs=[pl.BlockSpec((r, d), lambda l, j: (0, 0)),
                  pl.BlockSpec((None, d, tn), lambda l, j: (l, 0, j)),
                  pl.BlockSpec((None, 1, tn), lambda l, j: (l, 0, j))],
        out_specs=pl.BlockSpec((None, r, tn), lambda l, j: (l, 0, j)),
        out_shape=jax.ShapeDtypeStruct((depth, r, n), F32),
        compiler_params=_cparams(("arbitrary", "arbitrary")),
        name="mod_vectors",
    )(cond, w_mod, b_mod.reshape(depth, 1, n))


def _norm_mod(x, gain, shift, scale):
    ms = jnp.mean(x * x, axis=-1, keepdims=True)
    return (x * lax.rsqrt(ms + EPS) * gain) * (1.0 + scale) + shift


def _inproj_kernel(x_ref, mod_ref, g_ref, w_ref, wz_ref, o_ref, oz_ref, h_ref):
    @pl.when(pl.program_id(2) == 0)
    def _():
        h = _norm_mod(x_ref[...], g_ref[...], mod_ref[0:1, :], mod_ref[1:2, :]).astype(BF16)
        h_ref[...] = h
        oz_ref[...] = jnp.dot(h, wz_ref[...], preferred_element_type=F32)

    o_ref[...] = jnp.dot(h_ref[...], w_ref[...], preferred_element_type=F32)


def in_proj(x, mod, gain, w_main, w_z, tm, tn=1024):
    g, l, d = x.shape
    n = w_main.shape[1]
    nz = w_z.shape[1]
    return pl.pallas_call(
        _inproj_kernel,
        grid=(g, l // tm, n // tn),
        in_specs=[pl.BlockSpec((None, tm, d), lambda b, i, j: (b, i, 0)),
                  pl.BlockSpec((None, MOD_ROWS, d), lambda b, i, j: (b, 0, 0)),
                  pl.BlockSpec((1, d), lambda b, i, j: (0, 0)),
                  pl.BlockSpec((d, tn), lambda b, i, j: (0, j)),
                  pl.BlockSpec((d, nz), lambda b, i, j: (0, 0))],
        out_specs=[pl.BlockSpec((None, tm, tn), lambda b, i, j: (b, i, j)),
                   pl.BlockSpec((None, tm, nz), lambda b, i, j: (b, i, 0))],
        out_shape=[jax.ShapeDtypeStruct((g, l, n), F32),
                   jax.ShapeDtypeStruct((g, l, nz), F32)],
        scratch_shapes=[pltpu.VMEM((tm, d), BF16)],
        compiler_params=_cparams(("arbitrary", "arbitrary", "arbitrary")),
        name="in_proj",
    )(x, mod, gain, w_main, w_z)


def _outproj_kernel(ya_ref, yb_ref, yc_ref, yd_ref, w_ref, x_ref, mod_ref, o_ref):
    kw = ya_ref.shape[-1]
    acc = jnp.dot(ya_ref[...], w_ref[0 * kw:1 * kw, :], preferred_element_type=F32)
    acc += jnp.dot(yb_ref[...], w_ref[1 * kw:2 * kw, :], preferred_element_type=F32)
    acc += jnp.dot(yc_ref[...], w_ref[2 * kw:3 * kw, :], preferred_element_type=F32)
    acc += jnp.dot(yd_ref[...], w_ref[3 * kw:4 * kw, :], preferred_element_type=F32)
    o_ref[...] = x_ref[...] + mod_ref[2:3, :] * acc


def out_proj(ys, w_out, x, mod, tm):
    g, l, d = x.shape
    kw = ys[0].shape[-1]
    yspec = pl.BlockSpec((None, tm, kw), lambda b, i: (b, i, 0))
    return pl.pallas_call(
        _outproj_kernel,
        grid=(g, l // tm),
        in_specs=[yspec, yspec, yspec, yspec,
                  pl.BlockSpec((4 * kw, d), lambda b, i: (0, 0)),
                  pl.BlockSpec((None, tm, d), lambda b, i: (b, i, 0)),
                  pl.BlockSpec((None, MOD_ROWS, d), lambda b, i: (b, 0, 0))],
        out_specs=pl.BlockSpec((None, tm, d), lambda b, i: (b, i, 0)),
        out_shape=jax.ShapeDtypeStruct((g, l, d), F32),
        compiler_params=_cparams(("arbitrary", "arbitrary")),
        name="out_proj",
    )(*ys, w_out, x, mod)


def _ffn_kernel(x_ref, xp_ref, xn_ref, mod_ref, g_ref, wa_ref, wg_ref, cw_ref, cb_ref, wd_ref,
                o_ref, h_ref, *, seg_len):
    i = pl.program_id(1)
    j = pl.program_id(2)
    tm = x_ref.shape[0]
    tf = wa_ref.shape[1]
    hs = FFN_HALO

    @pl.when(j == 0)
    def _():
        gain, shift, scale = g_ref[...], mod_ref[3:4, :], mod_ref[4:5, :]
        h_ref[hs:hs + tm, :] = _norm_mod(x_ref[...], gain, shift, scale).astype(BF16)
        hp = jnp.where(i > 0, _norm_mod(xp_ref[...], gain, shift, scale), 0.0)
        hn = jnp.where(i < pl.num_programs(1) - 1, _norm_mod(xn_ref[...], gain, shift, scale), 0.0)
        zero = jnp.zeros_like(hp)
        h_ref[0:hs, :] = jnp.concatenate([zero, hp], axis=0).astype(BF16)
        h_ref[hs + tm:2 * hs + tm, :] = jnp.concatenate([hn, zero], axis=0).astype(BF16)
        o_ref[...] = x_ref[...]

    h = h_ref[...]

    def conv(u, c0, width):
        w = cw_ref[:, c0:c0 + width]
        prev, nxt = u[hs - 1:hs - 1 + tm], u[hs + 1:hs + 1 + tm]
        if seg_len is not None:
            pos = lax.broadcasted_iota(jnp.int32, prev.shape, 0) % seg_len
            prev = jnp.where(pos == 0, 0.0, prev)
            nxt = jnp.where(pos == seg_len - 1, 0.0, nxt)
        return cb_ref[:, c0:c0 + width] + prev * w[0:1] + u[hs:hs + tm] * w[1:2] + nxt * w[2:3]

    a = conv(jnp.dot(h, wa_ref[...], preferred_element_type=F32), 0, tf)
    gt = conv(jnp.dot(h, wg_ref[...], preferred_element_type=F32), tf, tf)
    act = (a * (gt * jax.nn.sigmoid(gt))).astype(BF16)
    o_ref[...] += mod_ref[5:6, :] * jnp.dot(act, wd_ref[...], preferred_element_type=F32)


def conv_ffn(x, mod, gain, w_up, conv_w, conv_b, w_down, tm, tf=512, seg_len=None):
    g, l, d = x.shape
    assert seg_len is None or (tm % seg_len == 0 and l % tm == 0)
    f = w_down.shape[0]
    nf = f // tf
    nb = tm // SUBLANES
    last = l // SUBLANES - 1
    cw = conv_w.reshape(3, 2, nf, tf).transpose(2, 0, 1, 3).reshape(nf, 3, 2 * tf)
    cb = conv_b.reshape(1, 2, nf, tf).transpose(2, 0, 1, 3).reshape(nf, 1, 2 * tf)
    return pl.pallas_call(
        functools.partial(_ffn_kernel, seg_len=seg_len),
        grid=(g, l // tm, nf),
        in_specs=[pl.BlockSpec((None, tm, d), lambda b, i, j: (b, i, 0)),
                  pl.BlockSpec((None, SUBLANES, d), lambda b, i, j: (b, jnp.maximum(i * nb - 1, 0), 0)),
                  pl.BlockSpec((None, SUBLANES, d), lambda b, i, j: (b, jnp.minimum((i + 1) * nb, last), 0)),
                  pl.BlockSpec((None, MOD_ROWS, d), lambda b, i, j: (b, 0, 0)),
                  pl.BlockSpec((1, d), lambda b, i, j: (0, 0)),
                  pl.BlockSpec((d, tf), lambda b, i, j: (0, j)),
                  pl.BlockSpec((d, tf), lambda b, i, j: (0, nf + j)),
                  pl.BlockSpec((None, 3, 2 * tf), lambda b, i, j: (j, 0, 0)),
                  pl.BlockSpec((None, 1, 2 * tf), lambda b, i, j: (j, 0, 0)),
                  pl.BlockSpec((tf, d), lambda b, i, j: (j, 0))],
        out_specs=pl.BlockSpec((None, tm, d), lambda b, i, j: (b, i, 0)),
        out_shape=jax.ShapeDtypeStruct((g, l, d), F32),
        scratch_shapes=[pltpu.VMEM((tm + 2 * FFN_HALO, d), BF16)],
        compiler_params=_cparams(("arbitrary", "arbitrary", "arbitrary")),
        name="conv_ffn",
    )(x, x, x, mod, gain, w_up, w_up, cw, cb, w_down)


def natten_bias_table(rpb, n_rows):
    kr = min(NA_KR_MAX, n_rows)
    h, n_dr, n_dc = rpb.shape
    col = np.arange(GRID_W)
    col0 = np.clip(col - NA_KC // 2, 0, GRID_W - NA_KC)
    in_win = (col[None, :] >= col0[:, None]) & (col[None, :] < col0[:, None] + NA_KC)
    lo = GRID_W - NA_KC
    ext = jnp.pad(rpb.astype(F32), ((0, 0), (0, 0), (lo, 2 * GRID_W - lo - n_dc)))
    skew = jnp.tile(ext, (1, 1, GRID_W))[:, :, :GRID_W * (2 * GRID_W - 1)]
    skew = skew.reshape(h, n_dr, GRID_W, 2 * GRID_W - 1)[:, :, :, GRID_W - 1:]
    skew = jnp.where(in_win[None, None], skew, -jnp.inf)
    tab = jnp.stack([skew[:, NA_KR_MAX - 1 - s:NA_KR_MAX - 1 - s + kr] for s in range(kr)], axis=1)
    return tab.transpose(0, 1, 3, 2, 4).reshape(h, kr, GRID_W, kr * GRID_W)


NA_ROW_GROUP = 4


def _rms(x, gain):
    return x * lax.rsqrt(jnp.mean(x * x, axis=-1, keepdims=True) + EPS) * gain


def _dot_nt(a, b):
    return lax.dot_general(a, b, (((1,), (1,)), ((), ())), preferred_element_type=F32)


def _natten_kernel(*refs, n_rows, with_ctx_out):
    if with_ctx_out:
        (q_ref, k_ref, v_ref, kc_ref, vc_ref, gq_ref, gk_ref, bias_ref, qc_ref,
         o_ref, oc_ref, qs, ks, vs) = refs
    else:
        q_ref, k_ref, v_ref, kc_ref, vc_ref, gq_ref, gk_ref, bias_ref, o_ref, qs, ks, vs = refs
    dh = q_ref.shape[-1]
    kr = min(NA_KR_MAX, n_rows)
    scale = dh ** -0.5
    qs[...] = (_rms(q_ref[...], gq_ref[...]) * scale).astype(BF16)
    ks[...] = _rms(k_ref[...], gk_ref[...]).astype(BF16)
    vs[...] = v_ref[...].astype(BF16)
    kc = _rms(kc_ref[...], gk_ref[...]).astype(BF16)
    vc = vc_ref[...].astype(BF16)

    grp = NA_ROW_GROUP
    nk = kr * GRID_W

    def rows_group(gi, carry):
        r0 = gi * grp
        q0 = pl.multiple_of(r0 * GRID_W, grp * GRID_W)
        ws = [jnp.clip(r0 + t - kr // 2, 0, n_rows - kr) for t in range(grp)]
        k0 = [pl.multiple_of(w * GRID_W, GRID_W) for w in ws]
        s_loc = [_dot_nt(qs[pl.ds(q0 + t * GRID_W, GRID_W), :], ks[pl.ds(k0[t], nk), :]) for t in range(grp)]
        s_ctx = _dot_nt(qs[pl.ds(q0, grp * GRID_W), :], kc)
        p_loc, p_ctx, den = [], [], []
        for t in range(grp):
            sl = s_loc[t] + bias_ref[r0 + t - ws[t]]
            sc = s_ctx[t * GRID_W:(t + 1) * GRID_W]
            m = jnp.maximum(jnp.max(sl, axis=-1, keepdims=True), jnp.max(sc, axis=-1, keepdims=True))
            pl_t = jnp.exp(sl - m)
            pc_t = jnp.exp(sc - m)
            den.append(jnp.sum(pl_t, axis=-1, keepdims=True) + jnp.sum(pc_t, axis=-1, keepdims=True))
            p_loc.append(pl_t.astype(BF16))
            p_ctx.append(pc_t.astype(BF16))
        o_loc = [jnp.dot(p_loc[t], vs[pl.ds(k0[t], nk), :], preferred_element_type=F32) for t in range(grp)]
        o_ctx = jnp.dot(jnp.concatenate(p_ctx, axis=0), vc, preferred_element_type=F32)
        for t in range(grp):
            o = (o_loc[t] + o_ctx[t * GRID_W:(t + 1) * GRID_W]) / den[t]
            o_ref[pl.ds(q0 + t * GRID_W, GRID_W), :] = o.astype(o_ref.dtype)
        return carry

    lax.fori_loop(0, n_rows // grp, rows_group, 0)

    if with_ctx_out:
        qc = (_rms(qc_ref[...], gq_ref[...]) * scale).astype(BF16)
        s = _dot_nt(qc, kc)
        p = jnp.exp(s - jnp.max(s, axis=-1, keepdims=True))
        o = jnp.dot(p.astype(BF16), vc, preferred_element_type=F32) / jnp.sum(p, axis=-1, keepdims=True)
        oc_ref[...] = o.astype(oc_ref.dtype)


def natten(px, pc, col_q, gq, gk, bias, with_ctx_out):
    bsz, l, _ = px.shape
    lc = pc.shape[1]
    h, kr, _, nk = bias.shape
    dh = gq.shape[-1]
    cb = col_q // dh
    assert (l // GRID_W) % NA_ROW_GROUP == 0

    def head_spec(rows, which):
        return pl.BlockSpec((None, rows, dh), lambda b, hh: (b, 0, cb + which * h + hh))

    in_specs = [head_spec(l, 0), head_spec(l, 1), head_spec(l, 2), head_spec(lc, 1), head_spec(lc, 2),
                pl.BlockSpec((1, dh), lambda b, hh: (0, 0)), pl.BlockSpec((1, dh), lambda b, hh: (0, 0)),
                pl.BlockSpec((None, kr, GRID_W, nk), lambda b, hh: (hh, 0, 0, 0))]
    args = [px, px, px, pc, pc, gq, gk, bias]
    out_specs = [pl.BlockSpec((None, l, dh), lambda b, hh: (b, 0, hh))]
    out_shape = [jax.ShapeDtypeStruct((bsz, l, h * dh), BF16)]
    if with_ctx_out:
        in_specs.append(head_spec(lc, 0))
        args.append(pc)
        out_specs.append(pl.BlockSpec((None, lc, dh), lambda b, hh: (b, 0, hh)))
        out_shape.append(jax.ShapeDtypeStruct((bsz, lc, h * dh), BF16))
    outs = pl.pallas_call(
        functools.partial(_natten_kernel, n_rows=l // GRID_W, with_ctx_out=with_ctx_out),
        grid=(bsz, h),
        in_specs=in_specs, out_specs=out_specs, out_shape=out_shape,
        scratch_shapes=[pltpu.VMEM((l, dh), BF16)] * 3,
        compiler_params=_cparams(("arbitrary", "arbitrary")),
        name="natten",
    )(*args)
    return outs if with_ctx_out else (outs[0], None)


def _cos_sin(n, period):
    k = np.arange(n, dtype=np.int64)
    ang = (2.0 * np.pi / period) * ((k[:, None] * k[None, :]) % period)
    return np.cos(ang), np.sin(ang)


def _fourier_kernel(u_ref, w_ref, cd_ref, sd_ref, cs_ref, o_ref, wc_ref, ws_ref, v_ref, *, scale):
    l, c = u_ref.shape
    dg = cd_ref.shape[0]

    @pl.when(pl.program_id(0) == 0)
    def _():
        wc_ref[...] = jnp.zeros_like(wc_ref)
        ws_ref[...] = jnp.zeros_like(ws_ref)
        for g in range(c // dg):
            sl = slice(g * dg, (g + 1) * dg)
            wg = w_ref[g]
            wc_ref[sl, sl] = jnp.dot(cd_ref[...], wg, precision=lax.Precision.HIGHEST,
                                     preferred_element_type=F32).astype(BF16)
            ws_ref[sl, sl] = jnp.dot(sd_ref[...], wg, precision=lax.Precision.HIGHEST,
                                     preferred_element_type=F32).astype(BF16)

    u = u_ref[...].astype(BF16)
    v_ref[0:l, :] = jnp.dot(u, wc_ref[...], preferred_element_type=F32).astype(BF16)
    v_ref[l:2 * l, :] = jnp.dot(u, ws_ref[...], preferred_element_type=F32).astype(BF16)
    y = jnp.dot(cs_ref[...], v_ref[...], preferred_element_type=F32)
    o_ref[...] = (y * scale).astype(o_ref.dtype)


def fourier_mix(p, col, w):
    bsz, l, _ = p.shape
    g, dg, _ = w.shape
    c = g * dg
    cl, sl = _cos_sin(l, l)
    cd, sd = _cos_sin(dg, dg)
    cs = jnp.asarray(np.concatenate([cl, -sl], axis=1), BF16)
    return pl.pallas_call(
        functools.partial(_fourier_kernel, scale=float((l * dg) ** -0.5)),
        grid=(bsz,),
        in_specs=[pl.BlockSpec((None, l, c), lambda b: (b, 0, col // c)),
                  pl.BlockSpec((g, dg, dg), lambda b: (0, 0, 0)),
                  pl.BlockSpec((dg, dg), lambda b: (0, 0)),
                  pl.BlockSpec((dg, dg), lambda b: (0, 0)),
                  pl.BlockSpec((l, 2 * l), lambda b: (0, 0), pipeline_mode=pl.Buffered(1))],
        out_specs=pl.BlockSpec((None, l, c), lambda b: (b, 0, 0)),
        out_shape=jax.ShapeDtypeStruct((bsz, l, c), BF16),
        scratch_shapes=[pltpu.VMEM((c, c), BF16), pltpu.VMEM((c, c), BF16), pltpu.VMEM((2 * l, c), BF16)],
        compiler_params=_cparams(("arbitrary",)),
        name="fourier_mix",
    )(p, w, jnp.asarray(cd, F32), jnp.asarray(sd, F32), cs)


HY_CBLK = 256
HY_FBLK = 512


def _hyena_dft_matrix(l):
    k = np.arange(l, dtype=np.int64)
    ang = (np.pi / l) * ((k[:, None] * k[None, :]) % (2 * l))
    sn = np.sin(ang)
    sn[0, :] = 1.0 - 2.0 * (k % 2)
    return jnp.asarray(np.concatenate([np.cos(ang), sn], axis=0), BF16)


def _hyena_pos_features(l):
    t = np.linspace(0.0, 1.0, l)[:, None]
    w = (2.0 * np.pi / l) * np.arange(l)[:, None]
    f = np.linspace(1e-4, HY_BANDS - 1, HY_BANDS)[None, :]
    z = np.concatenate([t, np.cos(f * w), -np.sin(f * w)], axis=-1)
    return np.pad(z, ((0, 0), (0, LANES - z.shape[1])))


def _split_bf16(x):
    hi = x.astype(BF16)
    return hi, (x - hi.astype(F32)).astype(BF16)


def _hyena_filter_kernel(z_ref, w1_ref, b1_ref, w2_ref, b2_ref, fr_ref, w3f_ref, w3b_ref, dl_ref, m_ref,
                         p_ref, q_ref):
    l = z_ref.shape[0]
    hp = lax.Precision.HIGHEST
    z = z_ref[...]
    h = jnp.sin(fr_ref[0:1, :] * (jnp.dot(z, w1_ref[...], precision=hp, preferred_element_type=F32)
                                  + b1_ref[...]))
    h = jnp.sin(fr_ref[1:2, :] * (jnp.dot(h, w2_ref[...], precision=hp, preferred_element_type=F32)
                                  + b2_ref[...]))
    window = jnp.exp(-z[:, 0:1] * dl_ref[...]) + HY_DECAY_SHIFT
    hf = jnp.dot(h, w3f_ref[...], precision=hp, preferred_element_type=F32) * window
    hb = jnp.dot(h, w3b_ref[...], precision=hp, preferred_element_type=F32) * window
    norm = jnp.sum(jnp.abs(hf) + jnp.abs(hb), axis=0, keepdims=True) + EPS
    hf = hf / norm
    hb = hb / norm
    g1h, g1l = _split_bf16(hf + hb)
    g2h, g2l = _split_bf16(hb - hf)
    f1 = (jnp.dot(m_ref[...], g1h, preferred_element_type=F32)
          + jnp.dot(m_ref[...], g1l, preferred_element_type=F32))
    f2 = (jnp.dot(m_ref[l:2 * l, :], g2h, preferred_element_type=F32)
          + jnp.dot(m_ref[l:2 * l, :], g2l, preferred_element_type=F32))
    p_ref[...] = f1[0:l]
    row = lax.broadcasted_iota(jnp.int32, f2.shape, 0)
    q_ref[...] = jnp.where(row == 0, f1[l:l + 1], f2)


def hyena_filter_spectrum(l, w1, b1, w2, b2, w3, freq, m):
    c = w3.shape[1] // 2
    hid = w1.shape[1]
    z = jnp.asarray(_hyena_pos_features(l), F32)
    w1p = jnp.pad(w1, ((0, z.shape[1] - w1.shape[0]), (0, 0)))
    deltas = np.abs(np.linspace(math.log(HY_DECAY_TARGET) / HY_SLOW_PCT,
                                math.log(HY_DECAY_TARGET) / HY_FAST_PCT, c))[None, :]
    nb = c // HY_CBLK
    full = lambda shape: pl.BlockSpec(shape, lambda j: (0,) * len(shape))
    return pl.pallas_call(
        _hyena_filter_kernel,
        grid=(nb,),
        in_specs=[full(z.shape), full(w1p.shape), full((1, hid)), full(w2.shape), full((1, hid)),
                  full((2, hid)),
                  pl.BlockSpec((hid, HY_CBLK), lambda j: (0, j)),
                  pl.BlockSpec((hid, HY_CBLK), lambda j: (0, nb + j)),
                  pl.BlockSpec((1, HY_CBLK), lambda j: (0, j)),
                  pl.BlockSpec(m.shape, lambda j: (0, 0), pipeline_mode=pl.Buffered(1))],
        out_specs=[pl.BlockSpec((l, HY_CBLK), lambda j: (0, j))] * 2,
        out_shape=[jax.ShapeDtypeStruct((l, c), F32)] * 2,
        compiler_params=_cparams(("arbitrary",)),
        name="hyena_filter",
    )(z, w1p, b1.reshape(1, hid), w2, b2.reshape(1, hid), freq, w3, w3, jnp.asarray(deltas, F32), m)


def _shift_rows(u, down):
    l = u.shape[0]
    row = lax.broadcasted_iota(jnp.int32, u.shape, 0)
    if down:
        return jnp.where(row == 0, 0.0, pltpu.roll(u, 1, 0))
    return jnp.where(row == l - 1, 0.0, pltpu.roll(u, l - 1, 0))


def _dwconv3(u, taps):
    return (taps[3:4] + _shift_rows(u, True) * taps[0:1] + u * taps[1:2] + _shift_rows(u, False) * taps[2:3])


def _hyena_conv_kernel(x0_ref, x1_ref, v_ref, taps_ref, bias_ref, p_ref, q_ref, m_ref, o_ref):
    l, c = x0_ref.shape
    x1 = _dwconv3(x1_ref[...], taps_ref[1])
    s = _dwconv3(v_ref[...], taps_ref[2]) * x1
    sb = s.astype(BF16)
    fb = min(HY_FBLK, l)
    nblk = l // fb
    inv_l = 1.0 / l

    def forward(i):
        return (jnp.dot(m_ref[i * fb:(i + 1) * fb, :], sb, preferred_element_type=F32),
                jnp.dot(m_ref[l + i * fb:l + (i + 1) * fb, :], sb, preferred_element_type=F32))

    y_cos_acc = y_sin_acc = nyquist = None
    ab = forward(0)
    for i in range(nblk):
        a, b = ab
        if i + 1 < nblk:
            ab = forward(i + 1)
        p, q = p_ref[i * fb:(i + 1) * fb, :], q_ref[i * fb:(i + 1) * fb, :]
        bq = b * q
        if i == 0:
            first = lax.broadcasted_iota(jnp.int32, (fb, c), 0) == 0
            nyquist = bq[0:1] * (0.5 * inv_l)
            y_cos = (a * p + jnp.where(first, 0.0, bq)) * jnp.where(first, 0.5 * inv_l, inv_l)
            y_sin = jnp.where(first, 0.0, (b * p - a * q) * inv_l)
        else:
            y_cos = (a * p + bq) * inv_l
            y_sin = (b * p - a * q) * inv_l
        d_cos = jnp.dot(m_ref[0:l, i * fb:(i + 1) * fb], y_cos.astype(BF16), preferred_element_type=F32)
        d_sin = jnp.dot(m_ref[l:2 * l, i * fb:(i + 1) * fb], y_sin.astype(BF16), preferred_element_type=F32)
        y_cos_acc = d_cos if y_cos_acc is None else y_cos_acc + d_cos
        y_sin_acc = d_sin if y_sin_acc is None else y_sin_acc + d_sin
    row = lax.broadcasted_iota(jnp.int32, (l, c), 0)
    y = (y_cos_acc + jnp.where(row == 0, 0.0, y_sin_acc)
         + jnp.where(row % 2 == 0, 1.0, -1.0) * nyquist)
    x0 = _dwconv3(x0_ref[...], taps_ref[0])
    o_ref[...] = ((y + s * bias_ref[...]) * x0).astype(o_ref.dtype)


def hyena_conv(p, col, conv_w, conv_b, bias, spec_p, spec_q, m):
    bsz, l, _ = p.shape
    c = bias.shape[-1]
    nb = c // HY_CBLK
    cb0 = col // HY_CBLK
    taps = jnp.concatenate([conv_w, conv_b[None]], axis=0)
    taps = taps.reshape(4, 3, nb, HY_CBLK).transpose(2, 1, 0, 3)

    def part(k):
        return pl.BlockSpec((None, l, HY_CBLK), lambda j, b: (b, 0, cb0 + k * nb + j))

    return pl.pallas_call(
        _hyena_conv_kernel,
        grid=(nb, bsz),
        in_specs=[part(0), part(1), part(2),
                  pl.BlockSpec((None, 3, 4, HY_CBLK), lambda j, b: (j, 0, 0, 0)),
                  pl.BlockSpec((1, HY_CBLK), lambda j, b: (0, j)),
                  pl.BlockSpec((l, HY_CBLK), lambda j, b: (0, j)),
                  pl.BlockSpec((l, HY_CBLK), lambda j, b: (0, j)),
                  pl.BlockSpec(m.shape, lambda j, b: (0, 0), pipeline_mode=pl.Buffered(1))],
        out_specs=pl.BlockSpec((None, l, HY_CBLK), lambda j, b: (b, 0, j)),
        out_shape=jax.ShapeDtypeStruct((bsz, l, c), BF16),
        compiler_params=_cparams(("arbitrary", "arbitrary")),
        name="hyena_conv",
    )(p, p, p, taps, bias.reshape(1, c), spec_p, spec_q, m)


GLA_HP = 2
GLA_ROPE_PAIR = 16
GLA_SCAN_UNROLL = 2


def _gla_rope_tables(l, dk):
    half = dk // 2
    nf = half // 2
    assert nf == GLA_ROPE_PAIR
    inv = ROPE_THETA ** (-np.arange(nf, dtype=np.float64) / nf)
    t = np.arange(l)
    ang_r = (t // GRID_W)[:, None] * inv
    ang_c = (t % GRID_W)[:, None] * inv
    cos = np.concatenate([np.cos(ang_r)] * 2 + [np.cos(ang_c)] * 2, axis=1)
    sin = np.concatenate([-np.sin(ang_r), np.sin(ang_r), -np.sin(ang_c), np.sin(ang_c)], axis=1)
    return (jnp.asarray(np.tile(cos, (1, GLA_HP)), F32), jnp.asarray(np.tile(sin, (1, GLA_HP)), F32))


def _rope(x, cos, sin):
    lane = lax.broadcasted_iota(jnp.int32, x.shape, 1)
    lanes = x.shape[1]
    partner = jnp.where(lane % (2 * GLA_ROPE_PAIR) < GLA_ROPE_PAIR,
                        pltpu.roll(x, lanes - GLA_ROPE_PAIR, 1), pltpu.roll(x, GLA_ROPE_PAIR, 1))
    return x * cos + partner * sin


def _log_sigmoid(x):
    return jnp.minimum(x, 0.0) - jnp.log(1.0 + jnp.exp(-jnp.abs(x)))


def _gla_kernel(*refs, with_ctx_out):
    (q_ref, k_ref, v_ref, r_ref, z_ref, cq_ref, ck_ref, cv_ref, cr_ref, cz_ref,
     wz_ref, bz_ref, g_ref, cos_ref, sin_ref) = refs[:15]
    if with_ctx_out:
        o_ref, oc_ref = refs[15:17]
        scratch = refs[17:]
    else:
        o_ref, oc_ref = refs[15], None
        scratch = refs[16:]
    qs, ks, las, ofs, obs, cqs, cks, clas, cofs, cobs, st_f, st_b = scratch
    dk2 = q_ref.shape[1]
    dv2 = v_ref.shape[1]
    dk, dv = dk2 // GLA_HP, dv2 // GLA_HP
    ch = GLA_CHUNK
    hp = lax.Precision.HIGHEST

    def gates(z):
        pre = jnp.dot(z, wz_ref[...], precision=hp, preferred_element_type=F32) + bz_ref[...]
        return _log_sigmoid(pre) * (1.0 / GLA_TAU)

    qs[...] = _rope(q_ref[...] * dk ** -0.5, cos_ref[...], sin_ref[...])
    ks[...] = _rope(k_ref[...], cos_ref[...], sin_ref[...])
    las[...] = gates(z_ref[...])
    cqs[...] = cq_ref[...] * dk ** -0.5
    cks[...] = ck_ref[...]
    clas[...] = gates(cz_ref[...])
    st_f[...] = jnp.zeros_like(st_f)
    st_b[...] = jnp.zeros_like(st_b)

    ri = lax.broadcasted_iota(jnp.int32, (ch, ch), 0)
    ci = lax.broadcasted_iota(jnp.int32, (ch, ch), 1)
    tri = {False: ri >= ci, True: ri <= ci}
    tri_b16 = {d: jnp.where(m, 1.0, 0.0).astype(BF16) for d, m in tri.items()}
    tri2 = {d: jnp.concatenate([m] * GLA_HP, axis=0) for d, m in tri.items()}
    lane_head = lax.broadcasted_iota(jnp.int32, (ch, dk2), 1) // dk
    st_r = lax.broadcasted_iota(jnp.int32, (dv2, dk2), 0) // dv
    st_c = lax.broadcasted_iota(jnp.int32, (dv2, dk2), 1) // dk
    st_diag = st_r == st_c

    def scan(q_s, k_s, la_s, v_in, of_s, ob_s, n):
        unroll = GLA_SCAN_UNROLL

        def body(trip, carry):
            items = []
            for u in range(unroll):
                c = trip * unroll + u
                items.append((False, pl.ds(pl.multiple_of(c * ch, ch), ch)))
                items.append((True, pl.ds(pl.multiple_of((n - 1 - c) * ch, ch), ch)))
            v_c = [v_in[rows, :].astype(BF16) for _, rows in items]
            cum = []
            for bw, rows in items:
                la_hi, la_lo = _split_bf16(la_s[rows, dk2:2 * dk2] if bw else la_s[rows, 0:dk2])
                cum.append(jnp.dot(tri_b16[bw], la_hi, preferred_element_type=F32)
                           + jnp.dot(tri_b16[bw], la_lo, preferred_element_type=F32))
            q_dec, k_end, decay, sc = [], [], [], []
            for i, (bw, rows) in enumerate(items):
                q_c, k_c = q_s[rows, :], k_s[rows, :]
                tot = cum[i][0:1] if bw else cum[i][ch - 1:ch]
                qd = q_c * jnp.exp(cum[i])
                k_inv = (k_c * jnp.exp(-cum[i])).astype(BF16)
                k_end.append((k_c * jnp.exp(tot - cum[i])).astype(BF16))
                decay.append(jnp.exp(tot))
                q_heads = jnp.concatenate([jnp.where(lane_head == h, qd, 0.0) for h in range(GLA_HP)], axis=0)
                sc.append(_dot_nt(q_heads.astype(BF16), k_inv))
                q_dec.append(qd.astype(BF16))
            o_intra, ds_t = [], []
            for i, (bw, rows) in enumerate(items):
                pv = jnp.dot(jnp.where(tri2[bw], sc[i], 0.0).astype(BF16), v_c[i],
                             preferred_element_type=F32)
                o_intra.append(jnp.concatenate(
                    [pv[h * ch:(h + 1) * ch, h * dv:(h + 1) * dv] for h in range(GLA_HP)], axis=1))
                ds_t.append(lax.dot_general(v_c[i], k_end[i], (((0,), (0,)), ((), ())),
                                            preferred_element_type=F32))
            for i, (bw, rows) in enumerate(items):
                st, o_s = (st_b, ob_s) if bw else (st_f, of_s)
                s_t = st[...]
                o_s[rows, :] = o_intra[i] + _dot_nt(q_dec[i], s_t.astype(BF16))
                st[...] = s_t * decay[i] + jnp.where(st_diag, ds_t[i], 0.0)
            return carry

        lax.fori_loop(0, n // unroll, body, 0)

    def finish(of_s, ob_s, gate_ref, out_ref, n_blocks, blk):
        def body(i, carry):
            rows = pl.ds(pl.multiple_of(i * blk, blk), blk)
            o = of_s[rows, :] + ob_s[rows, :]
            gate = gate_ref[rows, :]
            parts = []
            for h in range(GLA_HP):
                oh = o[:, h * dv:(h + 1) * dv]
                gh = gate[:, h * dv:(h + 1) * dv]
                parts.append(_rms(oh, g_ref[...]) * (gh * jax.nn.sigmoid(gh)))
            out_ref[rows, :] = jnp.concatenate(parts, axis=1).astype(out_ref.dtype)
            return carry
        lax.fori_loop(0, n_blocks, body, 0)

    l, lc = q_ref.shape[0], cq_ref.shape[0]
    scan(cqs, cks, clas, cv_ref, cofs, cobs, lc // ch)
    if with_ctx_out:
        finish(cofs, cobs, cr_ref, oc_ref, 1, lc)
    scan(qs, ks, las, v_ref, ofs, obs, l // ch)
    finish(ofs, obs, r_ref, o_ref, l // lc, lc)


def gla(px, pxz, pc, pcz, w_gate, b_gate, out_gain, with_ctx_out):
    bsz, l, _ = px.shape
    lc = pc.shape[1]
    dv = out_gain.shape[-1]
    hdk = w_gate.shape[-1]
    dk = hdk // GLA_HEADS
    dk2, dv2 = GLA_HP * dk, GLA_HP * dv
    nhp = GLA_HEADS // GLA_HP
    zw = pxz.shape[-1]
    rank = w_gate.shape[1]
    wz = jnp.zeros((nhp, zw, 2 * dk2), F32)
    for u in range(2):
        blk = w_gate[u].reshape(rank, nhp, dk2).transpose(1, 0, 2)
        wz = wz.at[:, u * rank:(u + 1) * rank, u * dk2:(u + 1) * dk2].set(blk)
    bz = b_gate.reshape(2, nhp, dk2).transpose(1, 0, 2).reshape(nhp, 1, 2 * dk2)
    cos, sin = _gla_rope_tables(l, dk)
    k_cb, v_cb, r_cb = hdk // dk2, 2 * hdk // dv2, (2 * hdk + GLA_HEADS * dv) // dv2

    def col(rows, width, cb):
        return pl.BlockSpec((None, rows, width), lambda b, j: (b, 0, cb + j))

    def whole(rows, width):
        return pl.BlockSpec((None, rows, width), lambda b, j: (b, 0, 0))

    const = lambda shape: pl.BlockSpec(shape, lambda b, j: (0,) * len(shape))
    in_specs = [col(l, dk2, 0), col(l, dk2, k_cb), col(l, dv2, v_cb), col(l, dv2, r_cb), whole(l, zw),
                col(lc, dk2, 0), col(lc, dk2, k_cb), col(lc, dv2, v_cb), col(lc, dv2, r_cb), whole(lc, zw),
                pl.BlockSpec((None, zw, 2 * dk2), lambda b, j: (j, 0, 0)),
                pl.BlockSpec((None, 1, 2 * dk2), lambda b, j: (j, 0, 0)),
                const((1, dv)), const((l, dk2)), const((l, dk2))]
    out_specs = [pl.BlockSpec((None, l, dv2), lambda b, j: (b, 0, j))]
    out_shape = [jax.ShapeDtypeStruct((bsz, l, GLA_HEADS * dv), BF16)]
    if with_ctx_out:
        out_specs.append(pl.BlockSpec((None, lc, dv2), lambda b, j: (b, 0, j)))
        out_shape.append(jax.ShapeDtypeStruct((bsz, lc, GLA_HEADS * dv), BF16))
    scratch = [pltpu.VMEM((l, dk2), F32), pltpu.VMEM((l, dk2), F32), pltpu.VMEM((l, 2 * dk2), F32),
               pltpu.VMEM((l, dv2), F32), pltpu.VMEM((l, dv2), F32),
               pltpu.VMEM((lc, dk2), F32), pltpu.VMEM((lc, dk2), F32), pltpu.VMEM((lc, 2 * dk2), F32),
               pltpu.VMEM((lc, dv2), F32), pltpu.VMEM((lc, dv2), F32),
               pltpu.VMEM((dv2, dk2), F32), pltpu.VMEM((dv2, dk2), F32)]
    outs = pl.pallas_call(
        functools.partial(_gla_kernel, with_ctx_out=with_ctx_out),
        grid=(bsz, nhp),
        in_specs=in_specs, out_specs=out_specs, out_shape=out_shape, scratch_shapes=scratch,
        compiler_params=_cparams(("arbitrary", "arbitrary")),
        name="gla",
    )(px, px, px, px, pxz, pc, pc, pc, pc, pcz, wz, bz, out_gain.reshape(1, dv), cos, sin)
    return outs if with_ctx_out else (outs[0], None)


def kernel(x, c, ctx, c_ctx, w_mod, b_mod, g_mix, w_in, gla_gate_w, gla_gate_b, gla_out_g,
           hy_conv_w, hy_conv_b, hy_w1, hy_b1, hy_w2, hy_b2, hy_w3, hy_freq, hy_bias, fn_w,
           na_q_g, na_k_g, na_rpb, w_out, g_ffn, ffn_w_up, ffn_conv_w, ffn_conv_b, ffn_w_down):
    bsz, l_lat, d = x.shape
    l_ctx = ctx.shape[1]
    depth = w_mod.shape[0]
    w = d // 4

    n_cond = -(-(bsz + 1) // SUBLANES) * SUBLANES
    cond = jnp.zeros((n_cond, d), F32).at[:bsz].set(c).at[bsz].set(c_ctx)
    mods = mod_vectors(cond, w_mod, b_mod).reshape(depth, n_cond, N_MOD, d)
    mods = jnp.pad(mods, ((0, 0), (0, 0), (0, MOD_ROWS - N_MOD), (0, 0)))

    z0 = 3 * w
    zw = 2 * GLA_GATE_RANK
    w_main = jnp.concatenate([w_in[:, :, :z0], w_in[:, :, z0 + zw:]], axis=-1).astype(BF16)
    w_z = jnp.pad(w_in[:, :, z0:z0 + zw], ((0, 0), (0, 0), (0, LANES - zw))).astype(BF16)
    w_out_b = w_out.astype(BF16)
    w_up_b = ffn_w_up.astype(BF16)
    w_down_b = ffn_w_down.astype(BF16)

    m_lat = _hyena_dft_matrix(l_lat)
    m_ctx = _hyena_dft_matrix(l_ctx)
    ctx_flat = ctx.reshape(1, bsz * l_ctx, d)
    for layer in range(depth):
        last = layer == depth - 1
        mod_x = mods[layer, :bsz]
        mod_c = mods[layer, bsz:bsz + 1]
        g_mix_l = g_mix[layer].reshape(1, d)
        g_ffn_l = g_ffn[layer].reshape(1, d)

        px, pxz = in_proj(x, mod_x, g_mix_l, w_main[layer], w_z[layer], tm=1024)
        pc, pcz = in_proj(ctx_flat, mod_c, g_mix_l, w_main[layer], w_z[layer], tm=1024)
        pc = pc.reshape(bsz, l_ctx, -1)
        pcz = pcz.reshape(bsz, l_ctx, -1)

        y_a, yc_a = gla(px, pxz, pc, pcz, gla_gate_w[layer], gla_gate_b[layer], gla_out_g[layer],
                        with_ctx_out=not last)
        hy_filt = (hy_w1[layer], hy_b1[layer], hy_w2[layer], hy_b2[layer], hy_w3[layer], hy_freq[layer])
        hy_p, hy_q = hyena_filter_spectrum(l_lat, *hy_filt, m_lat)
        y_b = hyena_conv(px, 3 * w, hy_conv_w[layer], hy_conv_b[layer], hy_bias[layer], hy_p, hy_q, m_lat)
        y_c = fourier_mix(px, 6 * w, fn_w[layer])
        na_bias = natten_bias_table(na_rpb[layer], l_lat // GRID_W)
        y_d, yc_d = natten(px, pc, 7 * w, na_q_g[layer].reshape(1, -1), na_k_g[layer].reshape(1, -1),
                           na_bias, with_ctx_out=not last)
        x = out_proj([y_a, y_b, y_c, y_d], w_out_b[layer], x, mod_x, tm=512)
        cw = ffn_conv_w[layer]
        cb = ffn_conv_b[layer].reshape(1, -1)
        x = conv_ffn(x, mod_x, g_ffn_l, w_up_b[layer], cw, cb, w_down_b[layer], tm=512)

        if not last:
            hc_p, hc_q = hyena_filter_spectrum(l_ctx, *hy_filt, m_ctx)
            yc_b = hyena_conv(pc, 3 * w, hy_conv_w[layer], hy_conv_b[layer], hy_bias[layer], hc_p, hc_q, m_ctx)
            yc_c = fourier_mix(pc, 6 * w, fn_w[layer])
            ycs = [y.reshape(1, bsz * l_ctx, w) for y in (yc_a, yc_b, yc_c, yc_d)]
            ctx_flat = out_proj(ycs, w_out_b[layer], ctx_flat, mod_c, tm=512)
            ctx_flat = conv_ffn(ctx_flat, mod_c, g_ffn_l, w_up_b[layer], cw, cb, w_down_b[layer],
                                tm=512, seg_len=l_ctx)
    return x
```

```python
import functools
import math

import numpy as np
import jax
import jax.numpy as jnp
from jax import lax
from jax.experimental import pallas as pl
from jax.experimental.pallas import tpu as pltpu

F32 = jnp.float32
BF16 = jnp.bfloat16

GRID_W = 64
GLA_HEADS = 4
GLA_GATE_RANK = 16
GLA_TAU = 16.0
GLA_CHUNK = 64
HY_BANDS = 16
HY_DECAY_TARGET = 1e-2
HY_FAST_PCT = 0.3
HY_SLOW_PCT = 1.5
HY_DECAY_SHIFT = 0.05
FN_GROUPS = 4
NA_HEADS = 4
NA_KR_MAX = 8
NA_KC = 16
ROPE_THETA = 10000.0
N_MOD = 6
EPS = 1e-6

V7X_VMEM_LIMIT = 56 * 1024 * 1024
SUBLANES = 8
LANES = 128
MOD_ROWS = 8
FFN_HALO = 16


def _cparams(sem):
    return pltpu.CompilerParams(dimension_semantics=sem, vmem_limit_bytes=V7X_VMEM_LIMIT)


def _mod_kernel(s_ref, w_ref, b_ref, o_ref):
    s = s_ref[...]
    s = s * jax.nn.sigmoid(s)
    o_ref[...] = jnp.dot(s.astype(BF16), w_ref[...].astype(BF16),
                         preferred_element_type=F32) + b_ref[...]


def mod_vectors(cond, w_mod, b_mod, tn=1024):
    depth, d, n = w_mod.shape
    r = cond.shape[0]
    return pl.pallas_call(
        _mod_kernel,
        grid=(depth, n // tn),
        in_specs=[pl.BlockSpec((r, d), lambda l, j: (0, 0)),
                  pl.BlockSpec((None, d, tn), lambda l, j: (l, 0, j)),
                  pl.BlockSpec((None, 1, tn), lambda l, j: (l, 0, j))],
        out_specs=pl.BlockSpec((None, r, tn), lambda l, j: (l, 0, j)),
        out_shape=jax.ShapeDtypeStruct((depth, r, n), F32),
        compiler_params=_cparams(("arbitrary", "arbitrary")),
        name="mod_vectors",
    )(cond, w_mod, b_mod.reshape(depth, 1, n))


def _norm_mod(x, gain, shift, scale):
    ms = jnp.mean(x * x, axis=-1, keepdims=True)
    return (x * lax.rsqrt(ms + EPS) * gain) * (1.0 + scale) + shift


def _inproj_kernel(x_ref, mod_ref, g_ref, w_ref, wz_ref, o_ref, oz_ref, h_ref):
    @pl.when(pl.program_id(2) == 0)
    def _():
        h = _norm_mod(x_ref[...], g_ref[...], mod_ref[0:1, :], mod_ref[1:2, :]).astype(BF16)
        h_ref[...] = h
        oz_ref[...] = jnp.dot(h, wz_ref[...], preferred_element_type=F32)

    o_ref[...] = jnp.dot(h_ref[...], w_ref[...], preferred_element_type=F32)


def in_proj(x, mod, gain, w_main, w_z, tm, tn=1024):
    g, l, d = x.shape
    n = w_main.shape[1]
    nz = w_z.shape[1]
    return pl.pallas_call(
        _inproj_kernel,
        grid=(g, l // tm, n // tn),
        in_specs=[pl.BlockSpec((None, tm, d), lambda b, i, j: (b, i, 0)),
                  pl.BlockSpec((None, MOD_ROWS, d), lambda b, i, j: (b, 0, 0)),
                  pl.BlockSpec((1, d), lambda b, i, j: (0, 0)),
                  pl.BlockSpec((d, tn), lambda b, i, j: (0, j)),
                  pl.BlockSpec((d, nz), lambda b, i, j: (0, 0))],
        out_specs=[pl.BlockSpec((None, tm, tn), lambda b, i, j: (b, i, j)),
                   pl.BlockSpec((None, tm, nz), lambda b, i, j: (b, i, 0))],
        out_shape=[jax.ShapeDtypeStruct((g, l, n), F32),
                   jax.ShapeDtypeStruct((g, l, nz), F32)],
        scratch_shapes=[pltpu.VMEM((tm, d), BF16)],
        compiler_params=_cparams(("arbitrary", "arbitrary", "arbitrary")),
        name="in_proj",
    )(x, mod, gain, w_main, w_z)


def _outproj_kernel(ya_ref, yb_ref, yc_ref, yd_ref, w_ref, x_ref, mod_ref, o_ref):
    kw = ya_ref.shape[-1]
    acc = jnp.dot(ya_ref[...], w_ref[0 * kw:1 * kw, :], preferred_element_type=F32)
    acc += jnp.dot(yb_ref[...], w_ref[1 * kw:2 * kw, :], preferred_element_type=F32)
    acc += jnp.dot(yc_ref[...], w_ref[2 * kw:3 * kw, :], preferred_element_type=F32)
    acc += jnp.dot(yd_ref[...], w_ref[3 * kw:4 * kw, :], preferred_element_type=F32)
    o_ref[...] = x_ref[...] + mod_ref[2:3, :] * acc


def out_proj(ys, w_out, x, mod, tm):
    g, l, d = x.shape
    kw = ys[0].shape[-1]
    yspec = pl.BlockSpec((None, tm, kw), lambda b, i: (b, i, 0))
    return pl.pallas_call(
        _outproj_kernel,
        grid=(g, l // tm),
        in_specs=[yspec, yspec, yspec, yspec,
                  pl.BlockSpec((4 * kw, d), lambda b, i: (0, 0)),
                  pl.BlockSpec((None, tm, d), lambda b, i: (b, i, 0)),
                  pl.BlockSpec((None, MOD_ROWS, d), lambda b, i: (b, 0, 0))],
        out_specs=pl.BlockSpec((None, tm, d), lambda b, i: (b, i, 0)),
        out_shape=jax.ShapeDtypeStruct((g, l, d), F32),
        compiler_params=_cparams(("arbitrary", "arbitrary")),
        name="out_proj",
    )(*ys, w_out, x, mod)


def _ffn_kernel(x_ref, xp_ref, xn_ref, mod_ref, g_ref, wa_ref, wg_ref, cw_ref, cb_ref, wd_ref,
                o_ref, h_ref, *, seg_len):
    i = pl.program_id(1)
    j = pl.program_id(2)
    tm = x_ref.shape[0]
    tf = wa_ref.shape[1]
    hs = FFN_HALO

    @pl.when(j == 0)
    def _():
        gain, shift, scale = g_ref[...], mod_ref[3:4, :], mod_ref[4:5, :]
        h_ref[hs:hs + tm, :] = _norm_mod(x_ref[...], gain, shift, scale).astype(BF16)
        hp = jnp.where(i > 0, _norm_mod(xp_ref[...], gain, shift, scale), 0.0)
        hn = jnp.where(i < pl.num_programs(1) - 1, _norm_mod(xn_ref[...], gain, shift, scale), 0.0)
        zero = jnp.zeros_like(hp)
        h_ref[0:hs, :] = jnp.concatenate([zero, hp], axis=0).astype(BF16)
        h_ref[hs + tm:2 * hs + tm, :] = jnp.concatenate([hn, zero], axis=0).astype(BF16)
        o_ref[...] = x_ref[...]

    h = h_ref[...]

    def conv(u, c0, width):
        w = cw_ref[:, c0:c0 + width]
        prev, nxt = u[hs - 1:hs - 1 + tm], u[hs + 1:hs + 1 + tm]
        if seg_len is not None:
            pos = lax.broadcasted_iota(jnp.int32, prev.shape, 0) % seg_len
            prev = jnp.where(pos == 0, 0.0, prev)
            nxt = jnp.where(pos == seg_len - 1, 0.0, nxt)
        return cb_ref[:, c0:c0 + width] + prev * w[0:1] + u[hs:hs + tm] * w[1:2] + nxt * w[2:3]

    a = conv(jnp.dot(h, wa_ref[...], preferred_element_type=F32), 0, tf)
    gt = conv(jnp.dot(h, wg_ref[...], preferred_element_type=F32), tf, tf)
    act = (a * (gt * jax.nn.sigmoid(gt))).astype(BF16)
    o_ref[...] += mod_ref[5:6, :] * jnp.dot(act, wd_ref[...], preferred_element_type=F32)


def conv_ffn(x, mod, gain, w_up, conv_w, conv_b, w_down, tm, tf=512, seg_len=None, single_buffer_x=False):
    g, l, d = x.shape
    assert seg_len is None or (tm % seg_len == 0 and l % tm == 0)
    f = w_down.shape[0]
    nf = f // tf
    nb = tm // SUBLANES
    last = l // SUBLANES - 1
    cw = conv_w.reshape(3, 2, nf, tf).transpose(2, 0, 1, 3).reshape(nf, 3, 2 * tf)
    cb = conv_b.reshape(1, 2, nf, tf).transpose(2, 0, 1, 3).reshape(nf, 1, 2 * tf)
    return pl.pallas_call(
        functools.partial(_ffn_kernel, seg_len=seg_len),
        grid=(g, l // tm, nf),
        in_specs=[pl.BlockSpec((None, tm, d), lambda b, i, j: (b, i, 0),
                               pipeline_mode=pl.Buffered(1) if single_buffer_x else None),
                  pl.BlockSpec((None, SUBLANES, d), lambda b, i, j: (b, jnp.maximum(i * nb - 1, 0), 0)),
                  pl.BlockSpec((None, SUBLANES, d), lambda b, i, j: (b, jnp.minimum((i + 1) * nb, last), 0)),
                  pl.BlockSpec((None, MOD_ROWS, d), lambda b, i, j: (b, 0, 0)),
                  pl.BlockSpec((1, d), lambda b, i, j: (0, 0)),
                  pl.BlockSpec((d, tf), lambda b, i, j: (0, j)),
                  pl.BlockSpec((d, tf), lambda b, i, j: (0, nf + j)),
                  pl.BlockSpec((None, 3, 2 * tf), lambda b, i, j: (j, 0, 0)),
                  pl.BlockSpec((None, 1, 2 * tf), lambda b, i, j: (j, 0, 0)),
                  pl.BlockSpec((tf, d), lambda b, i, j: (j, 0))],
        out_specs=pl.BlockSpec((None, tm, d), lambda b, i, j: (b, i, 0)),
        out_shape=jax.ShapeDtypeStruct((g, l, d), F32),
        scratch_shapes=[pltpu.VMEM((tm + 2 * FFN_HALO, d), BF16)],
        compiler_params=_cparams(("arbitrary", "arbitrary", "arbitrary")),
        name="conv_ffn",
    )(x, x, x, mod, gain, w_up, w_up, cw, cb, w_down)


def natten_bias_table(rpb, n_rows):
    kr = min(NA_KR_MAX, n_rows)
    h, n_dr, n_dc = rpb.shape
    col = np.arange(GRID_W)
    col0 = np.clip(col - NA_KC // 2, 0, GRID_W - NA_KC)
    in_win = (col[None, :] >= col0[:, None]) & (col[None, :] < col0[:, None] + NA_KC)
    lo = GRID_W - NA_KC
    ext = jnp.pad(rpb.astype(F32), ((0, 0), (0, 0), (lo, 2 * GRID_W - lo - n_dc)))
    skew = jnp.tile(ext, (1, 1, GRID_W))[:, :, :GRID_W * (2 * GRID_W - 1)]
    skew = skew.reshape(h, n_dr, GRID_W, 2 * GRID_W - 1)[:, :, :, GRID_W - 1:]
    skew = jnp.where(in_win[None, None], skew, -jnp.inf)
    tab = jnp.stack([skew[:, NA_KR_MAX - 1 - s:NA_KR_MAX - 1 - s + kr] for s in range(kr)], axis=1)
    return tab.transpose(0, 1, 3, 2, 4).reshape(h, kr, GRID_W, kr * GRID_W)


NA_ROW_GROUP = 8


def _rms(x, gain):
    return x * lax.rsqrt(jnp.mean(x * x, axis=-1, keepdims=True) + EPS) * gain


def _dot_nt(a, b):
    return lax.dot_general(a, b, (((1,), (1,)), ((), ())), preferred_element_type=F32)


def _natten_kernel(*refs, n_rows, with_ctx_out):
    if with_ctx_out:
        (q_ref, k_ref, v_ref, kc_ref, vc_ref, gq_ref, gk_ref, bias_ref, qc_ref,
         o_ref, oc_ref, qs, ks, vs) = refs
    else:
        q_ref, k_ref, v_ref, kc_ref, vc_ref, gq_ref, gk_ref, bias_ref, o_ref, qs, ks, vs = refs
    dh = q_ref.shape[-1]
    kr = min(NA_KR_MAX, n_rows)
    scale = dh ** -0.5
    qs[...] = (_rms(q_ref[...], gq_ref[...]) * scale).astype(BF16)
    ks[...] = _rms(k_ref[...], gk_ref[...]).astype(BF16)
    vs[...] = v_ref[...].astype(BF16)
    kc = _rms(kc_ref[...], gk_ref[...]).astype(BF16)
    vc = vc_ref[...].astype(BF16)

    grp = NA_ROW_GROUP
    nk = kr * GRID_W

    def rows_group(gi, carry):
        r0 = gi * grp
        q0 = pl.multiple_of(r0 * GRID_W, grp * GRID_W)
        ws = [jnp.clip(r0 + t - kr // 2, 0, n_rows - kr) for t in range(grp)]
        k0 = [pl.multiple_of(w * GRID_W, GRID_W) for w in ws]
        s_loc = [_dot_nt(qs[pl.ds(q0 + t * GRID_W, GRID_W), :], ks[pl.ds(k0[t], nk), :]) for t in range(grp)]
        s_ctx = _dot_nt(qs[pl.ds(q0, grp * GRID_W), :], kc)
        p_loc, p_ctx, den = [], [], []
        for t in range(grp):
            sl = s_loc[t] + bias_ref[r0 + t - ws[t]]
            sc = s_ctx[t * GRID_W:(t + 1) * GRID_W]
            m = jnp.maximum(jnp.max(sl, axis=-1, keepdims=True), jnp.max(sc, axis=-1, keepdims=True))
            pl_t = jnp.exp(sl - m)
            pc_t = jnp.exp(sc - m)
            den.append(jnp.sum(pl_t, axis=-1, keepdims=True) + jnp.sum(pc_t, axis=-1, keepdims=True))
            p_loc.append(pl_t.astype(BF16))
            p_ctx.append(pc_t.astype(BF16))
        o_loc = [jnp.dot(p_loc[t], vs[pl.ds(k0[t], nk), :], preferred_element_type=F32) for t in range(grp)]
        o_ctx = jnp.dot(jnp.concatenate(p_ctx, axis=0), vc, preferred_element_type=F32)
        for t in range(grp):
            o = (o_loc[t] + o_ctx[t * GRID_W:(t + 1) * GRID_W]) / den[t]
            o_ref[pl.ds(q0 + t * GRID_W, GRID_W), :] = o.astype(o_ref.dtype)
        return carry

    lax.fori_loop(0, n_rows // grp, rows_group, 0)

    if with_ctx_out:
        qc = (_rms(qc_ref[...], gq_ref[...]) * scale).astype(BF16)
        s = _dot_nt(qc, kc)
        p = jnp.exp(s - jnp.max(s, axis=-1, keepdims=True))
        o = jnp.dot(p.astype(BF16), vc, preferred_element_type=F32) / jnp.sum(p, axis=-1, keepdims=True)
        oc_ref[...] = o.astype(oc_ref.dtype)


def natten(px, pc, col_q, gq, gk, bias, with_ctx_out):
    bsz, l, _ = px.shape
    lc = pc.shape[1]
    h, kr, _, nk = bias.shape
    dh = gq.shape[-1]
    cb = col_q // dh
    assert (l // GRID_W) % NA_ROW_GROUP == 0

    def head_spec(rows, which):
        return pl.BlockSpec((None, rows, dh), lambda b, hh: (b, 0, cb + which * h + hh))

    in_specs = [head_spec(l, 0), head_spec(l, 1), head_spec(l, 2), head_spec(lc, 1), head_spec(lc, 2),
                pl.BlockSpec((1, dh), lambda b, hh: (0, 0)), pl.BlockSpec((1, dh), lambda b, hh: (0, 0)),
                pl.BlockSpec((None, kr, GRID_W, nk), lambda b, hh: (hh, 0, 0, 0))]
    args = [px, px, px, pc, pc, gq, gk, bias]
    out_specs = [pl.BlockSpec((None, l, dh), lambda b, hh: (b, 0, hh))]
    out_shape = [jax.ShapeDtypeStruct((bsz, l, h * dh), BF16)]
    if with_ctx_out:
        in_specs.append(head_spec(lc, 0))
        args.append(pc)
        out_specs.append(pl.BlockSpec((None, lc, dh), lambda b, hh: (b, 0, hh)))
        out_shape.append(jax.ShapeDtypeStruct((bsz, lc, h * dh), BF16))
    outs = pl.pallas_call(
        functools.partial(_natten_kernel, n_rows=l // GRID_W, with_ctx_out=with_ctx_out),
        grid=(bsz, h),
        in_specs=in_specs, out_specs=out_specs, out_shape=out_shape,
        scratch_shapes=[pltpu.VMEM((l, dh), BF16)] * 3,
        compiler_params=_cparams(("arbitrary", "arbitrary")),
        name="natten",
    )(*args)
    return outs if with_ctx_out else (outs[0], None)


def _cos_sin(n, period):
    k = np.arange(n, dtype=np.int64)
    ang = (2.0 * np.pi / period) * ((k[:, None] * k[None, :]) % period)
    return np.cos(ang), np.sin(ang)


def _fourier_kernel(u_ref, w_ref, cd_ref, sd_ref, cs_ref, o_ref, wc_ref, ws_ref, v_ref, *, scale):
    l, c = u_ref.shape
    dg = cd_ref.shape[0]

    @pl.when(pl.program_id(0) == 0)
    def _():
        wc_ref[...] = jnp.zeros_like(wc_ref)
        ws_ref[...] = jnp.zeros_like(ws_ref)
        for g in range(c // dg):
            sl = slice(g * dg, (g + 1) * dg)
            wg = w_ref[g]
            wc_ref[sl, sl] = jnp.dot(cd_ref[...], wg, precision=lax.Precision.HIGHEST,
                                     preferred_element_type=F32).astype(BF16)
            ws_ref[sl, sl] = jnp.dot(sd_ref[...], wg, precision=lax.Precision.HIGHEST,
                                     preferred_element_type=F32).astype(BF16)

    u = u_ref[...].astype(BF16)
    v_ref[0:l, :] = jnp.dot(u, wc_ref[...], preferred_element_type=F32).astype(BF16)
    v_ref[l:2 * l, :] = jnp.dot(u, ws_ref[...], preferred_element_type=F32).astype(BF16)
    y = jnp.dot(cs_ref[...], v_ref[...], preferred_element_type=F32)
    o_ref[...] = (y * scale).astype(o_ref.dtype)


def fourier_mix(p, col, w):
    bsz, l, _ = p.shape
    g, dg, _ = w.shape
    c = g * dg
    cl, sl = _cos_sin(l, l)
    cd, sd = _cos_sin(dg, dg)
    cs = jnp.asarray(np.concatenate([cl, -sl], axis=1), BF16)
    return pl.pallas_call(
        functools.partial(_fourier_kernel, scale=float((l * dg) ** -0.5)),
        grid=(bsz,),
        in_specs=[pl.BlockSpec((None, l, c), lambda b: (b, 0, col // c)),
                  pl.BlockSpec((g, dg, dg), lambda b: (0, 0, 0)),
                  pl.BlockSpec((dg, dg), lambda b: (0, 0)),
                  pl.BlockSpec((dg, dg), lambda b: (0, 0)),
                  pl.BlockSpec((l, 2 * l), lambda b: (0, 0), pipeline_mode=pl.Buffered(1))],
        out_specs=pl.BlockSpec((None, l, c), lambda b: (b, 0, 0)),
        out_shape=jax.ShapeDtypeStruct((bsz, l, c), BF16),
        scratch_shapes=[pltpu.VMEM((c, c), BF16), pltpu.VMEM((c, c), BF16), pltpu.VMEM((2 * l, c), BF16)],
        compiler_params=_cparams(("arbitrary",)),
        name="fourier_mix",
    )(p, w, jnp.asarray(cd, F32), jnp.asarray(sd, F32), cs)


HY_CBLK = 256
HY_FBLK = 512


def _hyena_dft_matrix(l):
    k = np.arange(l, dtype=np.int64)
    ang = (np.pi / l) * ((k[:, None] * k[None, :]) % (2 * l))
    sn = np.sin(ang)
    sn[0, :] = 1.0 - 2.0 * (k % 2)
    return jnp.asarray(np.concatenate([np.cos(ang), sn], axis=0), BF16)


def _hyena_pos_features(l):
    t = np.linspace(0.0, 1.0, l)[:, None]
    w = (2.0 * np.pi / l) * np.arange(l)[:, None]
    f = np.linspace(1e-4, HY_BANDS - 1, HY_BANDS)[None, :]
    z = np.concatenate([t, np.cos(f * w), -np.sin(f * w)], axis=-1)
    return np.pad(z, ((0, 0), (0, LANES - z.shape[1])))


def _split_bf16(x):
    hi = x.astype(BF16)
    return hi, (x - hi.astype(F32)).astype(BF16)


def _hyena_filter_kernel(z_ref, w1_ref, b1_ref, w2_ref, b2_ref, fr_ref, w3f_ref, w3b_ref, dl_ref, m_ref,
                         p_ref, q_ref):
    l = z_ref.shape[0]
    hp = lax.Precision.HIGHEST
    z = z_ref[...]
    h = jnp.sin(fr_ref[0:1, :] * (jnp.dot(z, w1_ref[...], precision=hp, preferred_element_type=F32)
                                  + b1_ref[...]))
    h = jnp.sin(fr_ref[1:2, :] * (jnp.dot(h, w2_ref[...], precision=hp, preferred_element_type=F32)
                                  + b2_ref[...]))
    window = jnp.exp(-z[:, 0:1] * dl_ref[...]) + HY_DECAY_SHIFT
    hf = jnp.dot(h, w3f_ref[...], precision=hp, preferred_element_type=F32) * window
    hb = jnp.dot(h, w3b_ref[...], precision=hp, preferred_element_type=F32) * window
    norm = jnp.sum(jnp.abs(hf) + jnp.abs(hb), axis=0, keepdims=True) + EPS
    hf = hf / norm
    hb = hb / norm
    g1h, g1l = _split_bf16(hf + hb)
    g2h, g2l = _split_bf16(hb - hf)
    f1 = (jnp.dot(m_ref[...], g1h, preferred_element_type=F32)
          + jnp.dot(m_ref[...], g1l, preferred_element_type=F32))
    f2 = (jnp.dot(m_ref[l:2 * l, :], g2h, preferred_element_type=F32)
          + jnp.dot(m_ref[l:2 * l, :], g2l, preferred_element_type=F32))
    p_ref[...] = f1[0:l]
    row = lax.broadcasted_iota(jnp.int32, f2.shape, 0)
    q_ref[...] = jnp.where(row == 0, f1[l:l + 1], f2)


def hyena_filter_spectrum(l, w1, b1, w2, b2, w3, freq, m):
    c = w3.shape[1] // 2
    hid = w1.shape[1]
    z = jnp.asarray(_hyena_pos_features(l), F32)
    w1p = jnp.pad(w1, ((0, z.shape[1] - w1.shape[0]), (0, 0)))
    deltas = np.abs(np.linspace(math.log(HY_DECAY_TARGET) / HY_SLOW_PCT,
                                math.log(HY_DECAY_TARGET) / HY_FAST_PCT, c))[None, :]
    nb = c // HY_CBLK
    full = lambda shape: pl.BlockSpec(shape, lambda j: (0,) * len(shape))
    return pl.pallas_call(
        _hyena_filter_kernel,
        grid=(nb,),
        in_specs=[full(z.shape), full(w1p.shape), full((1, hid)), full(w2.shape), full((1, hid)),
                  full((2, hid)),
                  pl.BlockSpec((hid, HY_CBLK), lambda j: (0, j)),
                  pl.BlockSpec((hid, HY_CBLK), lambda j: (0, nb + j)),
                  pl.BlockSpec((1, HY_CBLK), lambda j: (0, j)),
                  pl.BlockSpec(m.shape, lambda j: (0, 0), pipeline_mode=pl.Buffered(1))],
        out_specs=[pl.BlockSpec((l, HY_CBLK), lambda j: (0, j))] * 2,
        out_shape=[jax.ShapeDtypeStruct((l, c), F32)] * 2,
        compiler_params=_cparams(("arbitrary",)),
        name="hyena_filter",
    )(z, w1p, b1.reshape(1, hid), w2, b2.reshape(1, hid), freq, w3, w3, jnp.asarray(deltas, F32), m)


def _shift_rows(u, down):
    l = u.shape[0]
    row = lax.broadcasted_iota(jnp.int32, u.shape, 0)
    if down:
        return jnp.where(row == 0, 0.0, pltpu.roll(u, 1, 0))
    return jnp.where(row == l - 1, 0.0, pltpu.roll(u, l - 1, 0))


def _dwconv3(u, taps):
    return (taps[3:4] + _shift_rows(u, True) * taps[0:1] + u * taps[1:2] + _shift_rows(u, False) * taps[2:3])


def _hyena_conv_kernel(x0_ref, x1_ref, v_ref, taps_ref, bias_ref, p_ref, q_ref, m_ref, o_ref):
    l, c = x0_ref.shape
    x1 = _dwconv3(x1_ref[...], taps_ref[1])
    s = _dwconv3(v_ref[...], taps_ref[2]) * x1
    sb = s.astype(BF16)
    fb = min(HY_FBLK, l)
    nblk = l // fb
    inv_l = 1.0 / l

    def forward(i):
        return (jnp.dot(m_ref[i * fb:(i + 1) * fb, :], sb, preferred_element_type=F32),
                jnp.dot(m_ref[l + i * fb:l + (i + 1) * fb, :], sb, preferred_element_type=F32))

    y_cos_acc = y_sin_acc = nyquist = None
    ab = forward(0)
    for i in range(nblk):
        a, b = ab
        if i + 1 < nblk:
            ab = forward(i + 1)
        p, q = p_ref[i * fb:(i + 1) * fb, :], q_ref[i * fb:(i + 1) * fb, :]
        bq = b * q
        if i == 0:
            first = lax.broadcasted_iota(jnp.int32, (fb, c), 0) == 0
            nyquist = bq[0:1] * (0.5 * inv_l)
            y_cos = (a * p + jnp.where(first, 0.0, bq)) * jnp.where(first, 0.5 * inv_l, inv_l)
            y_sin = jnp.where(first, 0.0, (b * p - a * q) * inv_l)
        else:
            y_cos = (a * p + bq) * inv_l
            y_sin = (b * p - a * q) * inv_l
        d_cos = jnp.dot(m_ref[0:l, i * fb:(i + 1) * fb], y_cos.astype(BF16), preferred_element_type=F32)
        d_sin = jnp.dot(m_ref[l:2 * l, i * fb:(i + 1) * fb], y_sin.astype(BF16), preferred_element_type=F32)
        y_cos_acc = d_cos if y_cos_acc is None else y_cos_acc + d_cos
        y_sin_acc = d_sin if y_sin_acc is None else y_sin_acc + d_sin
    row = lax.broadcasted_iota(jnp.int32, (l, c), 0)
    y = (y_cos_acc + jnp.where(row == 0, 0.0, y_sin_acc)
         + jnp.where(row % 2 == 0, 1.0, -1.0) * nyquist)
    x0 = _dwconv3(x0_ref[...], taps_ref[0])
    o_ref[...] = ((y + s * bias_ref[...]) * x0).astype(o_ref.dtype)


def hyena_conv(p, col, conv_w, conv_b, bias, spec_p, spec_q, m):
    bsz, l, _ = p.shape
    c = bias.shape[-1]
    nb = c // HY_CBLK
    cb0 = col // HY_CBLK
    taps = jnp.concatenate([conv_w, conv_b[None]], axis=0)
    taps = taps.reshape(4, 3, nb, HY_CBLK).transpose(2, 1, 0, 3)

    def part(k):
        return pl.BlockSpec((None, l, HY_CBLK), lambda j, b: (b, 0, cb0 + k * nb + j))

    return pl.pallas_call(
        _hyena_conv_kernel,
        grid=(nb, bsz),
        in_specs=[part(0), part(1), part(2),
                  pl.BlockSpec((None, 3, 4, HY_CBLK), lambda j, b: (j, 0, 0, 0)),
                  pl.BlockSpec((1, HY_CBLK), lambda j, b: (0, j)),
                  pl.BlockSpec((l, HY_CBLK), lambda j, b: (0, j)),
                  pl.BlockSpec((l, HY_CBLK), lambda j, b: (0, j)),
                  pl.BlockSpec(m.shape, lambda j, b: (0, 0), pipeline_mode=pl.Buffered(1))],
        out_specs=pl.BlockSpec((None, l, HY_CBLK), lambda j, b: (b, 0, j)),
        out_shape=jax.ShapeDtypeStruct((bsz, l, c), BF16),
        compiler_params=_cparams(("arbitrary", "arbitrary")),
        name="hyena_conv",
    )(p, p, p, taps, bias.reshape(1, c), spec_p, spec_q, m)


GLA_HP = 2
GLA_ROPE_PAIR = 16
GLA_SCAN_UNROLL = 4


def _gla_rope_tables(l, dk):
    half = dk // 2
    nf = half // 2
    assert nf == GLA_ROPE_PAIR
    inv = ROPE_THETA ** (-np.arange(nf, dtype=np.float64) / nf)
    t = np.arange(l)
    ang_r = (t // GRID_W)[:, None] * inv
    ang_c = (t % GRID_W)[:, None] * inv
    cos = np.concatenate([np.cos(ang_r)] * 2 + [np.cos(ang_c)] * 2, axis=1)
    sin = np.concatenate([-np.sin(ang_r), np.sin(ang_r), -np.sin(ang_c), np.sin(ang_c)], axis=1)
    return (jnp.asarray(np.tile(cos, (1, GLA_HP)), F32), jnp.asarray(np.tile(sin, (1, GLA_HP)), F32))


def _rope(x, cos, sin):
    lane = lax.broadcasted_iota(jnp.int32, x.shape, 1)
    lanes = x.shape[1]
    partner = jnp.where(lane % (2 * GLA_ROPE_PAIR) < GLA_ROPE_PAIR,
                        pltpu.roll(x, lanes - GLA_ROPE_PAIR, 1), pltpu.roll(x, GLA_ROPE_PAIR, 1))
    return x * cos + partner * sin


def _log_sigmoid(x):
    return jnp.minimum(x, 0.0) - jnp.log(1.0 + jnp.exp(-jnp.abs(x)))


def _gla_kernel(*refs, with_ctx_out):
    (q_ref, k_ref, v_ref, r_ref, z_ref, cq_ref, ck_ref, cv_ref, cr_ref, cz_ref,
     wz_ref, bz_ref, g_ref, cos_ref, sin_ref) = refs[:15]
    if with_ctx_out:
        o_ref, oc_ref = refs[15:17]
        scratch = refs[17:]
    else:
        o_ref, oc_ref = refs[15], None
        scratch = refs[16:]
    qs, ks, las, ofs, obs, cqs, cks, clas, cofs, cobs, st_f, st_b = scratch
    dk2 = q_ref.shape[1]
    dv2 = v_ref.shape[1]
    dk, dv = dk2 // GLA_HP, dv2 // GLA_HP
    ch = GLA_CHUNK
    hp = lax.Precision.HIGHEST

    def gates(z):
        pre = jnp.dot(z, wz_ref[...], precision=hp, preferred_element_type=F32) + bz_ref[...]
        return _log_sigmoid(pre) * (1.0 / GLA_TAU)

    qs[...] = _rope(q_ref[...] * dk ** -0.5, cos_ref[...], sin_ref[...])
    ks[...] = _rope(k_ref[...], cos_ref[...], sin_ref[...])
    las[...] = gates(z_ref[...])
    cqs[...] = cq_ref[...] * dk ** -0.5
    cks[...] = ck_ref[...]
    clas[...] = gates(cz_ref[...])
    st_f[...] = jnp.zeros_like(st_f)
    st_b[...] = jnp.zeros_like(st_b)

    ri = lax.broadcasted_iota(jnp.int32, (ch, ch), 0)
    ci = lax.broadcasted_iota(jnp.int32, (ch, ch), 1)
    tri = {False: ri >= ci, True: ri <= ci}
    tri_b16 = {d: jnp.where(m, 1.0, 0.0).astype(BF16) for d, m in tri.items()}
    tri2 = {d: jnp.concatenate([m] * GLA_HP, axis=0) for d, m in tri.items()}
    lane_head = lax.broadcasted_iota(jnp.int32, (ch, dk2), 1) // dk
    st_r = lax.broadcasted_iota(jnp.int32, (dv2, dk2), 0) // dv
    st_c = lax.broadcasted_iota(jnp.int32, (dv2, dk2), 1) // dk
    st_diag = st_r == st_c

    def scan(q_s, k_s, la_s, v_in, of_s, ob_s, n):
        unroll = GLA_SCAN_UNROLL

        def body(trip, carry):
            items = []
            for u in range(unroll):
                c = trip * unroll + u
                items.append((False, pl.ds(pl.multiple_of(c * ch, ch), ch)))
                items.append((True, pl.ds(pl.multiple_of((n - 1 - c) * ch, ch), ch)))
            v_c = [v_in[rows, :].astype(BF16) for _, rows in items]
            cum = []
            for bw, rows in items:
                la_hi, la_lo = _split_bf16(la_s[rows, dk2:2 * dk2] if bw else la_s[rows, 0:dk2])
                cum.append(jnp.dot(tri_b16[bw], la_hi, preferred_element_type=F32)
                           + jnp.dot(tri_b16[bw], la_lo, preferred_element_type=F32))
            q_dec, k_end, decay, sc = [], [], [], []
            for i, (bw, rows) in enumerate(items):
                q_c, k_c = q_s[rows, :], k_s[rows, :]
                tot = cum[i][0:1] if bw else cum[i][ch - 1:ch]
                qd = q_c * jnp.exp(cum[i])
                k_inv = (k_c * jnp.exp(-cum[i])).astype(BF16)
                k_end.append((k_c * jnp.exp(tot - cum[i])).astype(BF16))
                decay.append(jnp.exp(tot))
                q_heads = jnp.concatenate([jnp.where(lane_head == h, qd, 0.0) for h in range(GLA_HP)], axis=0)
                sc.append(_dot_nt(q_heads.astype(BF16), k_inv))
                q_dec.append(qd.astype(BF16))
            o_intra, ds_t = [], []
            for i, (bw, rows) in enumerate(items):
                pv = jnp.dot(jnp.where(tri2[bw], sc[i], 0.0).astype(BF16), v_c[i],
                             preferred_element_type=F32)
                o_intra.append(jnp.concatenate(
                    [pv[h * ch:(h + 1) * ch, h * dv:(h + 1) * dv] for h in range(GLA_HP)], axis=1))
                ds_t.append(lax.dot_general(v_c[i], k_end[i], (((0,), (0,)), ((), ())),
                                            preferred_element_type=F32))
            for i, (bw, rows) in enumerate(items):
                st, o_s = (st_b, ob_s) if bw else (st_f, of_s)
                s_t = st[...]
                o_s[rows, :] = o_intra[i] + _dot_nt(q_dec[i], s_t.astype(BF16))
                st[...] = s_t * decay[i] + jnp.where(st_diag, ds_t[i], 0.0)
            return carry

        lax.fori_loop(0, n // unroll, body, 0)

    def finish(of_s, ob_s, gate_ref, out_ref, n_blocks, blk):
        def body(i, carry):
            rows = pl.ds(pl.multiple_of(i * blk, blk), blk)
            o = of_s[rows, :] + ob_s[rows, :]
            gate = gate_ref[rows, :]
            parts = []
            for h in range(GLA_HP):
                oh = o[:, h * dv:(h + 1) * dv]
                gh = gate[:, h * dv:(h + 1) * dv]
                parts.append(_rms(oh, g_ref[...]) * (gh * jax.nn.sigmoid(gh)))
            out_ref[rows, :] = jnp.concatenate(parts, axis=1).astype(out_ref.dtype)
            return carry
        lax.fori_loop(0, n_blocks, body, 0)

    l, lc = q_ref.shape[0], cq_ref.shape[0]
    scan(cqs, cks, clas, cv_ref, cofs, cobs, lc // ch)
    if with_ctx_out:
        finish(cofs, cobs, cr_ref, oc_ref, 1, lc)
    scan(qs, ks, las, v_ref, ofs, obs, l // ch)
    finish(ofs, obs, r_ref, o_ref, l // lc, lc)


def gla(px, pxz, pc, pcz, w_gate, b_gate, out_gain, with_ctx_out):
    bsz, l, _ = px.shape
    lc = pc.shape[1]
    dv = out_gain.shape[-1]
    hdk = w_gate.shape[-1]
    dk = hdk // GLA_HEADS
    dk2, dv2 = GLA_HP * dk, GLA_HP * dv
    nhp = GLA_HEADS // GLA_HP
    zw = pxz.shape[-1]
    rank = w_gate.shape[1]
    wz = jnp.zeros((nhp, zw, 2 * dk2), F32)
    for u in range(2):
        blk = w_gate[u].reshape(rank, nhp, dk2).transpose(1, 0, 2)
        wz = wz.at[:, u * rank:(u + 1) * rank, u * dk2:(u + 1) * dk2].set(blk)
    bz = b_gate.reshape(2, nhp, dk2).transpose(1, 0, 2).reshape(nhp, 1, 2 * dk2)
    cos, sin = _gla_rope_tables(l, dk)
    k_cb, v_cb, r_cb = hdk // dk2, 2 * hdk // dv2, (2 * hdk + GLA_HEADS * dv) // dv2

    def col(rows, width, cb):
        return pl.BlockSpec((None, rows, width), lambda b, j: (b, 0, cb + j))

    def whole(rows, width):
        return pl.BlockSpec((None, rows, width), lambda b, j: (b, 0, 0))

    const = lambda shape: pl.BlockSpec(shape, lambda b, j: (0,) * len(shape))
    in_specs = [col(l, dk2, 0), col(l, dk2, k_cb), col(l, dv2, v_cb), col(l, dv2, r_cb), whole(l, zw),
                col(lc, dk2, 0), col(lc, dk2, k_cb), col(lc, dv2, v_cb), col(lc, dv2, r_cb), whole(lc, zw),
                pl.BlockSpec((None, zw, 2 * dk2), lambda b, j: (j, 0, 0)),
                pl.BlockSpec((None, 1, 2 * dk2), lambda b, j: (j, 0, 0)),
                const((1, dv)), const((l, dk2)), const((l, dk2))]
    out_specs = [pl.BlockSpec((None, l, dv2), lambda b, j: (b, 0, j))]
    out_shape = [jax.ShapeDtypeStruct((bsz, l, GLA_HEADS * dv), BF16)]
    if with_ctx_out:
        out_specs.append(pl.BlockSpec((None, lc, dv2), lambda b, j: (b, 0, j)))
        out_shape.append(jax.ShapeDtypeStruct((bsz, lc, GLA_HEADS * dv), BF16))
    scratch = [pltpu.VMEM((l, dk2), F32), pltpu.VMEM((l, dk2), F32), pltpu.VMEM((l, 2 * dk2), F32),
               pltpu.VMEM((l, dv2), F32), pltpu.VMEM((l, dv2), F32),
               pltpu.VMEM((lc, dk2), F32), pltpu.VMEM((lc, dk2), F32), pltpu.VMEM((lc, 2 * dk2), F32),
               pltpu.VMEM((lc, dv2), F32), pltpu.VMEM((lc, dv2), F32),
               pltpu.VMEM((dv2, dk2), F32), pltpu.VMEM((dv2, dk2), F32)]
    outs = pl.pallas_call(
        functools.partial(_gla_kernel, with_ctx_out=with_ctx_out),
        grid=(bsz, nhp),
        in_specs=in_specs, out_specs=out_specs, out_shape=out_shape, scratch_shapes=scratch,
        compiler_params=_cparams(("arbitrary", "arbitrary")),
        name="gla",
    )(px, px, px, px, pxz, pc, pc, pc, pc, pcz, wz, bz, out_gain.reshape(1, dv), cos, sin)
    return outs if with_ctx_out else (outs[0], None)


def kernel(x, c, ctx, c_ctx, w_mod, b_mod, g_mix, w_in, gla_gate_w, gla_gate_b, gla_out_g,
           hy_conv_w, hy_conv_b, hy_w1, hy_b1, hy_w2, hy_b2, hy_w3, hy_freq, hy_bias, fn_w,
           na_q_g, na_k_g, na_rpb, w_out, g_ffn, ffn_w_up, ffn_conv_w, ffn_conv_b, ffn_w_down):
    bsz, l_lat, d = x.shape
    l_ctx = ctx.shape[1]
    depth = w_mod.shape[0]
    w = d // 4

    n_cond = -(-(bsz + 1) // SUBLANES) * SUBLANES
    cond = jnp.zeros((n_cond, d), F32).at[:bsz].set(c).at[bsz].set(c_ctx)
    mods = mod_vectors(cond, w_mod, b_mod).reshape(depth, n_cond, N_MOD, d)
    mods = jnp.pad(mods, ((0, 0), (0, 0), (0, MOD_ROWS - N_MOD), (0, 0)))

    z0 = 3 * w
    zw = 2 * GLA_GATE_RANK
    w_main = jnp.concatenate([w_in[:, :, :z0], w_in[:, :, z0 + zw:]], axis=-1).astype(BF16)
    w_z = jnp.pad(w_in[:, :, z0:z0 + zw], ((0, 0), (0, 0), (0, LANES - zw))).astype(BF16)
    w_out_b = w_out.astype(BF16)
    w_up_b = ffn_w_up.astype(BF16)
    w_down_b = ffn_w_down.astype(BF16)

    m_lat = _hyena_dft_matrix(l_lat)
    m_ctx = _hyena_dft_matrix(l_ctx)
    ctx_flat = ctx.reshape(1, bsz * l_ctx, d)
    for layer in range(depth):
        last = layer == depth - 1
        mod_x = mods[layer, :bsz]
        mod_c = mods[layer, bsz:bsz + 1]
        g_mix_l = g_mix[layer].reshape(1, d)
        g_ffn_l = g_ffn[layer].reshape(1, d)

        px, pxz = in_proj(x, mod_x, g_mix_l, w_main[layer], w_z[layer], tm=1024)
        pc, pcz = in_proj(ctx_flat, mod_c, g_mix_l, w_main[layer], w_z[layer], tm=1024)
        pc = pc.reshape(bsz, l_ctx, -1)
        pcz = pcz.reshape(bsz, l_ctx, -1)

        y_a, yc_a = gla(px, pxz, pc, pcz, gla_gate_w[layer], gla_gate_b[layer], gla_out_g[layer],
                        with_ctx_out=not last)
        hy_filt = (hy_w1[layer], hy_b1[layer], hy_w2[layer], hy_b2[layer], hy_w3[layer], hy_freq[layer])
        hy_p, hy_q = hyena_filter_spectrum(l_lat, *hy_filt, m_lat)
        y_b = hyena_conv(px, 3 * w, hy_conv_w[layer], hy_conv_b[layer], hy_bias[layer], hy_p, hy_q, m_lat)
        y_c = fourier_mix(px, 6 * w, fn_w[layer])
        na_bias = natten_bias_table(na_rpb[layer], l_lat // GRID_W)
        y_d, yc_d = natten(px, pc, 7 * w, na_q_g[layer].reshape(1, -1), na_k_g[layer].reshape(1, -1),
                           na_bias, with_ctx_out=not last)
        x = out_proj([y_a, y_b, y_c, y_d], w_out_b[layer], x, mod_x, tm=512)
        cw = ffn_conv_w[layer]
        cb = ffn_conv_b[layer].reshape(1, -1)
        x = conv_ffn(x, mod_x, g_ffn_l, w_up_b[layer], cw, cb, w_down_b[layer], tm=1024,
                     single_buffer_x=True)

        if not last:
            hc_p, hc_q = hyena_filter_spectrum(l_ctx, *hy_filt, m_ctx)
            yc_b = hyena_conv(pc, 3 * w, hy_conv_w[layer], hy_conv_b[layer], hy_bias[layer], hc_p, hc_q, m_ctx)
            yc_c = fourier_mix(pc, 6 * w, fn_w[layer])
            ycs = [y.reshape(1, bsz * l_ctx, w) for y in (yc_a, yc_b, yc_c, yc_d)]
            ctx_flat = out_proj(ycs, w_out_b[layer], ctx_flat, mod_c, tm=512)
            ctx_flat = conv_ffn(ctx_flat, mod_c, g_ffn_l, w_up_b[layer], cw, cb, w_down_b[layer],
                                tm=512, seg_len=l_ctx)
    return x
```

```python
import functools
import math

import numpy as np
import jax
import jax.numpy as jnp
from jax import lax
from jax.experimental import pallas as pl
from jax.experimental.pallas import tpu as pltpu

F32 = jnp.float32
BF16 = jnp.bfloat16

GRID_W = 64
GLA_HEADS = 4
GLA_GATE_RANK = 16
GLA_TAU = 16.0
GLA_CHUNK = 64
HY_BANDS = 16
HY_DECAY_TARGET = 1e-2
HY_FAST_PCT = 0.3
HY_SLOW_PCT = 1.5
HY_DECAY_SHIFT = 0.05
FN_GROUPS = 4
NA_HEADS = 4
NA_KR_MAX = 8
NA_KC = 16
ROPE_THETA = 10000.0
N_MOD = 6
EPS = 1e-6

V7X_VMEM_LIMIT = 58 * 1024 * 1024
SUBLANES = 8
LANES = 128
MOD_ROWS = 8
FFN_HALO = 16


def _cparams(sem):
    return pltpu.CompilerParams(dimension_semantics=sem, vmem_limit_bytes=V7X_VMEM_LIMIT)


def _mod_kernel(s_ref, w_ref, b_ref, o_ref):
    s = s_ref[...]
    s = s * jax.nn.sigmoid(s)
    o_ref[...] = jnp.dot(s.astype(BF16), w_ref[...].astype(BF16),
                         preferred_element_type=F32) + b_ref[...]


def mod_vectors(cond, w_mod, b_mod, tn=1024):
    depth, d, n = w_mod.shape
    r = cond.shape[0]
    return pl.pallas_call(
        _mod_kernel,
        grid=(depth, n // tn),
        in_specs=[pl.BlockSpec((r, d), lambda l, j: (0, 0)),
                  pl.BlockSpec((None, d, tn), lambda l, j: (l, 0, j)),
                  pl.BlockSpec((None, 1, tn), lambda l, j: (l, 0, j))],
        out_specs=pl.BlockSpec((None, r, tn), lambda l, j: (l, 0, j)),
        out_shape=jax.ShapeDtypeStruct((depth, r, n), F32),
        compiler_params=_cparams(("arbitrary", "arbitrary")),
        name="mod_vectors",
    )(cond, w_mod, b_mod.reshape(depth, 1, n))


def _norm_mod(x, gain, shift, scale):
    ms = jnp.mean(x * x, axis=-1, keepdims=True)
    return (x * lax.rsqrt(ms + EPS) * gain) * (1.0 + scale) + shift


def _inproj_kernel(x_ref, mod_ref, g_ref, w_ref, wz_ref, o_ref, oz_ref, h_ref):
    @pl.when(pl.program_id(2) == 0)
    def _():
        h = _norm_mod(x_ref[...], g_ref[...], mod_ref[0:1, :], mod_ref[1:2, :]).astype(BF16)
        h_ref[...] = h
        oz_ref[...] = jnp.dot(h, wz_ref[...], preferred_element_type=F32)

    o_ref[...] = jnp.dot(h_ref[...], w_ref[...], preferred_element_type=F32)


def in_proj(x, mod, gain, w_main, w_z, tm, tn=1024):
    g, l, d = x.shape
    n = w_main.shape[1]
    nz = w_z.shape[1]
    return pl.pallas_call(
        _inproj_kernel,
        grid=(g, l // tm, n // tn),
        in_specs=[pl.BlockSpec((None, tm, d), lambda b, i, j: (b, i, 0)),
                  pl.BlockSpec((None, MOD_ROWS, d), lambda b, i, j: (b, 0, 0)),
                  pl.BlockSpec((1, d), lambda b, i, j: (0, 0)),
                  pl.BlockSpec((d, tn), lambda b, i, j: (0, j)),
                  pl.BlockSpec((d, nz), lambda b, i, j: (0, 0))],
        out_specs=[pl.BlockSpec((None, tm, tn), lambda b, i, j: (b, i, j)),
                   pl.BlockSpec((None, tm, nz), lambda b, i, j: (b, i, 0))],
        out_shape=[jax.ShapeDtypeStruct((g, l, n), F32),
                   jax.ShapeDtypeStruct((g, l, nz), F32)],
        scratch_shapes=[pltpu.VMEM((tm, d), BF16)],
        compiler_params=_cparams(("arbitrary", "arbitrary", "arbitrary")),
        name="in_proj",
    )(x, mod, gain, w_main, w_z)


def _outproj_kernel(ya_ref, yb_ref, yc_ref, yd_ref, w_ref, x_ref, mod_ref, o_ref):
    kw = ya_ref.shape[-1]
    acc = jnp.dot(ya_ref[...], w_ref[0 * kw:1 * kw, :], preferred_element_type=F32)
    acc += jnp.dot(yb_ref[...], w_ref[1 * kw:2 * kw, :], preferred_element_type=F32)
    acc += jnp.dot(yc_ref[...], w_ref[2 * kw:3 * kw, :], preferred_element_type=F32)
    acc += jnp.dot(yd_ref[...], w_ref[3 * kw:4 * kw, :], preferred_element_type=F32)
    o_ref[...] = x_ref[...] + mod_ref[2:3, :] * acc


def out_proj(ys, w_out, x, mod, tm):
    g, l, d = x.shape
    kw = ys[0].shape[-1]
    yspec = pl.BlockSpec((None, tm, kw), lambda b, i: (b, i, 0))
    return pl.pallas_call(
        _outproj_kernel,
        grid=(g, l // tm),
        in_specs=[yspec, yspec, yspec, yspec,
                  pl.BlockSpec((4 * kw, d), lambda b, i: (0, 0)),
                  pl.BlockSpec((None, tm, d), lambda b, i: (b, i, 0)),
                  pl.BlockSpec((None, MOD_ROWS, d), lambda b, i: (b, 0, 0))],
        out_specs=pl.BlockSpec((None, tm, d), lambda b, i: (b, i, 0)),
        out_shape=jax.ShapeDtypeStruct((g, l, d), F32),
        compiler_params=_cparams(("arbitrary", "arbitrary")),
        name="out_proj",
    )(*ys, w_out, x, mod)


def _ffn_kernel(x_ref, xp_ref, xn_ref, mod_ref, g_ref, wa_ref, wg_ref, cw_ref, cb_ref, wd_ref,
                o_ref, h_ref, act0_ref, act1_ref, *, seg_len, n_hidden_tiles):
    i = pl.program_id(1)
    j = pl.program_id(2)
    nf = n_hidden_tiles
    acts = (act0_ref, act1_ref)
    tm = x_ref.shape[0]
    tf = wa_ref.shape[1]
    hs = FFN_HALO

    def prologue():
        gain, shift, scale = g_ref[...], mod_ref[3:4, :], mod_ref[4:5, :]
        h_ref[hs:hs + tm, :] = _norm_mod(x_ref[...], gain, shift, scale).astype(BF16)
        hp = jnp.where(i > 0, _norm_mod(xp_ref[...], gain, shift, scale), 0.0)
        hn = jnp.where(i < pl.num_programs(1) - 1, _norm_mod(xn_ref[...], gain, shift, scale), 0.0)
        zero = jnp.zeros_like(hp)
        h_ref[0:hs, :] = jnp.concatenate([zero, hp], axis=0).astype(BF16)
        h_ref[hs + tm:2 * hs + tm, :] = jnp.concatenate([hn, zero], axis=0).astype(BF16)
        o_ref[...] = x_ref[...]

    def conv(u, c0, width):
        w = cw_ref[:, c0:c0 + width]
        prev, nxt = u[hs - 1:hs - 1 + tm], u[hs + 1:hs + 1 + tm]
        if seg_len is not None:
            pos = lax.broadcasted_iota(jnp.int32, prev.shape, 0) % seg_len
            prev = jnp.where(pos == 0, 0.0, prev)
            nxt = jnp.where(pos == seg_len - 1, 0.0, nxt)
        return cb_ref[:, c0:c0 + width] + prev * w[0:1] + u[hs:hs + tm] * w[1:2] + nxt * w[2:3]

    def up_matmuls():
        h = h_ref[...]
        return (jnp.dot(h, wa_ref[...], preferred_element_type=F32),
                jnp.dot(h, wg_ref[...], preferred_element_type=F32))

    def gate_to(act_ref, ua, ug):
        a = conv(ua, 0, tf)
        gt = conv(ug, tf, tf)
        act_ref[...] = (a * (gt * jax.nn.sigmoid(gt))).astype(BF16)

    def down_from(act_ref):
        o_ref[...] += mod_ref[5:6, :] * jnp.dot(act_ref[...], wd_ref[...], preferred_element_type=F32)

    middle = jnp.logical_and(j > 0, j < nf)

    @pl.when(j == 0)
    def _():
        prologue()
        ua, ug = up_matmuls()
        gate_to(acts[0], ua, ug)

    for parity in range(2):
        @pl.when(jnp.logical_and(middle, j % 2 == parity))
        def _():
            ua, ug = up_matmuls()
            down_from(acts[1 - parity])
            gate_to(acts[parity], ua, ug)

    @pl.when(j == nf)
    def _():
        down_from(acts[(n_hidden_tiles - 1) % 2])


def conv_ffn(x, mod, gain, w_up, conv_w, conv_b, w_down, tm, tf=512, seg_len=None):
    g, l, d = x.shape
    assert seg_len is None or (tm % seg_len == 0 and l % tm == 0)
    f = w_down.shape[0]
    nf = f // tf
    nb = tm // SUBLANES
    last = l // SUBLANES - 1
    cw = conv_w.reshape(3, 2, nf, tf).transpose(2, 0, 1, 3).reshape(nf, 3, 2 * tf)
    cb = conv_b.reshape(1, 2, nf, tf).transpose(2, 0, 1, 3).reshape(nf, 1, 2 * tf)
    return pl.pallas_call(
        functools.partial(_ffn_kernel, seg_len=seg_len, n_hidden_tiles=nf),
        grid=(g, l // tm, nf + 1),
        in_specs=[pl.BlockSpec((None, tm, d), lambda b, i, j: (b, i, 0)),
                  pl.BlockSpec((None, SUBLANES, d), lambda b, i, j: (b, jnp.maximum(i * nb - 1, 0), 0)),
                  pl.BlockSpec((None, SUBLANES, d), lambda b, i, j: (b, jnp.minimum((i + 1) * nb, last), 0)),
                  pl.BlockSpec((None, MOD_ROWS, d), lambda b, i, j: (b, 0, 0)),
                  pl.BlockSpec((1, d), lambda b, i, j: (0, 0)),
                  pl.BlockSpec((d, tf), lambda b, i, j: (0, jnp.minimum(j, nf - 1))),
                  pl.BlockSpec((d, tf), lambda b, i, j: (0, nf + jnp.minimum(j, nf - 1))),
                  pl.BlockSpec((None, 3, 2 * tf), lambda b, i, j: (jnp.minimum(j, nf - 1), 0, 0)),
                  pl.BlockSpec((None, 1, 2 * tf), lambda b, i, j: (jnp.minimum(j, nf - 1), 0, 0)),
                  pl.BlockSpec((tf, d), lambda b, i, j: (jnp.maximum(j - 1, 0), 0))],
        out_specs=pl.BlockSpec((None, tm, d), lambda b, i, j: (b, i, 0)),
        out_shape=jax.ShapeDtypeStruct((g, l, d), F32),
        scratch_shapes=[pltpu.VMEM((tm + 2 * FFN_HALO, d), BF16),
                        pltpu.VMEM((tm, tf), BF16), pltpu.VMEM((tm, tf), BF16)],
        compiler_params=_cparams(("arbitrary", "arbitrary", "arbitrary")),
        name="conv_ffn",
    )(x, x, x, mod, gain, w_up, w_up, cw, cb, w_down)


def natten_bias_table(rpb, n_rows):
    kr = min(NA_KR_MAX, n_rows)
    h, n_dr, n_dc = rpb.shape
    col = np.arange(GRID_W)
    col0 = np.clip(col - NA_KC // 2, 0, GRID_W - NA_KC)
    in_win = (col[None, :] >= col0[:, None]) & (col[None, :] < col0[:, None] + NA_KC)
    lo = GRID_W - NA_KC
    ext = jnp.pad(rpb.astype(F32), ((0, 0), (0, 0), (lo, 2 * GRID_W - lo - n_dc)))
    skew = jnp.tile(ext, (1, 1, GRID_W))[:, :, :GRID_W * (2 * GRID_W - 1)]
    skew = skew.reshape(h, n_dr, GRID_W, 2 * GRID_W - 1)[:, :, :, GRID_W - 1:]
    skew = jnp.where(in_win[None, None], skew, -jnp.inf)
    tab = jnp.stack([skew[:, NA_KR_MAX - 1 - s:NA_KR_MAX - 1 - s + kr] for s in range(kr)], axis=1)
    return tab.transpose(0, 1, 3, 2, 4).reshape(h, kr, GRID_W, kr * GRID_W)


NA_ROW_GROUP = 8


def _rms(x, gain):
    return x * lax.rsqrt(jnp.mean(x * x, axis=-1, keepdims=True) + EPS) * gain


def _dot_nt(a, b):
    return lax.dot_general(a, b, (((1,), (1,)), ((), ())), preferred_element_type=F32)


def _natten_kernel(*refs, n_rows, with_ctx_out):
    if with_ctx_out:
        (q_ref, k_ref, v_ref, kc_ref, vc_ref, gq_ref, gk_ref, bias_ref, qc_ref,
         o_ref, oc_ref, qs, ks, vs) = refs
    else:
        q_ref, k_ref, v_ref, kc_ref, vc_ref, gq_ref, gk_ref, bias_ref, o_ref, qs, ks, vs = refs
    dh = q_ref.shape[-1]
    kr = min(NA_KR_MAX, n_rows)
    scale = dh ** -0.5
    qs[...] = (_rms(q_ref[...], gq_ref[...]) * scale).astype(BF16)
    ks[...] = _rms(k_ref[...], gk_ref[...]).astype(BF16)
    vs[...] = v_ref[...].astype(BF16)
    kc = _rms(kc_ref[...], gk_ref[...]).astype(BF16)
    vc = vc_ref[...].astype(BF16)

    grp = NA_ROW_GROUP
    nk = kr * GRID_W

    def rows_group(gi, carry):
        r0 = gi * grp
        q0 = pl.multiple_of(r0 * GRID_W, grp * GRID_W)
        ws = [jnp.clip(r0 + t - kr // 2, 0, n_rows - kr) for t in range(grp)]
        k0 = [pl.multiple_of(w * GRID_W, GRID_W) for w in ws]
        s_loc = [_dot_nt(qs[pl.ds(q0 + t * GRID_W, GRID_W), :], ks[pl.ds(k0[t], nk), :]) for t in range(grp)]
        s_ctx = _dot_nt(qs[pl.ds(q0, grp * GRID_W), :], kc)
        p_loc, p_ctx, den = [], [], []
        for t in range(grp):
            sl = s_loc[t] + bias_ref[r0 + t - ws[t]]
            sc = s_ctx[t * GRID_W:(t + 1) * GRID_W]
            m = jnp.maximum(jnp.max(sl, axis=-1, keepdims=True), jnp.max(sc, axis=-1, keepdims=True))
            pl_t = jnp.exp(sl - m)
            pc_t = jnp.exp(sc - m)
            den.append(jnp.sum(pl_t, axis=-1, keepdims=True) + jnp.sum(pc_t, axis=-1, keepdims=True))
            p_loc.append(pl_t.astype(BF16))
            p_ctx.append(pc_t.astype(BF16))
        o_loc = [jnp.dot(p_loc[t], vs[pl.ds(k0[t], nk), :], preferred_element_type=F32) for t in range(grp)]
        o_ctx = jnp.dot(jnp.concatenate(p_ctx, axis=0), vc, preferred_element_type=F32)
        for t in range(grp):
            o = (o_loc[t] + o_ctx[t * GRID_W:(t + 1) * GRID_W]) / den[t]
            o_ref[pl.ds(q0 + t * GRID_W, GRID_W), :] = o.astype(o_ref.dtype)
        return carry

    lax.fori_loop(0, n_rows // grp, rows_group, 0)

    if with_ctx_out:
        qc = (_rms(qc_ref[...], gq_ref[...]) * scale).astype(BF16)
        s = _dot_nt(qc, kc)
        p = jnp.exp(s - jnp.max(s, axis=-1, keepdims=True))
        o = jnp.dot(p.astype(BF16), vc, preferred_element_type=F32) / jnp.sum(p, axis=-1, keepdims=True)
        oc_ref[...] = o.astype(oc_ref.dtype)


def natten(px, pc, col_q, gq, gk, bias, with_ctx_out):
    bsz, l, _ = px.shape
    lc = pc.shape[1]
    h, kr, _, nk = bias.shape
    dh = gq.shape[-1]
    cb = col_q // dh
    assert (l // GRID_W) % NA_ROW_GROUP == 0

    def head_spec(rows, which):
        return pl.BlockSpec((None, rows, dh), lambda b, hh: (b, 0, cb + which * h + hh))

    in_specs = [head_spec(l, 0), head_spec(l, 1), head_spec(l, 2), head_spec(lc, 1), head_spec(lc, 2),
                pl.BlockSpec((1, dh), lambda b, hh: (0, 0)), pl.BlockSpec((1, dh), lambda b, hh: (0, 0)),
                pl.BlockSpec((None, kr, GRID_W, nk), lambda b, hh: (hh, 0, 0, 0))]
    args = [px, px, px, pc, pc, gq, gk, bias]
    out_specs = [pl.BlockSpec((None, l, dh), lambda b, hh: (b, 0, hh))]
    out_shape = [jax.ShapeDtypeStruct((bsz, l, h * dh), BF16)]
    if with_ctx_out:
        in_specs.append(head_spec(lc, 0))
        args.append(pc)
        out_specs.append(pl.BlockSpec((None, lc, dh), lambda b, hh: (b, 0, hh)))
        out_shape.append(jax.ShapeDtypeStruct((bsz, lc, h * dh), BF16))
    outs = pl.pallas_call(
        functools.partial(_natten_kernel, n_rows=l // GRID_W, with_ctx_out=with_ctx_out),
        grid=(bsz, h),
        in_specs=in_specs, out_specs=out_specs, out_shape=out_shape,
        scratch_shapes=[pltpu.VMEM((l, dh), BF16)] * 3,
        compiler_params=_cparams(("arbitrary", "arbitrary")),
        name="natten",
    )(*args)
    return outs if with_ctx_out else (outs[0], None)


def _cos_sin(n, period):
    k = np.arange(n, dtype=np.int64)
    ang = (2.0 * np.pi / period) * ((k[:, None] * k[None, :]) % period)
    return np.cos(ang), np.sin(ang)


def _fourier_kernel(u_ref, w_ref, cd_ref, sd_ref, cs_ref, o_ref, wc_ref, ws_ref, v_ref, *, scale):
    l, c = u_ref.shape
    dg = cd_ref.shape[0]

    @pl.when(pl.program_id(0) == 0)
    def _():
        wc_ref[...] = jnp.zeros_like(wc_ref)
        ws_ref[...] = jnp.zeros_like(ws_ref)
        for g in range(c // dg):
            sl = slice(g * dg, (g + 1) * dg)
            wg = w_ref[g]
            wc_ref[sl, sl] = jnp.dot(cd_ref[...], wg, precision=lax.Precision.HIGHEST,
                                     preferred_element_type=F32).astype(BF16)
            ws_ref[sl, sl] = jnp.dot(sd_ref[...], wg, precision=lax.Precision.HIGHEST,
                                     preferred_element_type=F32).astype(BF16)

    u = u_ref[...].astype(BF16)
    v_ref[0:l, :] = jnp.dot(u, wc_ref[...], preferred_element_type=F32).astype(BF16)
    v_ref[l:2 * l, :] = jnp.dot(u, ws_ref[...], preferred_element_type=F32).astype(BF16)
    y = jnp.dot(cs_ref[...], v_ref[...], preferred_element_type=F32)
    o_ref[...] = (y * scale).astype(o_ref.dtype)


def fourier_mix(p, col, w):
    bsz, l, _ = p.shape
    g, dg, _ = w.shape
    c = g * dg
    cl, sl = _cos_sin(l, l)
    cd, sd = _cos_sin(dg, dg)
    cs = jnp.asarray(np.concatenate([cl, -sl], axis=1), BF16)
    return pl.pallas_call(
        functools.partial(_fourier_kernel, scale=float((l * dg) ** -0.5)),
        grid=(bsz,),
        in_specs=[pl.BlockSpec((None, l, c), lambda b: (b, 0, col // c)),
                  pl.BlockSpec((g, dg, dg), lambda b: (0, 0, 0)),
                  pl.BlockSpec((dg, dg), lambda b: (0, 0)),
                  pl.BlockSpec((dg, dg), lambda b: (0, 0)),
                  pl.BlockSpec((l, 2 * l), lambda b: (0, 0), pipeline_mode=pl.Buffered(1))],
        out_specs=pl.BlockSpec((None, l, c), lambda b: (b, 0, 0)),
        out_shape=jax.ShapeDtypeStruct((bsz, l, c), BF16),
        scratch_shapes=[pltpu.VMEM((c, c), BF16), pltpu.VMEM((c, c), BF16), pltpu.VMEM((2 * l, c), BF16)],
        compiler_params=_cparams(("arbitrary",)),
        name="fourier_mix",
    )(p, w, jnp.asarray(cd, F32), jnp.asarray(sd, F32), cs)


HY_CBLK = 256
HY_FBLK = 512


def _hyena_dft_matrix(l):
    k = np.arange(l, dtype=np.int64)
    ang = (np.pi / l) * ((k[:, None] * k[None, :]) % (2 * l))
    sn = np.sin(ang)
    sn[0, :] = 1.0 - 2.0 * (k % 2)
    return jnp.asarray(np.concatenate([np.cos(ang), sn], axis=0), BF16)


def _hyena_pos_features(l):
    t = np.linspace(0.0, 1.0, l)[:, None]
    w = (2.0 * np.pi / l) * np.arange(l)[:, None]
    f = np.linspace(1e-4, HY_BANDS - 1, HY_BANDS)[None, :]
    z = np.concatenate([t, np.cos(f * w), -np.sin(f * w)], axis=-1)
    return np.pad(z, ((0, 0), (0, LANES - z.shape[1])))


def _split_bf16(x):
    hi = x.astype(BF16)
    return hi, (x - hi.astype(F32)).astype(BF16)


def _hyena_filter_kernel(z_ref, w1_ref, b1_ref, w2_ref, b2_ref, fr_ref, w3f_ref, w3b_ref, dl_ref, m_ref,
                         p_ref, q_ref):
    l = z_ref.shape[0]
    hp = lax.Precision.HIGHEST
    z = z_ref[...]
    h = jnp.sin(fr_ref[0:1, :] * (jnp.dot(z, w1_ref[...], precision=hp, preferred_element_type=F32)
                                  + b1_ref[...]))
    h = jnp.sin(fr_ref[1:2, :] * (jnp.dot(h, w2_ref[...], precision=hp, preferred_element_type=F32)
                                  + b2_ref[...]))
    window = jnp.exp(-z[:, 0:1] * dl_ref[...]) + HY_DECAY_SHIFT
    hf = jnp.dot(h, w3f_ref[...], precision=hp, preferred_element_type=F32) * window
    hb = jnp.dot(h, w3b_ref[...], precision=hp, preferred_element_type=F32) * window
    norm = jnp.sum(jnp.abs(hf) + jnp.abs(hb), axis=0, keepdims=True) + EPS
    hf = hf / norm
    hb = hb / norm
    g1h, g1l = _split_bf16(hf + hb)
    g2h, g2l = _split_bf16(hb - hf)
    f1 = (jnp.dot(m_ref[...], g1h, preferred_element_type=F32)
          + jnp.dot(m_ref[...], g1l, preferred_element_type=F32))
    f2 = (jnp.dot(m_ref[l:2 * l, :], g2h, preferred_element_type=F32)
          + jnp.dot(m_ref[l:2 * l, :], g2l, preferred_element_type=F32))
    p_ref[...] = f1[0:l]
    row = lax.broadcasted_iota(jnp.int32, f2.shape, 0)
    q_ref[...] = jnp.where(row == 0, f1[l:l + 1], f2)


def hyena_filter_spectrum(l, w1, b1, w2, b2, w3, freq, m):
    c = w3.shape[1] // 2
    hid = w1.shape[1]
    z = jnp.asarray(_hyena_pos_features(l), F32)
    w1p = jnp.pad(w1, ((0, z.shape[1] - w1.shape[0]), (0, 0)))
    deltas = np.abs(np.linspace(math.log(HY_DECAY_TARGET) / HY_SLOW_PCT,
                                math.log(HY_DECAY_TARGET) / HY_FAST_PCT, c))[None, :]
    nb = c // HY_CBLK
    full = lambda shape: pl.BlockSpec(shape, lambda j: (0,) * len(shape))
    return pl.pallas_call(
        _hyena_filter_kernel,
        grid=(nb,),
        in_specs=[full(z.shape), full(w1p.shape), full((1, hid)), full(w2.shape), full((1, hid)),
                  full((2, hid)),
                  pl.BlockSpec((hid, HY_CBLK), lambda j: (0, j)),
                  pl.BlockSpec((hid, HY_CBLK), lambda j: (0, nb + j)),
                  pl.BlockSpec((1, HY_CBLK), lambda j: (0, j)),
                  pl.BlockSpec(m.shape, lambda j: (0, 0), pipeline_mode=pl.Buffered(1))],
        out_specs=[pl.BlockSpec((l, HY_CBLK), lambda j: (0, j))] * 2,
        out_shape=[jax.ShapeDtypeStruct((l, c), F32)] * 2,
        compiler_params=_cparams(("arbitrary",)),
        name="hyena_filter",
    )(z, w1p, b1.reshape(1, hid), w2, b2.reshape(1, hid), freq, w3, w3, jnp.asarray(deltas, F32), m)


def _shift_rows(u, down):
    l = u.shape[0]
    row = lax.broadcasted_iota(jnp.int32, u.shape, 0)
    if down:
        return jnp.where(row == 0, 0.0, pltpu.roll(u, 1, 0))
    return jnp.where(row == l - 1, 0.0, pltpu.roll(u, l - 1, 0))


def _dwconv3(u, taps):
    return (taps[3:4] + _shift_rows(u, True) * taps[0:1] + u * taps[1:2] + _shift_rows(u, False) * taps[2:3])


def _hyena_conv_kernel(x0_ref, x1_ref, v_ref, taps_ref, bias_ref, p_ref, q_ref, m_ref, o_ref):
    l, c = x0_ref.shape
    x1 = _dwconv3(x1_ref[...], taps_ref[1])
    s = _dwconv3(v_ref[...], taps_ref[2]) * x1
    sb = s.astype(BF16)
    fb = min(HY_FBLK, l)
    nblk = l // fb
    inv_l = 1.0 / l

    def forward(i):
        return (jnp.dot(m_ref[i * fb:(i + 1) * fb, :], sb, preferred_element_type=F32),
                jnp.dot(m_ref[l + i * fb:l + (i + 1) * fb, :], sb, preferred_element_type=F32))

    y_cos_acc = y_sin_acc = nyquist = None
    ab = forward(0)
    for i in range(nblk):
        a, b = ab
        if i + 1 < nblk:
            ab = forward(i + 1)
        p, q = p_ref[i * fb:(i + 1) * fb, :], q_ref[i * fb:(i + 1) * fb, :]
        bq = b * q
        if i == 0:
            first = lax.broadcasted_iota(jnp.int32, (fb, c), 0) == 0
            nyquist = bq[0:1] * (0.5 * inv_l)
            y_cos = (a * p + jnp.where(first, 0.0, bq)) * jnp.where(first, 0.5 * inv_l, inv_l)
            y_sin = jnp.where(first, 0.0, (b * p - a * q) * inv_l)
        else:
            y_cos = (a * p + bq) * inv_l
            y_sin = (b * p - a * q) * inv_l
        d_cos = jnp.dot(m_ref[0:l, i * fb:(i + 1) * fb], y_cos.astype(BF16), preferred_element_type=F32)
        d_sin = jnp.dot(m_ref[l:2 * l, i * fb:(i + 1) * fb], y_sin.astype(BF16), preferred_element_type=F32)
        y_cos_acc = d_cos if y_cos_acc is None else y_cos_acc + d_cos
        y_sin_acc = d_sin if y_sin_acc is None else y_sin_acc + d_sin
    row = lax.broadcasted_iota(jnp.int32, (l, c), 0)
    y = (y_cos_acc + jnp.where(row == 0, 0.0, y_sin_acc)
         + jnp.where(row % 2 == 0, 1.0, -1.0) * nyquist)
    x0 = _dwconv3(x0_ref[...], taps_ref[0])
    o_ref[...] = ((y + s * bias_ref[...]) * x0).astype(o_ref.dtype)


def hyena_conv(p, col, conv_w, conv_b, bias, spec_p, spec_q, m):
    bsz, l, _ = p.shape
    c = bias.shape[-1]
    nb = c // HY_CBLK
    cb0 = col // HY_CBLK
    taps = jnp.concatenate([conv_w, conv_b[None]], axis=0)
    taps = taps.reshape(4, 3, nb, HY_CBLK).transpose(2, 1, 0, 3)

    def part(k):
        return pl.BlockSpec((None, l, HY_CBLK), lambda j, b: (b, 0, cb0 + k * nb + j))

    return pl.pallas_call(
        _hyena_conv_kernel,
        grid=(nb, bsz),
        in_specs=[part(0), part(1), part(2),
                  pl.BlockSpec((None, 3, 4, HY_CBLK), lambda j, b: (j, 0, 0, 0)),
                  pl.BlockSpec((1, HY_CBLK), lambda j, b: (0, j)),
                  pl.BlockSpec((l, HY_CBLK), lambda j, b: (0, j)),
                  pl.BlockSpec((l, HY_CBLK), lambda j, b: (0, j)),
                  pl.BlockSpec(m.shape, lambda j, b: (0, 0), pipeline_mode=pl.Buffered(1))],
        out_specs=pl.BlockSpec((None, l, HY_CBLK), lambda j, b: (b, 0, j)),
        out_shape=jax.ShapeDtypeStruct((bsz, l, c), BF16),
        compiler_params=_cparams(("arbitrary", "arbitrary")),
        name="hyena_conv",
    )(p, p, p, taps, bias.reshape(1, c), spec_p, spec_q, m)


GLA_HP = 2
GLA_ROPE_PAIR = 16
GLA_SCAN_UNROLL = 4


def _gla_rope_tables(l, dk):
    half = dk // 2
    nf = half // 2
    assert nf == GLA_ROPE_PAIR
    inv = ROPE_THETA ** (-np.arange(nf, dtype=np.float64) / nf)
    t = np.arange(l)
    ang_r = (t // GRID_W)[:, None] * inv
    ang_c = (t % GRID_W)[:, None] * inv
    cos = np.concatenate([np.cos(ang_r)] * 2 + [np.cos(ang_c)] * 2, axis=1)
    sin = np.concatenate([-np.sin(ang_r), np.sin(ang_r), -np.sin(ang_c), np.sin(ang_c)], axis=1)
    return (jnp.asarray(np.tile(cos, (1, GLA_HP)), F32), jnp.asarray(np.tile(sin, (1, GLA_HP)), F32))


def _rope(x, cos, sin):
    lane = lax.broadcasted_iota(jnp.int32, x.shape, 1)
    lanes = x.shape[1]
    partner = jnp.where(lane % (2 * GLA_ROPE_PAIR) < GLA_ROPE_PAIR,
                        pltpu.roll(x, lanes - GLA_ROPE_PAIR, 1), pltpu.roll(x, GLA_ROPE_PAIR, 1))
    return x * cos + partner * sin


def _log_sigmoid(x):
    return jnp.minimum(x, 0.0) - jnp.log(1.0 + jnp.exp(-jnp.abs(x)))


def _gla_kernel(*refs, with_ctx_out):
    (q_ref, k_ref, v_ref, r_ref, z_ref, cq_ref, ck_ref, cv_ref, cr_ref, cz_ref,
     wz_ref, bz_ref, g_ref, cos_ref, sin_ref) = refs[:15]
    if with_ctx_out:
        o_ref, oc_ref = refs[15:17]
        scratch = refs[17:]
    else:
        o_ref, oc_ref = refs[15], None
        scratch = refs[16:]
    qs, ks, las, ofs, obs, cqs, cks, clas, cofs, cobs, st_f, st_b = scratch
    dk2 = q_ref.shape[1]
    dv2 = v_ref.shape[1]
    dk, dv = dk2 // GLA_HP, dv2 // GLA_HP
    ch = GLA_CHUNK
    hp = lax.Precision.HIGHEST

    def gates(z):
        pre = jnp.dot(z, wz_ref[...], precision=hp, preferred_element_type=F32) + bz_ref[...]
        return _log_sigmoid(pre) * (1.0 / GLA_TAU)

    qs[...] = _rope(q_ref[...] * dk ** -0.5, cos_ref[...], sin_ref[...])
    ks[...] = _rope(k_ref[...], cos_ref[...], sin_ref[...])
    las[...] = gates(z_ref[...])
    cqs[...] = cq_ref[...] * dk ** -0.5
    cks[...] = ck_ref[...]
    clas[...] = gates(cz_ref[...])
    st_f[...] = jnp.zeros_like(st_f)
    st_b[...] = jnp.zeros_like(st_b)

    ri = lax.broadcasted_iota(jnp.int32, (ch, ch), 0)
    ci = lax.broadcasted_iota(jnp.int32, (ch, ch), 1)
    tri = {False: ri >= ci, True: ri <= ci}
    tri_b16 = {d: jnp.where(m, 1.0, 0.0).astype(BF16) for d, m in tri.items()}
    tri2 = {d: jnp.concatenate([m] * GLA_HP, axis=0) for d, m in tri.items()}
    lane_head = lax.broadcasted_iota(jnp.int32, (ch, dk2), 1) // dk
    st_r = lax.broadcasted_iota(jnp.int32, (dv2, dk2), 0) // dv
    st_c = lax.broadcasted_iota(jnp.int32, (dv2, dk2), 1) // dk
    st_diag = st_r == st_c

    def scan(q_s, k_s, la_s, v_in, of_s, ob_s, n):
        unroll = GLA_SCAN_UNROLL

        def body(trip, carry):
            items = []
            for u in range(unroll):
                c = trip * unroll + u
                items.append((False, pl.ds(pl.multiple_of(c * ch, ch), ch)))
                items.append((True, pl.ds(pl.multiple_of((n - 1 - c) * ch, ch), ch)))
            v_c = [v_in[rows, :].astype(BF16) for _, rows in items]
            cum = []
            for bw, rows in items:
                la_hi, la_lo = _split_bf16(la_s[rows, dk2:2 * dk2] if bw else la_s[rows, 0:dk2])
                cum.append(jnp.dot(tri_b16[bw], la_hi, preferred_element_type=F32)
                           + jnp.dot(tri_b16[bw], la_lo, preferred_element_type=F32))
            q_dec, k_end, decay, sc = [], [], [], []
            for i, (bw, rows) in enumerate(items):
                q_c, k_c = q_s[rows, :], k_s[rows, :]
                tot = cum[i][0:1] if bw else cum[i][ch - 1:ch]
                qd = q_c * jnp.exp(cum[i])
                k_inv = (k_c * jnp.exp(-cum[i])).astype(BF16)
                k_end.append((k_c * jnp.exp(tot - cum[i])).astype(BF16))
                decay.append(jnp.exp(tot))
                q_heads = jnp.concatenate([jnp.where(lane_head == h, qd, 0.0) for h in range(GLA_HP)], axis=0)
                sc.append(_dot_nt(q_heads.astype(BF16), k_inv))
                q_dec.append(qd.astype(BF16))
            o_intra, ds_t = [], []
            for i, (bw, rows) in enumerate(items):
                pv = jnp.dot(jnp.where(tri2[bw], sc[i], 0.0).astype(BF16), v_c[i],
                             preferred_element_type=F32)
                o_intra.append(jnp.concatenate(
                    [pv[h * ch:(h + 1) * ch, h * dv:(h + 1) * dv] for h in range(GLA_HP)], axis=1))
                ds_t.append(lax.dot_general(v_c[i], k_end[i], (((0,), (0,)), ((), ())),
                                            preferred_element_type=F32))
            for i, (bw, rows) in enumerate(items):
                st, o_s = (st_b, ob_s) if bw else (st_f, of_s)
                s_t = st[...]
                o_s[rows, :] = o_intra[i] + _dot_nt(q_dec[i], s_t.astype(BF16))
                st[...] = s_t * decay[i] + jnp.where(st_diag, ds_t[i], 0.0)
            return carry

        lax.fori_loop(0, n // unroll, body, 0)

    def finish(of_s, ob_s, gate_ref, out_ref, n_blocks, blk):
        def body(i, carry):
            rows = pl.ds(pl.multiple_of(i * blk, blk), blk)
            o = of_s[rows, :] + ob_s[rows, :]
            gate = gate_ref[rows, :]
            parts = []
            for h in range(GLA_HP):
                oh = o[:, h * dv:(h + 1) * dv]
                gh = gate[:, h * dv:(h + 1) * dv]
                parts.append(_rms(oh, g_ref[...]) * (gh * jax.nn.sigmoid(gh)))
            out_ref[rows, :] = jnp.concatenate(parts, axis=1).astype(out_ref.dtype)
            return carry
        lax.fori_loop(0, n_blocks, body, 0)

    l, lc = q_ref.shape[0], cq_ref.shape[0]
    scan(cqs, cks, clas, cv_ref, cofs, cobs, lc // ch)
    if with_ctx_out:
        finish(cofs, cobs, cr_ref, oc_ref, 1, lc)
    scan(qs, ks, las, v_ref, ofs, obs, l // ch)
    finish(ofs, obs, r_ref, o_ref, l // lc, lc)


def gla(px, pxz, pc, pcz, w_gate, b_gate, out_gain, with_ctx_out):
    bsz, l, _ = px.shape
    lc = pc.shape[1]
    dv = out_gain.shape[-1]
    hdk = w_gate.shape[-1]
    dk = hdk // GLA_HEADS
    dk2, dv2 = GLA_HP * dk, GLA_HP * dv
    nhp = GLA_HEADS // GLA_HP
    zw = pxz.shape[-1]
    rank = w_gate.shape[1]
    wz = jnp.zeros((nhp, zw, 2 * dk2), F32)
    for u in range(2):
        blk = w_gate[u].reshape(rank, nhp, dk2).transpose(1, 0, 2)
        wz = wz.at[:, u * rank:(u + 1) * rank, u * dk2:(u + 1) * dk2].set(blk)
    bz = b_gate.reshape(2, nhp, dk2).transpose(1, 0, 2).reshape(nhp, 1, 2 * dk2)
    cos, sin = _gla_rope_tables(l, dk)
    k_cb, v_cb, r_cb = hdk // dk2, 2 * hdk // dv2, (2 * hdk + GLA_HEADS * dv) // dv2

    def col(rows, width, cb):
        return pl.BlockSpec((None, rows, width), lambda b, j: (b, 0, cb + j))

    def whole(rows, width):
        return pl.BlockSpec((None, rows, width), lambda b, j: (b, 0, 0))

    const = lambda shape: pl.BlockSpec(shape, lambda b, j: (0,) * len(shape))
    in_specs = [col(l, dk2, 0), col(l, dk2, k_cb), col(l, dv2, v_cb), col(l, dv2, r_cb), whole(l, zw),
                col(lc, dk2, 0), col(lc, dk2, k_cb), col(lc, dv2, v_cb), col(lc, dv2, r_cb), whole(lc, zw),
                pl.BlockSpec((None, zw, 2 * dk2), lambda b, j: (j, 0, 0)),
                pl.BlockSpec((None, 1, 2 * dk2), lambda b, j: (j, 0, 0)),
                const((1, dv)), const((l, dk2)), const((l, dk2))]
    out_specs = [pl.BlockSpec((None, l, dv2), lambda b, j: (b, 0, j))]
    out_shape = [jax.ShapeDtypeStruct((bsz, l, GLA_HEADS * dv), BF16)]
    if with_ctx_out:
        out_specs.append(pl.BlockSpec((None, lc, dv2), lambda b, j: (b, 0, j)))
        out_shape.append(jax.ShapeDtypeStruct((bsz, lc, GLA_HEADS * dv), BF16))
    scratch = [pltpu.VMEM((l, dk2), F32), pltpu.VMEM((l, dk2), F32), pltpu.VMEM((l, 2 * dk2), F32),
               pltpu.VMEM((l, dv2), F32), pltpu.VMEM((l, dv2), F32),
               pltpu.VMEM((lc, dk2), F32), pltpu.VMEM((lc, dk2), F32), pltpu.VMEM((lc, 2 * dk2), F32),
               pltpu.VMEM((lc, dv2), F32), pltpu.VMEM((lc, dv2), F32),
               pltpu.VMEM((dv2, dk2), F32), pltpu.VMEM((dv2, dk2), F32)]
    outs = pl.pallas_call(
        functools.partial(_gla_kernel, with_ctx_out=with_ctx_out),
        grid=(bsz, nhp),
        in_specs=in_specs, out_specs=out_specs, out_shape=out_shape, scratch_shapes=scratch,
        compiler_params=_cparams(("arbitrary", "arbitrary")),
        name="gla",
    )(px, px, px, px, pxz, pc, pc, pc, pc, pcz, wz, bz, out_gain.reshape(1, dv), cos, sin)
    return outs if with_ctx_out else (outs[0], None)


def kernel(x, c, ctx, c_ctx, w_mod, b_mod, g_mix, w_in, gla_gate_w, gla_gate_b, gla_out_g,
           hy_conv_w, hy_conv_b, hy_w1, hy_b1, hy_w2, hy_b2, hy_w3, hy_freq, hy_bias, fn_w,
           na_q_g, na_k_g, na_rpb, w_out, g_ffn, ffn_w_up, ffn_conv_w, ffn_conv_b, ffn_w_down):
    bsz, l_lat, d = x.shape
    l_ctx = ctx.shape[1]
    depth = w_mod.shape[0]
    w = d // 4

    n_cond = -(-(bsz + 1) // SUBLANES) * SUBLANES
    cond = jnp.zeros((n_cond, d), F32).at[:bsz].set(c).at[bsz].set(c_ctx)
    mods = mod_vectors(cond, w_mod, b_mod).reshape(depth, n_cond, N_MOD, d)
    mods = jnp.pad(mods, ((0, 0), (0, 0), (0, MOD_ROWS - N_MOD), (0, 0)))

    z0 = 3 * w
    zw = 2 * GLA_GATE_RANK

    m_lat = _hyena_dft_matrix(l_lat)
    m_ctx = _hyena_dft_matrix(l_ctx)
    ctx_flat = ctx.reshape(1, bsz * l_ctx, d)
    for layer in range(depth):
        last = layer == depth - 1
        mod_x = mods[layer, :bsz]
        mod_c = mods[layer, bsz:bsz + 1]
        g_mix_l = g_mix[layer].reshape(1, d)
        g_ffn_l = g_ffn[layer].reshape(1, d)
        w_in_l = w_in[layer]
        w_main = jnp.concatenate([w_in_l[:, :z0], w_in_l[:, z0 + zw:]], axis=-1).astype(BF16)
        w_z = jnp.pad(w_in_l[:, z0:z0 + zw], ((0, 0), (0, LANES - zw))).astype(BF16)
        w_out_l = w_out[layer].astype(BF16)
        w_up_l = ffn_w_up[layer].astype(BF16)
        w_down_l = ffn_w_down[layer].astype(BF16)

        px, pxz = in_proj(x, mod_x, g_mix_l, w_main, w_z, tm=1024)
        pc, pcz = in_proj(ctx_flat, mod_c, g_mix_l, w_main, w_z, tm=1024)
        pc = pc.reshape(bsz, l_ctx, -1)
        pcz = pcz.reshape(bsz, l_ctx, -1)

        y_a, yc_a = gla(px, pxz, pc, pcz, gla_gate_w[layer], gla_gate_b[layer], gla_out_g[layer],
                        with_ctx_out=not last)
        hy_filt = (hy_w1[layer], hy_b1[layer], hy_w2[layer], hy_b2[layer], hy_w3[layer], hy_freq[layer])
        hy_p, hy_q = hyena_filter_spectrum(l_lat, *hy_filt, m_lat)
        y_b = hyena_conv(px, 3 * w, hy_conv_w[layer], hy_conv_b[layer], hy_bias[layer], hy_p, hy_q, m_lat)
        y_c = fourier_mix(px, 6 * w, fn_w[layer])
        na_bias = natten_bias_table(na_rpb[layer], l_lat // GRID_W)
        y_d, yc_d = natten(px, pc, 7 * w, na_q_g[layer].reshape(1, -1), na_k_g[layer].reshape(1, -1),
                           na_bias, with_ctx_out=not last)
        x = out_proj([y_a, y_b, y_c, y_d], w_out_l, x, mod_x, tm=512)
        cw = ffn_conv_w[layer]
        cb = ffn_conv_b[layer].reshape(1, -1)
        x = conv_ffn(x, mod_x, g_ffn_l, w_up_l, cw, cb, w_down_l, tm=1024)

        if not last:
            hc_p, hc_q = hyena_filter_spectrum(l_ctx, *hy_filt, m_ctx)
            yc_b = hyena_conv(pc, 3 * w, hy_conv_w[layer], hy_conv_b[layer], hy_bias[layer], hc_p, hc_q, m_ctx)
            yc_c = fourier_mix(pc, 6 * w, fn_w[layer])
            ycs = [y.reshape(1, bsz * l_ctx, w) for y in (yc_a, yc_b, yc_c, yc_d)]
            ctx_flat = out_proj(ycs, w_out_l, ctx_flat, mod_c, tm=512)
            ctx_flat = conv_ffn(ctx_flat, mod_c, g_ffn_l, w_up_l, cw, cb, w_down_l, tm=1024, seg_len=l_ctx)
    return x
```

```python
import functools
import math

import numpy as np
import jax
import jax.numpy as jnp
from jax import lax
from jax.experimental import pallas as pl
from jax.experimental.pallas import tpu as pltpu

F32 = jnp.float32
BF16 = jnp.bfloat16

GRID_W = 64
GLA_HEADS = 4
GLA_GATE_RANK = 16
GLA_TAU = 16.0
GLA_CHUNK = 64
HY_BANDS = 16
HY_DECAY_TARGET = 1e-2
HY_FAST_PCT = 0.3
HY_SLOW_PCT = 1.5
HY_DECAY_SHIFT = 0.05
FN_GROUPS = 4
NA_HEADS = 4
NA_KR_MAX = 8
NA_KC = 16
ROPE_THETA = 10000.0
N_MOD = 6
EPS = 1e-6

V7X_VMEM_LIMIT = 58 * 1024 * 1024
SUBLANES = 8
LANES = 128
MOD_ROWS = 8
FFN_HALO = 16


def _cparams(sem):
    return pltpu.CompilerParams(dimension_semantics=sem, vmem_limit_bytes=V7X_VMEM_LIMIT)


def _mod_kernel(s_ref, w_ref, b_ref, o_ref):
    s = s_ref[...]
    s = s * jax.nn.sigmoid(s)
    o_ref[...] = jnp.dot(s.astype(BF16), w_ref[...].astype(BF16),
                         preferred_element_type=F32) + b_ref[...]


def mod_vectors(cond, w_mod, b_mod, tn=1024):
    depth, d, n = w_mod.shape
    r = cond.shape[0]
    return pl.pallas_call(
        _mod_kernel,
        grid=(depth, n // tn),
        in_specs=[pl.BlockSpec((r, d), lambda l, j: (0, 0)),
                  pl.BlockSpec((None, d, tn), lambda l, j: (l, 0, j)),
                  pl.BlockSpec((None, 1, tn), lambda l, j: (l, 0, j))],
        out_specs=pl.BlockSpec((None, r, tn), lambda l, j: (l, 0, j)),
        out_shape=jax.ShapeDtypeStruct((depth, r, n), F32),
        compiler_params=_cparams(("arbitrary", "arbitrary")),
        name="mod_vectors",
    )(cond, w_mod, b_mod.reshape(depth, 1, n))


def _norm_mod(x, gain, shift, scale):
    ms = jnp.mean(x * x, axis=-1, keepdims=True)
    return (x * lax.rsqrt(ms + EPS) * gain) * (1.0 + scale) + shift


def _inproj_kernel(x_ref, mod_ref, g_ref, w_ref, wz_ref, o_ref, oz_ref, h_ref):
    j = pl.program_id(2)
    tn = o_ref.shape[1]

    @pl.when(j == 0)
    def _():
        h = _norm_mod(x_ref[...], g_ref[...], mod_ref[0:1, :], mod_ref[1:2, :]).astype(BF16)
        h_ref[...] = h
        oz_ref[...] = jnp.dot(h, wz_ref[...], preferred_element_type=F32)

    w = w_ref[:, pl.ds(pl.multiple_of(j * tn, tn), tn)]
    o_ref[...] = jnp.dot(h_ref[...], w, preferred_element_type=F32)


def in_proj(x, mod, gain, w_main, w_z, layer, tm, tn=1024):
    g, l, d = x.shape
    n = w_main.shape[-1]
    nz = w_z.shape[-1]
    return pl.pallas_call(
        _inproj_kernel,
        grid=(g, l // tm, n // tn),
        in_specs=[pl.BlockSpec((None, tm, d), lambda b, i, j: (b, i, 0)),
                  pl.BlockSpec((None, MOD_ROWS, d), lambda b, i, j: (b, 0, 0)),
                  pl.BlockSpec((1, d), lambda b, i, j: (0, 0)),
                  pl.BlockSpec((None, d, n), lambda b, i, j: (layer, 0, 0), pipeline_mode=pl.Buffered(1)),
                  pl.BlockSpec((None, d, nz), lambda b, i, j: (layer, 0, 0))],
        out_specs=[pl.BlockSpec((None, tm, tn), lambda b, i, j: (b, i, j)),
                   pl.BlockSpec((None, tm, nz), lambda b, i, j: (b, i, 0))],
        out_shape=[jax.ShapeDtypeStruct((g, l, n), F32),
                   jax.ShapeDtypeStruct((g, l, nz), F32)],
        scratch_shapes=[pltpu.VMEM((tm, d), BF16)],
        compiler_params=_cparams(("arbitrary", "arbitrary", "arbitrary")),
        name="in_proj",
    )(x, mod, gain, w_main, w_z)


def _outproj_kernel(ya_ref, yb_ref, yc_ref, yd_ref, w_ref, x_ref, mod_ref, o_ref):
    kw = ya_ref.shape[-1]
    acc = jnp.dot(ya_ref[...], w_ref[0 * kw:1 * kw, :], preferred_element_type=F32)
    acc += jnp.dot(yb_ref[...], w_ref[1 * kw:2 * kw, :], preferred_element_type=F32)
    acc += jnp.dot(yc_ref[...], w_ref[2 * kw:3 * kw, :], preferred_element_type=F32)
    acc += jnp.dot(yd_ref[...], w_ref[3 * kw:4 * kw, :], preferred_element_type=F32)
    o_ref[...] = x_ref[...] + mod_ref[2:3, :] * acc


def out_proj(ys, w_out, layer, x, mod, tm):
    g, l, d = x.shape
    kw = ys[0].shape[-1]
    yspec = pl.BlockSpec((None, tm, kw), lambda b, i: (b, i, 0))
    return pl.pallas_call(
        _outproj_kernel,
        grid=(g, l // tm),
        in_specs=[yspec, yspec, yspec, yspec,
                  pl.BlockSpec((None, 4 * kw, d), lambda b, i: (layer, 0, 0)),
                  pl.BlockSpec((None, tm, d), lambda b, i: (b, i, 0)),
                  pl.BlockSpec((None, MOD_ROWS, d), lambda b, i: (b, 0, 0))],
        out_specs=pl.BlockSpec((None, tm, d), lambda b, i: (b, i, 0)),
        out_shape=jax.ShapeDtypeStruct((g, l, d), F32),
        compiler_params=_cparams(("arbitrary", "arbitrary")),
        name="out_proj",
    )(*ys, w_out, x, mod)


def _ffn_kernel(x_ref, xp_ref, xn_ref, mod_ref, g_ref, wa_ref, wg_ref, cw_ref, cb_ref, wd_ref,
                o_ref, h_ref, act0_ref, act1_ref, *, seg_len, n_hidden_tiles):
    i = pl.program_id(1)
    j = pl.program_id(2)
    nf = n_hidden_tiles
    acts = (act0_ref, act1_ref)
    tm = x_ref.shape[0]
    tf = wa_ref.shape[1]
    hs = FFN_HALO

    def prologue():
        gain, shift, scale = g_ref[...], mod_ref[3:4, :], mod_ref[4:5, :]
        h_ref[hs:hs + tm, :] = _norm_mod(x_ref[...], gain, shift, scale).astype(BF16)
        hp = jnp.where(i > 0, _norm_mod(xp_ref[...], gain, shift, scale), 0.0)
        hn = jnp.where(i < pl.num_programs(1) - 1, _norm_mod(xn_ref[...], gain, shift, scale), 0.0)
        zero = jnp.zeros_like(hp)
        h_ref[0:hs, :] = jnp.concatenate([zero, hp], axis=0).astype(BF16)
        h_ref[hs + tm:2 * hs + tm, :] = jnp.concatenate([hn, zero], axis=0).astype(BF16)
        o_ref[...] = x_ref[...]

    def conv(u, c0, width):
        w = cw_ref[:, c0:c0 + width]
        prev, nxt = u[hs - 1:hs - 1 + tm], u[hs + 1:hs + 1 + tm]
        if seg_len is not None:
            pos = lax.broadcasted_iota(jnp.int32, prev.shape, 0) % seg_len
            prev = jnp.where(pos == 0, 0.0, prev)
            nxt = jnp.where(pos == seg_len - 1, 0.0, nxt)
        return cb_ref[:, c0:c0 + width] + prev * w[0:1] + u[hs:hs + tm] * w[1:2] + nxt * w[2:3]

    def up_matmuls():
        h = h_ref[...]
        return (jnp.dot(h, wa_ref[...], preferred_element_type=F32),
                jnp.dot(h, wg_ref[...], preferred_element_type=F32))

    def gate_to(act_ref, ua, ug):
        a = conv(ua, 0, tf)
        gt = conv(ug, tf, tf)
        act_ref[...] = (a * (gt * jax.nn.sigmoid(gt))).astype(BF16)

    def down_from(act_ref):
        o_ref[...] += mod_ref[5:6, :] * jnp.dot(act_ref[...], wd_ref[...], preferred_element_type=F32)

    middle = jnp.logical_and(j > 0, j < nf)

    @pl.when(j == 0)
    def _():
        prologue()
        ua, ug = up_matmuls()
        gate_to(acts[0], ua, ug)

    for parity in range(2):
        @pl.when(jnp.logical_and(middle, j % 2 == parity))
        def _():
            ua, ug = up_matmuls()
            down_from(acts[1 - parity])
            gate_to(acts[parity], ua, ug)

    @pl.when(j == nf)
    def _():
        down_from(acts[(n_hidden_tiles - 1) % 2])


def conv_ffn(x, mod, gain, w_up, conv_w, conv_b, w_down, layer, tm, tf=512, seg_len=None):
    g, l, d = x.shape
    assert seg_len is None or (tm % seg_len == 0 and l % tm == 0)
    f = w_down.shape[1]
    nf = f // tf
    nb = tm // SUBLANES
    last = l // SUBLANES - 1
    cw = conv_w.reshape(3, 2, nf, tf).transpose(2, 0, 1, 3).reshape(nf, 3, 2 * tf)
    cb = conv_b.reshape(1, 2, nf, tf).transpose(2, 0, 1, 3).reshape(nf, 1, 2 * tf)
    return pl.pallas_call(
        functools.partial(_ffn_kernel, seg_len=seg_len, n_hidden_tiles=nf),
        grid=(g, l // tm, nf + 1),
        in_specs=[pl.BlockSpec((None, tm, d), lambda b, i, j: (b, i, 0)),
                  pl.BlockSpec((None, SUBLANES, d), lambda b, i, j: (b, jnp.maximum(i * nb - 1, 0), 0)),
                  pl.BlockSpec((None, SUBLANES, d), lambda b, i, j: (b, jnp.minimum((i + 1) * nb, last), 0)),
                  pl.BlockSpec((None, MOD_ROWS, d), lambda b, i, j: (b, 0, 0)),
                  pl.BlockSpec((1, d), lambda b, i, j: (0, 0)),
                  pl.BlockSpec((None, d, tf), lambda b, i, j: (layer, 0, jnp.minimum(j, nf - 1))),
                  pl.BlockSpec((None, d, tf), lambda b, i, j: (layer, 0, nf + jnp.minimum(j, nf - 1))),
                  pl.BlockSpec((None, 3, 2 * tf), lambda b, i, j: (jnp.minimum(j, nf - 1), 0, 0)),
                  pl.BlockSpec((None, 1, 2 * tf), lambda b, i, j: (jnp.minimum(j, nf - 1), 0, 0)),
                  pl.BlockSpec((None, tf, d), lambda b, i, j: (layer, jnp.maximum(j - 1, 0), 0))],
        out_specs=pl.BlockSpec((None, tm, d), lambda b, i, j: (b, i, 0)),
        out_shape=jax.ShapeDtypeStruct((g, l, d), F32),
        scratch_shapes=[pltpu.VMEM((tm + 2 * FFN_HALO, d), BF16),
                        pltpu.VMEM((tm, tf), BF16), pltpu.VMEM((tm, tf), BF16)],
        compiler_params=_cparams(("arbitrary", "arbitrary", "arbitrary")),
        name="conv_ffn",
    )(x, x, x, mod, gain, w_up, w_up, cw, cb, w_down)


def natten_bias_table(rpb, n_rows):
    kr = min(NA_KR_MAX, n_rows)
    h, n_dr, n_dc = rpb.shape
    col = np.arange(GRID_W)
    col0 = np.clip(col - NA_KC // 2, 0, GRID_W - NA_KC)
    in_win = (col[None, :] >= col0[:, None]) & (col[None, :] < col0[:, None] + NA_KC)
    lo = GRID_W - NA_KC
    ext = jnp.pad(rpb.astype(F32), ((0, 0), (0, 0), (lo, 2 * GRID_W - lo - n_dc)))
    skew = jnp.tile(ext, (1, 1, GRID_W))[:, :, :GRID_W * (2 * GRID_W - 1)]
    skew = skew.reshape(h, n_dr, GRID_W, 2 * GRID_W - 1)[:, :, :, GRID_W - 1:]
    skew = jnp.where(in_win[None, None], skew, -jnp.inf)
    tab = jnp.stack([skew[:, NA_KR_MAX - 1 - s:NA_KR_MAX - 1 - s + kr] for s in range(kr)], axis=1)
    return tab.transpose(0, 1, 3, 2, 4).reshape(h, kr, GRID_W, kr * GRID_W)


NA_ROW_GROUP = 8


def _rms(x, gain):
    return x * lax.rsqrt(jnp.mean(x * x, axis=-1, keepdims=True) + EPS) * gain


def _dot_nt(a, b):
    return lax.dot_general(a, b, (((1,), (1,)), ((), ())), preferred_element_type=F32)


def _natten_kernel(*refs, n_rows, with_ctx_out):
    if with_ctx_out:
        (q_ref, k_ref, v_ref, kc_ref, vc_ref, gq_ref, gk_ref, bias_ref, qc_ref,
         o_ref, oc_ref, qs, ks, vs) = refs
    else:
        q_ref, k_ref, v_ref, kc_ref, vc_ref, gq_ref, gk_ref, bias_ref, o_ref, qs, ks, vs = refs
    dh = q_ref.shape[-1]
    kr = min(NA_KR_MAX, n_rows)
    scale = dh ** -0.5
    qs[...] = (_rms(q_ref[...], gq_ref[...]) * scale).astype(BF16)
    ks[...] = _rms(k_ref[...], gk_ref[...]).astype(BF16)
    vs[...] = v_ref[...].astype(BF16)
    kc = _rms(kc_ref[...], gk_ref[...]).astype(BF16)
    vc = vc_ref[...].astype(BF16)

    grp = NA_ROW_GROUP
    nk = kr * GRID_W

    def rows_group(gi, carry):
        r0 = gi * grp
        q0 = pl.multiple_of(r0 * GRID_W, grp * GRID_W)
        ws = [jnp.clip(r0 + t - kr // 2, 0, n_rows - kr) for t in range(grp)]
        k0 = [pl.multiple_of(w * GRID_W, GRID_W) for w in ws]
        s_loc = [_dot_nt(qs[pl.ds(q0 + t * GRID_W, GRID_W), :], ks[pl.ds(k0[t], nk), :]) for t in range(grp)]
        s_ctx = _dot_nt(qs[pl.ds(q0, grp * GRID_W), :], kc)
        p_loc, p_ctx, den = [], [], []
        for t in range(grp):
            sl = s_loc[t] + bias_ref[r0 + t - ws[t]]
            sc = s_ctx[t * GRID_W:(t + 1) * GRID_W]
            m = jnp.maximum(jnp.max(sl, axis=-1, keepdims=True), jnp.max(sc, axis=-1, keepdims=True))
            pl_t = jnp.exp(sl - m)
            pc_t = jnp.exp(sc - m)
            den.append(jnp.sum(pl_t, axis=-1, keepdims=True) + jnp.sum(pc_t, axis=-1, keepdims=True))
            p_loc.append(pl_t.astype(BF16))
            p_ctx.append(pc_t.astype(BF16))
        o_loc = [jnp.dot(p_loc[t], vs[pl.ds(k0[t], nk), :], preferred_element_type=F32) for t in range(grp)]
        o_ctx = jnp.dot(jnp.concatenate(p_ctx, axis=0), vc, preferred_element_type=F32)
        for t in range(grp):
            o = (o_loc[t] + o_ctx[t * GRID_W:(t + 1) * GRID_W]) / den[t]
            o_ref[pl.ds(q0 + t * GRID_W, GRID_W), :] = o.astype(o_ref.dtype)
        return carry

    lax.fori_loop(0, n_rows // grp, rows_group, 0)

    if with_ctx_out:
        qc = (_rms(qc_ref[...], gq_ref[...]) * scale).astype(BF16)
        s = _dot_nt(qc, kc)
        p = jnp.exp(s - jnp.max(s, axis=-1, keepdims=True))
        o = jnp.dot(p.astype(BF16), vc, preferred_element_type=F32) / jnp.sum(p, axis=-1, keepdims=True)
        oc_ref[...] = o.astype(oc_ref.dtype)


def natten(px, pc, col_q, gq, gk, bias, with_ctx_out):
    bsz, l, _ = px.shape
    lc = pc.shape[1]
    h, kr, _, nk = bias.shape
    dh = gq.shape[-1]
    cb = col_q // dh
    assert (l // GRID_W) % NA_ROW_GROUP == 0

    def head_spec(rows, which):
        return pl.BlockSpec((None, rows, dh), lambda b, hh: (b, 0, cb + which * h + hh))

    in_specs = [head_spec(l, 0), head_spec(l, 1), head_spec(l, 2), head_spec(lc, 1), head_spec(lc, 2),
                pl.BlockSpec((1, dh), lambda b, hh: (0, 0)), pl.BlockSpec((1, dh), lambda b, hh: (0, 0)),
                pl.BlockSpec((None, kr, GRID_W, nk), lambda b, hh: (hh, 0, 0, 0))]
    args = [px, px, px, pc, pc, gq, gk, bias]
    out_specs = [pl.BlockSpec((None, l, dh), lambda b, hh: (b, 0, hh))]
    out_shape = [jax.ShapeDtypeStruct((bsz, l, h * dh), BF16)]
    if with_ctx_out:
        in_specs.append(head_spec(lc, 0))
        args.append(pc)
        out_specs.append(pl.BlockSpec((None, lc, dh), lambda b, hh: (b, 0, hh)))
        out_shape.append(jax.ShapeDtypeStruct((bsz, lc, h * dh), BF16))
    outs = pl.pallas_call(
        functools.partial(_natten_kernel, n_rows=l // GRID_W, with_ctx_out=with_ctx_out),
        grid=(bsz, h),
        in_specs=in_specs, out_specs=out_specs, out_shape=out_shape,
        scratch_shapes=[pltpu.VMEM((l, dh), BF16)] * 3,
        compiler_params=_cparams(("arbitrary", "arbitrary")),
        name="natten",
    )(*args)
    return outs if with_ctx_out else (outs[0], None)


def _cos_sin(n, period):
    k = np.arange(n, dtype=np.int64)
    ang = (2.0 * np.pi / period) * ((k[:, None] * k[None, :]) % period)
    return np.cos(ang), np.sin(ang)


def _fourier_kernel(u_ref, w_ref, cd_ref, sd_ref, cs_ref, o_ref, wc_ref, ws_ref, v_ref, *, scale):
    l, c = u_ref.shape
    dg = cd_ref.shape[0]

    @pl.when(pl.program_id(0) == 0)
    def _():
        wc_ref[...] = jnp.zeros_like(wc_ref)
        ws_ref[...] = jnp.zeros_like(ws_ref)
        for g in range(c // dg):
            sl = slice(g * dg, (g + 1) * dg)
            wg = w_ref[g]
            wc_ref[sl, sl] = jnp.dot(cd_ref[...], wg, precision=lax.Precision.HIGHEST,
                                     preferred_element_type=F32).astype(BF16)
            ws_ref[sl, sl] = jnp.dot(sd_ref[...], wg, precision=lax.Precision.HIGHEST,
                                     preferred_element_type=F32).astype(BF16)

    u = u_ref[...].astype(BF16)
    v_ref[0:l, :] = jnp.dot(u, wc_ref[...], preferred_element_type=F32).astype(BF16)
    v_ref[l:2 * l, :] = jnp.dot(u, ws_ref[...], preferred_element_type=F32).astype(BF16)
    y = jnp.dot(cs_ref[...], v_ref[...], preferred_element_type=F32)
    o_ref[...] = (y * scale).astype(o_ref.dtype)


def fourier_mix(p, col, w):
    bsz, l, _ = p.shape
    g, dg, _ = w.shape
    c = g * dg
    cl, sl = _cos_sin(l, l)
    cd, sd = _cos_sin(dg, dg)
    cs = jnp.asarray(np.concatenate([cl, -sl], axis=1), BF16)
    return pl.pallas_call(
        functools.partial(_fourier_kernel, scale=float((l * dg) ** -0.5)),
        grid=(bsz,),
        in_specs=[pl.BlockSpec((None, l, c), lambda b: (b, 0, col // c)),
                  pl.BlockSpec((g, dg, dg), lambda b: (0, 0, 0)),
                  pl.BlockSpec((dg, dg), lambda b: (0, 0)),
                  pl.BlockSpec((dg, dg), lambda b: (0, 0)),
                  pl.BlockSpec((l, 2 * l), lambda b: (0, 0), pipeline_mode=pl.Buffered(1))],
        out_specs=pl.BlockSpec((None, l, c), lambda b: (b, 0, 0)),
        out_shape=jax.ShapeDtypeStruct((bsz, l, c), BF16),
        scratch_shapes=[pltpu.VMEM((c, c), BF16), pltpu.VMEM((c, c), BF16), pltpu.VMEM((2 * l, c), BF16)],
        compiler_params=_cparams(("arbitrary",)),
        name="fourier_mix",
    )(p, w, jnp.asarray(cd, F32), jnp.asarray(sd, F32), cs)


HY_CBLK = 256
HY_FBLK = 512


def _hyena_dft_matrix(l):
    k = np.arange(l, dtype=np.int64)
    ang = (np.pi / l) * ((k[:, None] * k[None, :]) % (2 * l))
    sn = np.sin(ang)
    sn[0, :] = 1.0 - 2.0 * (k % 2)
    return jnp.asarray(np.concatenate([np.cos(ang), sn], axis=0), BF16)


def _hyena_pos_features(l):
    t = np.linspace(0.0, 1.0, l)[:, None]
    w = (2.0 * np.pi / l) * np.arange(l)[:, None]
    f = np.linspace(1e-4, HY_BANDS - 1, HY_BANDS)[None, :]
    z = np.concatenate([t, np.cos(f * w), -np.sin(f * w)], axis=-1)
    return np.pad(z, ((0, 0), (0, LANES - z.shape[1])))


def _split_bf16(x):
    hi = x.astype(BF16)
    return hi, (x - hi.astype(F32)).astype(BF16)


def _hyena_filter_kernel(z_ref, w1_ref, b1_ref, w2_ref, b2_ref, fr_ref, w3f_ref, w3b_ref, dl_ref, m_ref,
                         p_ref, q_ref):
    l = z_ref.shape[0]
    hp = lax.Precision.HIGHEST
    z = z_ref[...]
    h = jnp.sin(fr_ref[0:1, :] * (jnp.dot(z, w1_ref[...], precision=hp, preferred_element_type=F32)
                                  + b1_ref[...]))
    h = jnp.sin(fr_ref[1:2, :] * (jnp.dot(h, w2_ref[...], precision=hp, preferred_element_type=F32)
                                  + b2_ref[...]))
    window = jnp.exp(-z[:, 0:1] * dl_ref[...]) + HY_DECAY_SHIFT
    hf = jnp.dot(h, w3f_ref[...], precision=hp, preferred_element_type=F32) * window
    hb = jnp.dot(h, w3b_ref[...], precision=hp, preferred_element_type=F32) * window
    norm = jnp.sum(jnp.abs(hf) + jnp.abs(hb), axis=0, keepdims=True) + EPS
    hf = hf / norm
    hb = hb / norm
    g1h, g1l = _split_bf16(hf + hb)
    g2h, g2l = _split_bf16(hb - hf)
    f1 = (jnp.dot(m_ref[...], g1h, preferred_element_type=F32)
          + jnp.dot(m_ref[...], g1l, preferred_element_type=F32))
    f2 = (jnp.dot(m_ref[l:2 * l, :], g2h, preferred_element_type=F32)
          + jnp.dot(m_ref[l:2 * l, :], g2l, preferred_element_type=F32))
    p_ref[...] = f1[0:l]
    row = lax.broadcasted_iota(jnp.int32, f2.shape, 0)
    q_ref[...] = jnp.where(row == 0, f1[l:l + 1], f2)


def hyena_filter_spectrum(l, w1, b1, w2, b2, w3, freq, m):
    c = w3.shape[1] // 2
    hid = w1.shape[1]
    z = jnp.asarray(_hyena_pos_features(l), F32)
    w1p = jnp.pad(w1, ((0, z.shape[1] - w1.shape[0]), (0, 0)))
    deltas = np.abs(np.linspace(math.log(HY_DECAY_TARGET) / HY_SLOW_PCT,
                                math.log(HY_DECAY_TARGET) / HY_FAST_PCT, c))[None, :]
    nb = c // HY_CBLK
    full = lambda shape: pl.BlockSpec(shape, lambda j: (0,) * len(shape))
    return pl.pallas_call(
        _hyena_filter_kernel,
        grid=(nb,),
        in_specs=[full(z.shape), full(w1p.shape), full((1, hid)), full(w2.shape), full((1, hid)),
                  full((2, hid)),
                  pl.BlockSpec((hid, HY_CBLK), lambda j: (0, j)),
                  pl.BlockSpec((hid, HY_CBLK), lambda j: (0, nb + j)),
                  pl.BlockSpec((1, HY_CBLK), lambda j: (0, j)),
                  pl.BlockSpec(m.shape, lambda j: (0, 0), pipeline_mode=pl.Buffered(1))],
        out_specs=[pl.BlockSpec((l, HY_CBLK), lambda j: (0, j))] * 2,
        out_shape=[jax.ShapeDtypeStruct((l, c), F32)] * 2,
        compiler_params=_cparams(("arbitrary",)),
        name="hyena_filter",
    )(z, w1p, b1.reshape(1, hid), w2, b2.reshape(1, hid), freq, w3, w3, jnp.asarray(deltas, F32), m)


def _shift_rows(u, down):
    l = u.shape[0]
    row = lax.broadcasted_iota(jnp.int32, u.shape, 0)
    if down:
        return jnp.where(row == 0, 0.0, pltpu.roll(u, 1, 0))
    return jnp.where(row == l - 1, 0.0, pltpu.roll(u, l - 1, 0))


def _dwconv3(u, taps):
    return (taps[3:4] + _shift_rows(u, True) * taps[0:1] + u * taps[1:2] + _shift_rows(u, False) * taps[2:3])


def _hyena_conv_kernel(x0_ref, x1_ref, v_ref, taps_ref, bias_ref, p_ref, q_ref, m_ref, o_ref):
    l, c = x0_ref.shape
    x1 = _dwconv3(x1_ref[...], taps_ref[1])
    s = _dwconv3(v_ref[...], taps_ref[2]) * x1
    sb = s.astype(BF16)
    fb = min(HY_FBLK, l)
    nblk = l // fb
    inv_l = 1.0 / l

    def forward(i):
        return (jnp.dot(m_ref[i * fb:(i + 1) * fb, :], sb, preferred_element_type=F32),
                jnp.dot(m_ref[l + i * fb:l + (i + 1) * fb, :], sb, preferred_element_type=F32))

    y_cos_acc = y_sin_acc = nyquist = None
    ab = forward(0)
    for i in range(nblk):
        a, b = ab
        if i + 1 < nblk:
            ab = forward(i + 1)
        p, q = p_ref[i * fb:(i + 1) * fb, :], q_ref[i * fb:(i + 1) * fb, :]
        bq = b * q
        if i == 0:
            first = lax.broadcasted_iota(jnp.int32, (fb, c), 0) == 0
            nyquist = bq[0:1] * (0.5 * inv_l)
            y_cos = (a * p + jnp.where(first, 0.0, bq)) * jnp.where(first, 0.5 * inv_l, inv_l)
            y_sin = jnp.where(first, 0.0, (b * p - a * q) * inv_l)
        else:
            y_cos = (a * p + bq) * inv_l
            y_sin = (b * p - a * q) * inv_l
        d_cos = jnp.dot(m_ref[0:l, i * fb:(i + 1) * fb], y_cos.astype(BF16), preferred_element_type=F32)
        d_sin = jnp.dot(m_ref[l:2 * l, i * fb:(i + 1) * fb], y_sin.astype(BF16), preferred_element_type=F32)
        y_cos_acc = d_cos if y_cos_acc is None else y_cos_acc + d_cos
        y_sin_acc = d_sin if y_sin_acc is None else y_sin_acc + d_sin
    row = lax.broadcasted_iota(jnp.int32, (l, c), 0)
    y = (y_cos_acc + jnp.where(row == 0, 0.0, y_sin_acc)
         + jnp.where(row % 2 == 0, 1.0, -1.0) * nyquist)
    x0 = _dwconv3(x0_ref[...], taps_ref[0])
    o_ref[...] = ((y + s * bias_ref[...]) * x0).astype(o_ref.dtype)


def hyena_conv(p, col, conv_w, conv_b, bias, spec_p, spec_q, m):
    bsz, l, _ = p.shape
    c = bias.shape[-1]
    nb = c // HY_CBLK
    cb0 = col // HY_CBLK
    taps = jnp.concatenate([conv_w, conv_b[None]], axis=0)
    taps = taps.reshape(4, 3, nb, HY_CBLK).transpose(2, 1, 0, 3)

    def part(k):
        return pl.BlockSpec((None, l, HY_CBLK), lambda j, b: (b, 0, cb0 + k * nb + j))

    return pl.pallas_call(
        _hyena_conv_kernel,
        grid=(nb, bsz),
        in_specs=[part(0), part(1), part(2),
                  pl.BlockSpec((None, 3, 4, HY_CBLK), lambda j, b: (j, 0, 0, 0)),
                  pl.BlockSpec((1, HY_CBLK), lambda j, b: (0, j)),
                  pl.BlockSpec((l, HY_CBLK), lambda j, b: (0, j)),
                  pl.BlockSpec((l, HY_CBLK), lambda j, b: (0, j)),
                  pl.BlockSpec(m.shape, lambda j, b: (0, 0), pipeline_mode=pl.Buffered(1))],
        out_specs=pl.BlockSpec((None, l, HY_CBLK), lambda j, b: (b, 0, j)),
        out_shape=jax.ShapeDtypeStruct((bsz, l, c), BF16),
        compiler_params=_cparams(("arbitrary", "arbitrary")),
        name="hyena_conv",
    )(p, p, p, taps, bias.reshape(1, c), spec_p, spec_q, m)


GLA_HP = 2
GLA_ROPE_PAIR = 16
GLA_SCAN_UNROLL = 4


def _gla_rope_tables(l, dk):
    half = dk // 2
    nf = half // 2
    assert nf == GLA_ROPE_PAIR
    inv = ROPE_THETA ** (-np.arange(nf, dtype=np.float64) / nf)
    t = np.arange(l)
    ang_r = (t // GRID_W)[:, None] * inv
    ang_c = (t % GRID_W)[:, None] * inv
    cos = np.concatenate([np.cos(ang_r)] * 2 + [np.cos(ang_c)] * 2, axis=1)
    sin = np.concatenate([-np.sin(ang_r), np.sin(ang_r), -np.sin(ang_c), np.sin(ang_c)], axis=1)
    return (jnp.asarray(np.tile(cos, (1, GLA_HP)), F32), jnp.asarray(np.tile(sin, (1, GLA_HP)), F32))


def _rope(x, cos, sin):
    lane = lax.broadcasted_iota(jnp.int32, x.shape, 1)
    lanes = x.shape[1]
    partner = jnp.where(lane % (2 * GLA_ROPE_PAIR) < GLA_ROPE_PAIR,
                        pltpu.roll(x, lanes - GLA_ROPE_PAIR, 1), pltpu.roll(x, GLA_ROPE_PAIR, 1))
    return x * cos + partner * sin


def _log_sigmoid(x):
    return jnp.minimum(x, 0.0) - jnp.log(1.0 + jnp.exp(-jnp.abs(x)))


def _gla_kernel(*refs, with_ctx_out):
    (q_ref, k_ref, v_ref, r_ref, z_ref, cq_ref, ck_ref, cv_ref, cr_ref, cz_ref,
     wz_ref, bz_ref, g_ref, cos_ref, sin_ref) = refs[:15]
    if with_ctx_out:
        o_ref, oc_ref = refs[15:17]
        scratch = refs[17:]
    else:
        o_ref, oc_ref = refs[15], None
        scratch = refs[16:]
    qs, ks, las, ofs, obs, cqs, cks, clas, cofs, cobs, st_f, st_b = scratch
    dk2 = q_ref.shape[1]
    dv2 = v_ref.shape[1]
    dk, dv = dk2 // GLA_HP, dv2 // GLA_HP
    ch = GLA_CHUNK
    hp = lax.Precision.HIGHEST

    def gates(z):
        pre = jnp.dot(z, wz_ref[...], precision=hp, preferred_element_type=F32) + bz_ref[...]
        return _log_sigmoid(pre) * (1.0 / GLA_TAU)

    qs[...] = _rope(q_ref[...] * dk ** -0.5, cos_ref[...], sin_ref[...])
    ks[...] = _rope(k_ref[...], cos_ref[...], sin_ref[...])
    las[...] = gates(z_ref[...])
    cqs[...] = cq_ref[...] * dk ** -0.5
    cks[...] = ck_ref[...]
    clas[...] = gates(cz_ref[...])
    st_f[...] = jnp.zeros_like(st_f)
    st_b[...] = jnp.zeros_like(st_b)

    ri = lax.broadcasted_iota(jnp.int32, (ch, ch), 0)
    ci = lax.broadcasted_iota(jnp.int32, (ch, ch), 1)
    tri = {False: ri >= ci, True: ri <= ci}
    tri_b16 = {d: jnp.where(m, 1.0, 0.0).astype(BF16) for d, m in tri.items()}
    tri2 = {d: jnp.concatenate([m] * GLA_HP, axis=0) for d, m in tri.items()}
    lane_head = lax.broadcasted_iota(jnp.int32, (ch, dk2), 1) // dk
    st_r = lax.broadcasted_iota(jnp.int32, (dv2, dk2), 0) // dv
    st_c = lax.broadcasted_iota(jnp.int32, (dv2, dk2), 1) // dk
    st_diag = st_r == st_c

    def scan(q_s, k_s, la_s, v_in, of_s, ob_s, n):
        unroll = GLA_SCAN_UNROLL

        def body(trip, carry):
            items = []
            for u in range(unroll):
                c = trip * unroll + u
                items.append((False, pl.ds(pl.multiple_of(c * ch, ch), ch)))
                items.append((True, pl.ds(pl.multiple_of((n - 1 - c) * ch, ch), ch)))
            v_c = [v_in[rows, :].astype(BF16) for _, rows in items]
            cum = []
            for bw, rows in items:
                la_hi, la_lo = _split_bf16(la_s[rows, dk2:2 * dk2] if bw else la_s[rows, 0:dk2])
                cum.append(jnp.dot(tri_b16[bw], la_hi, preferred_element_type=F32)
                           + jnp.dot(tri_b16[bw], la_lo, preferred_element_type=F32))
            q_dec, k_end, decay, sc = [], [], [], []
            for i, (bw, rows) in enumerate(items):
                q_c, k_c = q_s[rows, :], k_s[rows, :]
                tot = cum[i][0:1] if bw else cum[i][ch - 1:ch]
                qd = q_c * jnp.exp(cum[i])
                k_inv = (k_c * jnp.exp(-cum[i])).astype(BF16)
                k_end.append((k_c * jnp.exp(tot - cum[i])).astype(BF16))
                decay.append(jnp.exp(tot))
                q_heads = jnp.concatenate([jnp.where(lane_head == h, qd, 0.0) for h in range(GLA_HP)], axis=0)
                sc.append(_dot_nt(q_heads.astype(BF16), k_inv))
                q_dec.append(qd.astype(BF16))
            o_intra, ds_t = [], []
            for i, (bw, rows) in enumerate(items):
                pv = jnp.dot(jnp.where(tri2[bw], sc[i], 0.0).astype(BF16), v_c[i],
                             preferred_element_type=F32)
                o_intra.append(jnp.concatenate(
                    [pv[h * ch:(h + 1) * ch, h * dv:(h + 1) * dv] for h in range(GLA_HP)], axis=1))
                ds_t.append(lax.dot_general(v_c[i], k_end[i], (((0,), (0,)), ((), ())),
                                            preferred_element_type=F32))
            for i, (bw, rows) in enumerate(items):
                st, o_s = (st_b, ob_s) if bw else (st_f, of_s)
                s_t = st[...]
                o_s[rows, :] = o_intra[i] + _dot_nt(q_dec[i], s_t.astype(BF16))
                st[...] = s_t * decay[i] + jnp.where(st_diag, ds_t[i], 0.0)
            return carry

        lax.fori_loop(0, n // unroll, body, 0)

    def finish(of_s, ob_s, gate_ref, out_ref, n_blocks, blk):
        def body(i, carry):
            rows = pl.ds(pl.multiple_of(i * blk, blk), blk)
            o = of_s[rows, :] + ob_s[rows, :]
            gate = gate_ref[rows, :]
            parts = []
            for h in range(GLA_HP):
                oh = o[:, h * dv:(h + 1) * dv]
                gh = gate[:, h * dv:(h + 1) * dv]
                parts.append(_rms(oh, g_ref[...]) * (gh * jax.nn.sigmoid(gh)))
            out_ref[rows, :] = jnp.concatenate(parts, axis=1).astype(out_ref.dtype)
            return carry
        lax.fori_loop(0, n_blocks, body, 0)

    l, lc = q_ref.shape[0], cq_ref.shape[0]
    scan(cqs, cks, clas, cv_ref, cofs, cobs, lc // ch)
    if with_ctx_out:
        finish(cofs, cobs, cr_ref, oc_ref, 1, lc)
    scan(qs, ks, las, v_ref, ofs, obs, l // ch)
    finish(ofs, obs, r_ref, o_ref, l // lc, lc)


def gla(px, pxz, pc, pcz, w_gate, b_gate, out_gain, with_ctx_out):
    bsz, l, _ = px.shape
    lc = pc.shape[1]
    dv = out_gain.shape[-1]
    hdk = w_gate.shape[-1]
    dk = hdk // GLA_HEADS
    dk2, dv2 = GLA_HP * dk, GLA_HP * dv
    nhp = GLA_HEADS // GLA_HP
    zw = pxz.shape[-1]
    rank = w_gate.shape[1]
    wz = jnp.zeros((nhp, zw, 2 * dk2), F32)
    for u in range(2):
        blk = w_gate[u].reshape(rank, nhp, dk2).transpose(1, 0, 2)
        wz = wz.at[:, u * rank:(u + 1) * rank, u * dk2:(u + 1) * dk2].set(blk)
    bz = b_gate.reshape(2, nhp, dk2).transpose(1, 0, 2).reshape(nhp, 1, 2 * dk2)
    cos, sin = _gla_rope_tables(l, dk)
    k_cb, v_cb, r_cb = hdk // dk2, 2 * hdk // dv2, (2 * hdk + GLA_HEADS * dv) // dv2

    def col(rows, width, cb):
        return pl.BlockSpec((None, rows, width), lambda b, j: (b, 0, cb + j))

    def whole(rows, width):
        return pl.BlockSpec((None, rows, width), lambda b, j: (b, 0, 0))

    const = lambda shape: pl.BlockSpec(shape, lambda b, j: (0,) * len(shape))
    in_specs = [col(l, dk2, 0), col(l, dk2, k_cb), col(l, dv2, v_cb), col(l, dv2, r_cb), whole(l, zw),
                col(lc, dk2, 0), col(lc, dk2, k_cb), col(lc, dv2, v_cb), col(lc, dv2, r_cb), whole(lc, zw),
                pl.BlockSpec((None, zw, 2 * dk2), lambda b, j: (j, 0, 0)),
                pl.BlockSpec((None, 1, 2 * dk2), lambda b, j: (j, 0, 0)),
                const((1, dv)), const((l, dk2)), const((l, dk2))]
    out_specs = [pl.BlockSpec((None, l, dv2), lambda b, j: (b, 0, j))]
    out_shape = [jax.ShapeDtypeStruct((bsz, l, GLA_HEADS * dv), BF16)]
    if with_ctx_out:
        out_specs.append(pl.BlockSpec((None, lc, dv2), lambda b, j: (b, 0, j)))
        out_shape.append(jax.ShapeDtypeStruct((bsz, lc, GLA_HEADS * dv), BF16))
    scratch = [pltpu.VMEM((l, dk2), F32), pltpu.VMEM((l, dk2), F32), pltpu.VMEM((l, 2 * dk2), F32),
               pltpu.VMEM((l, dv2), F32), pltpu.VMEM((l, dv2), F32),
               pltpu.VMEM((lc, dk2), F32), pltpu.VMEM((lc, dk2), F32), pltpu.VMEM((lc, 2 * dk2), F32),
               pltpu.VMEM((lc, dv2), F32), pltpu.VMEM((lc, dv2), F32),
               pltpu.VMEM((dv2, dk2), F32), pltpu.VMEM((dv2, dk2), F32)]
    outs = pl.pallas_call(
        functools.partial(_gla_kernel, with_ctx_out=with_ctx_out),
        grid=(bsz, nhp),
        in_specs=in_specs, out_specs=out_specs, out_shape=out_shape, scratch_shapes=scratch,
        compiler_params=_cparams(("arbitrary", "arbitrary")),
        name="gla",
    )(px, px, px, px, pxz, pc, pc, pc, pc, pcz, wz, bz, out_gain.reshape(1, dv), cos, sin)
    return outs if with_ctx_out else (outs[0], None)


def kernel(x, c, ctx, c_ctx, w_mod, b_mod, g_mix, w_in, gla_gate_w, gla_gate_b, gla_out_g,
           hy_conv_w, hy_conv_b, hy_w1, hy_b1, hy_w2, hy_b2, hy_w3, hy_freq, hy_bias, fn_w,
           na_q_g, na_k_g, na_rpb, w_out, g_ffn, ffn_w_up, ffn_conv_w, ffn_conv_b, ffn_w_down):
    bsz, l_lat, d = x.shape
    l_ctx = ctx.shape[1]
    depth = w_mod.shape[0]
    w = d // 4

    n_cond = -(-(bsz + 1) // SUBLANES) * SUBLANES
    cond = jnp.zeros((n_cond, d), F32).at[:bsz].set(c).at[bsz].set(c_ctx)
    mods = mod_vectors(cond, w_mod, b_mod).reshape(depth, n_cond, N_MOD, d)
    mods = jnp.pad(mods, ((0, 0), (0, 0), (0, MOD_ROWS - N_MOD), (0, 0)))

    z0 = 3 * w
    zw = 2 * GLA_GATE_RANK
    w_main = jnp.concatenate([w_in[:, :, :z0], w_in[:, :, z0 + zw:]], axis=-1).astype(BF16)
    w_z = jnp.pad(w_in[:, :, z0:z0 + zw], ((0, 0), (0, 0), (0, LANES - zw))).astype(BF16)
    w_out_b = w_out.astype(BF16)
    w_up_b = ffn_w_up.astype(BF16)
    w_down_b = ffn_w_down.astype(BF16)

    m_lat = _hyena_dft_matrix(l_lat)
    m_ctx = _hyena_dft_matrix(l_ctx)
    ctx_flat = ctx.reshape(1, bsz * l_ctx, d)
    for layer in range(depth):
        last = layer == depth - 1
        mod_x = mods[layer, :bsz]
        mod_c = mods[layer, bsz:bsz + 1]
        g_mix_l = g_mix[layer].reshape(1, d)
        g_ffn_l = g_ffn[layer].reshape(1, d)

        px, pxz = in_proj(x, mod_x, g_mix_l, w_main, w_z, layer, tm=1024)
        pc, pcz = in_proj(ctx_flat, mod_c, g_mix_l, w_main, w_z, layer, tm=1024)
        pc = pc.reshape(bsz, l_ctx, -1)
        pcz = pcz.reshape(bsz, l_ctx, -1)

        y_a, yc_a = gla(px, pxz, pc, pcz, gla_gate_w[layer], gla_gate_b[layer], gla_out_g[layer],
                        with_ctx_out=not last)
        hy_filt = (hy_w1[layer], hy_b1[layer], hy_w2[layer], hy_b2[layer], hy_w3[layer], hy_freq[layer])
        hy_p, hy_q = hyena_filter_spectrum(l_lat, *hy_filt, m_lat)
        y_b = hyena_conv(px, 3 * w, hy_conv_w[layer], hy_conv_b[layer], hy_bias[layer], hy_p, hy_q, m_lat)
        y_c = fourier_mix(px, 6 * w, fn_w[layer])
        na_bias = natten_bias_table(na_rpb[layer], l_lat // GRID_W)
        y_d, yc_d = natten(px, pc, 7 * w, na_q_g[layer].reshape(1, -1), na_k_g[layer].reshape(1, -1),
                           na_bias, with_ctx_out=not last)
        x = out_proj([y_a, y_b, y_c, y_d], w_out_b, layer, x, mod_x, tm=512)
        cw = ffn_conv_w[layer]
        cb = ffn_conv_b[layer].reshape(1, -1)
        x = conv_ffn(x, mod_x, g_ffn_l, w_up_b, cw, cb, w_down_b, layer, tm=1024)

        if not last:
            hc_p, hc_q = hyena_filter_spectrum(l_ctx, *hy_filt, m_ctx)
            yc_b = hyena_conv(pc, 3 * w, hy_conv_w[layer], hy_conv_b[layer], hy_bias[layer], hc_p, hc_q, m_ctx)
            yc_c = fourier_mix(pc, 6 * w, fn_w[layer])
            ycs = [y.reshape(1, bsz * l_ctx, w) for y in (yc_a, yc_b, yc_c, yc_d)]
            ctx_flat = out_proj(ycs, w_out_b, layer, ctx_flat, mod_c, tm=512)
            ctx_flat = conv_ffn(ctx_flat, mod_c, g_ffn_l, w_up_b, cw, cb, w_down_b, layer, tm=1024,
                                seg_len=l_ctx)
    return x
```

```python
import functools
import math

import numpy as np
import jax
import jax.numpy as jnp
from jax import lax
from jax.experimental import pallas as pl
from jax.experimental.pallas import tpu as pltpu

F32 = jnp.float32
BF16 = jnp.bfloat16

GRID_W = 64
GLA_HEADS = 4
GLA_GATE_RANK = 16
GLA_TAU = 16.0
GLA_CHUNK = 64
HY_BANDS = 16
HY_DECAY_TARGET = 1e-2
HY_FAST_PCT = 0.3
HY_SLOW_PCT = 1.5
HY_DECAY_SHIFT = 0.05
FN_GROUPS = 4
NA_HEADS = 4
NA_KR_MAX = 8
NA_KC = 16
ROPE_THETA = 10000.0
N_MOD = 6
EPS = 1e-6

V7X_VMEM_LIMIT = 58 * 1024 * 1024
SUBLANES = 8
LANES = 128
MOD_ROWS = 8
FFN_HALO = 16


def _cparams(sem):
    return pltpu.CompilerParams(dimension_semantics=sem, vmem_limit_bytes=V7X_VMEM_LIMIT)


def _mod_kernel(s_ref, w_ref, b_ref, o_ref):
    s = s_ref[...]
    s = s * jax.nn.sigmoid(s)
    o_ref[...] = jnp.dot(s.astype(BF16), w_ref[...].astype(BF16),
                         preferred_element_type=F32) + b_ref[...]


def mod_vectors(cond, w_mod, b_mod, tn=1024):
    depth, d, n = w_mod.shape
    r = cond.shape[0]
    return pl.pallas_call(
        _mod_kernel,
        grid=(depth, n // tn),
        in_specs=[pl.BlockSpec((r, d), lambda l, j: (0, 0)),
                  pl.BlockSpec((None, d, tn), lambda l, j: (l, 0, j)),
                  pl.BlockSpec((None, 1, tn), lambda l, j: (l, 0, j))],
        out_specs=pl.BlockSpec((None, r, tn), lambda l, j: (l, 0, j)),
        out_shape=jax.ShapeDtypeStruct((depth, r, n), F32),
        compiler_params=_cparams(("arbitrary", "arbitrary")),
        name="mod_vectors",
    )(cond, w_mod, b_mod.reshape(depth, 1, n))


def _norm_mod(x, gain, shift, scale):
    ms = jnp.mean(x * x, axis=-1, keepdims=True)
    return (x * lax.rsqrt(ms + EPS) * gain) * (1.0 + scale) + shift


def _inproj_kernel(x_ref, mod_ref, g_ref, w_ref, wz_ref, o_ref, oz_ref, h_ref):
    j = pl.program_id(2)
    tn = o_ref.shape[1]

    @pl.when(j == 0)
    def _():
        h = _norm_mod(x_ref[...], g_ref[...], mod_ref[0:1, :], mod_ref[1:2, :]).astype(BF16)
        h_ref[...] = h
        oz_ref[...] = jnp.dot(h, wz_ref[...], preferred_element_type=F32)

    w = w_ref[:, pl.ds(pl.multiple_of(j * tn, tn), tn)]
    o_ref[...] = jnp.dot(h_ref[...], w, preferred_element_type=F32)


def in_proj(x, mod, gain, w_main, w_z, layer, tm, tn=1024):
    g, l, d = x.shape
    n = w_main.shape[-1]
    nz = w_z.shape[-1]
    return pl.pallas_call(
        _inproj_kernel,
        grid=(g, l // tm, n // tn),
        in_specs=[pl.BlockSpec((None, tm, d), lambda b, i, j: (b, i, 0)),
                  pl.BlockSpec((None, MOD_ROWS, d), lambda b, i, j: (b, 0, 0)),
                  pl.BlockSpec((1, d), lambda b, i, j: (0, 0)),
                  pl.BlockSpec((None, d, n), lambda b, i, j: (layer, 0, 0), pipeline_mode=pl.Buffered(1)),
                  pl.BlockSpec((None, d, nz), lambda b, i, j: (layer, 0, 0))],
        out_specs=[pl.BlockSpec((None, tm, tn), lambda b, i, j: (b, i, j)),
                   pl.BlockSpec((None, tm, nz), lambda b, i, j: (b, i, 0))],
        out_shape=[jax.ShapeDtypeStruct((g, l, n), F32),
                   jax.ShapeDtypeStruct((g, l, nz), F32)],
        scratch_shapes=[pltpu.VMEM((tm, d), BF16)],
        compiler_params=_cparams(("arbitrary", "arbitrary", "arbitrary")),
        name="in_proj",
    )(x, mod, gain, w_main, w_z)


def _outproj_kernel(ya_ref, yb_ref, yc_ref, yd_ref, w_ref, x_ref, mod_ref, o_ref):
    kw = ya_ref.shape[-1]
    acc = jnp.dot(ya_ref[...], w_ref[0 * kw:1 * kw, :], preferred_element_type=F32)
    acc += jnp.dot(yb_ref[...], w_ref[1 * kw:2 * kw, :], preferred_element_type=F32)
    acc += jnp.dot(yc_ref[...], w_ref[2 * kw:3 * kw, :], preferred_element_type=F32)
    acc += jnp.dot(yd_ref[...], w_ref[3 * kw:4 * kw, :], preferred_element_type=F32)
    o_ref[...] = x_ref[...] + mod_ref[2:3, :] * acc


def out_proj(ys, w_out, layer, x, mod, tm):
    g, l, d = x.shape
    kw = ys[0].shape[-1]
    yspec = pl.BlockSpec((None, tm, kw), lambda b, i: (b, i, 0))
    return pl.pallas_call(
        _outproj_kernel,
        grid=(g, l // tm),
        in_specs=[yspec, yspec, yspec, yspec,
                  pl.BlockSpec((None, 4 * kw, d), lambda b, i: (layer, 0, 0)),
                  pl.BlockSpec((None, tm, d), lambda b, i: (b, i, 0)),
                  pl.BlockSpec((None, MOD_ROWS, d), lambda b, i: (b, 0, 0))],
        out_specs=pl.BlockSpec((None, tm, d), lambda b, i: (b, i, 0)),
        out_shape=jax.ShapeDtypeStruct((g, l, d), F32),
        compiler_params=_cparams(("arbitrary", "arbitrary")),
        name="out_proj",
    )(*ys, w_out, x, mod)


def _ffn_kernel(x_ref, xp_ref, xn_ref, mod_ref, g_ref, wa_ref, wg_ref, cw_ref, cb_ref, wd_ref,
                o_ref, h_ref, act0_ref, act1_ref, *, seg_len, n_hidden_tiles):
    i = pl.program_id(1)
    j = pl.program_id(2)
    nf = n_hidden_tiles
    acts = (act0_ref, act1_ref)
    tm = x_ref.shape[0]
    tf = wa_ref.shape[1]
    hs = FFN_HALO

    def prologue():
        gain, shift, scale = g_ref[...], mod_ref[3:4, :], mod_ref[4:5, :]
        h_ref[hs:hs + tm, :] = _norm_mod(x_ref[...], gain, shift, scale).astype(BF16)
        hp = jnp.where(i > 0, _norm_mod(xp_ref[...], gain, shift, scale), 0.0)
        hn = jnp.where(i < pl.num_programs(1) - 1, _norm_mod(xn_ref[...], gain, shift, scale), 0.0)
        zero = jnp.zeros_like(hp)
        h_ref[0:hs, :] = jnp.concatenate([zero, hp], axis=0).astype(BF16)
        h_ref[hs + tm:2 * hs + tm, :] = jnp.concatenate([hn, zero], axis=0).astype(BF16)
        o_ref[...] = x_ref[...]

    def conv(u, c0, width):
        w = cw_ref[:, c0:c0 + width]
        prev, nxt = u[hs - 1:hs - 1 + tm], u[hs + 1:hs + 1 + tm]
        if seg_len is not None:
            pos = lax.broadcasted_iota(jnp.int32, prev.shape, 0) % seg_len
            prev = jnp.where(pos == 0, 0.0, prev)
            nxt = jnp.where(pos == seg_len - 1, 0.0, nxt)
        return cb_ref[:, c0:c0 + width] + prev * w[0:1] + u[hs:hs + tm] * w[1:2] + nxt * w[2:3]

    def up_matmuls():
        h = h_ref[...]
        return (jnp.dot(h, wa_ref[...], preferred_element_type=F32),
                jnp.dot(h, wg_ref[...], preferred_element_type=F32))

    def gate_to(act_ref, ua, ug):
        a = conv(ua, 0, tf)
        gt = conv(ug, tf, tf)
        act_ref[...] = (a * (gt * jax.nn.sigmoid(gt))).astype(BF16)

    def down_from(act_ref):
        o_ref[...] += mod_ref[5:6, :] * jnp.dot(act_ref[...], wd_ref[...], preferred_element_type=F32)

    middle = jnp.logical_and(j > 0, j < nf)

    @pl.when(j == 0)
    def _():
        prologue()
        ua, ug = up_matmuls()
        gate_to(acts[0], ua, ug)

    for parity in range(2):
        @pl.when(jnp.logical_and(middle, j % 2 == parity))
        def _():
            ua, ug = up_matmuls()
            down_from(acts[1 - parity])
            gate_to(acts[parity], ua, ug)

    @pl.when(j == nf)
    def _():
        down_from(acts[(n_hidden_tiles - 1) % 2])


FFN_TF = 512


def ffn_up_tiles(w_up, tf=FFN_TF):
    depth, d, f2 = w_up.shape
    nf = f2 // (2 * tf)
    return w_up.astype(BF16).reshape(depth, d, 2, nf, tf).transpose(0, 2, 3, 1, 4)


def conv_ffn(x, mod, gain, w_up, conv_w, conv_b, w_down, layer, tm, tf=FFN_TF, seg_len=None):
    g, l, d = x.shape
    assert seg_len is None or (tm % seg_len == 0 and l % tm == 0)
    f = w_down.shape[1]
    nf = f // tf
    nb = tm // SUBLANES
    last = l // SUBLANES - 1
    cw = conv_w.reshape(3, 2, nf, tf).transpose(2, 0, 1, 3).reshape(nf, 3, 2 * tf)
    cb = conv_b.reshape(1, 2, nf, tf).transpose(2, 0, 1, 3).reshape(nf, 1, 2 * tf)
    return pl.pallas_call(
        functools.partial(_ffn_kernel, seg_len=seg_len, n_hidden_tiles=nf),
        grid=(g, l // tm, nf + 1),
        in_specs=[pl.BlockSpec((None, tm, d), lambda b, i, j: (b, i, 0)),
                  pl.BlockSpec((None, SUBLANES, d), lambda b, i, j: (b, jnp.maximum(i * nb - 1, 0), 0)),
                  pl.BlockSpec((None, SUBLANES, d), lambda b, i, j: (b, jnp.minimum((i + 1) * nb, last), 0)),
                  pl.BlockSpec((None, MOD_ROWS, d), lambda b, i, j: (b, 0, 0)),
                  pl.BlockSpec((1, d), lambda b, i, j: (0, 0)),
                  pl.BlockSpec((None, None, None, d, tf), lambda b, i, j: (layer, 0, jnp.minimum(j, nf - 1), 0, 0)),
                  pl.BlockSpec((None, None, None, d, tf), lambda b, i, j: (layer, 1, jnp.minimum(j, nf - 1), 0, 0)),
                  pl.BlockSpec((None, 3, 2 * tf), lambda b, i, j: (jnp.minimum(j, nf - 1), 0, 0)),
                  pl.BlockSpec((None, 1, 2 * tf), lambda b, i, j: (jnp.minimum(j, nf - 1), 0, 0)),
                  pl.BlockSpec((None, tf, d), lambda b, i, j: (layer, jnp.maximum(j - 1, 0), 0))],
        out_specs=pl.BlockSpec((None, tm, d), lambda b, i, j: (b, i, 0)),
        out_shape=jax.ShapeDtypeStruct((g, l, d), F32),
        scratch_shapes=[pltpu.VMEM((tm + 2 * FFN_HALO, d), BF16),
                        pltpu.VMEM((tm, tf), BF16), pltpu.VMEM((tm, tf), BF16)],
        compiler_params=_cparams(("arbitrary", "arbitrary", "arbitrary")),
        name="conv_ffn",
    )(x, x, x, mod, gain, w_up, w_up, cw, cb, w_down)


def natten_bias_table(rpb, n_rows):
    kr = min(NA_KR_MAX, n_rows)
    h, n_dr, n_dc = rpb.shape
    col = np.arange(GRID_W)
    col0 = np.clip(col - NA_KC // 2, 0, GRID_W - NA_KC)
    in_win = (col[None, :] >= col0[:, None]) & (col[None, :] < col0[:, None] + NA_KC)
    lo = GRID_W - NA_KC
    ext = jnp.pad(rpb.astype(F32), ((0, 0), (0, 0), (lo, 2 * GRID_W - lo - n_dc)))
    skew = jnp.tile(ext, (1, 1, GRID_W))[:, :, :GRID_W * (2 * GRID_W - 1)]
    skew = skew.reshape(h, n_dr, GRID_W, 2 * GRID_W - 1)[:, :, :, GRID_W - 1:]
    skew = jnp.where(in_win[None, None], skew, -jnp.inf)
    tab = jnp.stack([skew[:, NA_KR_MAX - 1 - s:NA_KR_MAX - 1 - s + kr] for s in range(kr)], axis=1)
    return tab.transpose(0, 1, 3, 2, 4).reshape(h, kr, GRID_W, kr * GRID_W)


NA_ROW_GROUP = 8


def _rms(x, gain):
    return x * lax.rsqrt(jnp.mean(x * x, axis=-1, keepdims=True) + EPS) * gain


def _dot_nt(a, b):
    return lax.dot_general(a, b, (((1,), (1,)), ((), ())), preferred_element_type=F32)


def _natten_kernel(*refs, n_rows, with_ctx_out):
    if with_ctx_out:
        (q_ref, k_ref, v_ref, kc_ref, vc_ref, gq_ref, gk_ref, bias_ref, qc_ref,
         o_ref, oc_ref, qs, ks, vs) = refs
    else:
        q_ref, k_ref, v_ref, kc_ref, vc_ref, gq_ref, gk_ref, bias_ref, o_ref, qs, ks, vs = refs
    dh = q_ref.shape[-1]
    kr = min(NA_KR_MAX, n_rows)
    scale = dh ** -0.5
    qs[...] = (_rms(q_ref[...], gq_ref[...]) * scale).astype(BF16)
    ks[...] = _rms(k_ref[...], gk_ref[...]).astype(BF16)
    vs[...] = v_ref[...].astype(BF16)
    kc = _rms(kc_ref[...], gk_ref[...]).astype(BF16)
    vc = vc_ref[...].astype(BF16)

    grp = NA_ROW_GROUP
    nk = kr * GRID_W

    def rows_group(gi, carry):
        r0 = gi * grp
        q0 = pl.multiple_of(r0 * GRID_W, grp * GRID_W)
        ws = [jnp.clip(r0 + t - kr // 2, 0, n_rows - kr) for t in range(grp)]
        k0 = [pl.multiple_of(w * GRID_W, GRID_W) for w in ws]
        s_loc = [_dot_nt(qs[pl.ds(q0 + t * GRID_W, GRID_W), :], ks[pl.ds(k0[t], nk), :]) for t in range(grp)]
        s_ctx = _dot_nt(qs[pl.ds(q0, grp * GRID_W), :], kc)
        p_loc, p_ctx, den = [], [], []
        for t in range(grp):
            sl = s_loc[t] + bias_ref[r0 + t - ws[t]]
            sc = s_ctx[t * GRID_W:(t + 1) * GRID_W]
            m = jnp.maximum(jnp.max(sl, axis=-1, keepdims=True), jnp.max(sc, axis=-1, keepdims=True))
            pl_t = jnp.exp(sl - m)
            pc_t = jnp.exp(sc - m)
            den.append(jnp.sum(pl_t, axis=-1, keepdims=True) + jnp.sum(pc_t, axis=-1, keepdims=True))
            p_loc.append(pl_t.astype(BF16))
            p_ctx.append(pc_t.astype(BF16))
        o_loc = [jnp.dot(p_loc[t], vs[pl.ds(k0[t], nk), :], preferred_element_type=F32) for t in range(grp)]
        o_ctx = jnp.dot(jnp.concatenate(p_ctx, axis=0), vc, preferred_element_type=F32)
        for t in range(grp):
            o = (o_loc[t] + o_ctx[t * GRID_W:(t + 1) * GRID_W]) / den[t]
            o_ref[pl.ds(q0 + t * GRID_W, GRID_W), :] = o.astype(o_ref.dtype)
        return carry

    lax.fori_loop(0, n_rows // grp, rows_group, 0)

    if with_ctx_out:
        qc = (_rms(qc_ref[...], gq_ref[...]) * scale).astype(BF16)
        s = _dot_nt(qc, kc)
        p = jnp.exp(s - jnp.max(s, axis=-1, keepdims=True))
        o = jnp.dot(p.astype(BF16), vc, preferred_element_type=F32) / jnp.sum(p, axis=-1, keepdims=True)
        oc_ref[...] = o.astype(oc_ref.dtype)


def natten(px, pc, col_q, gq, gk, bias, with_ctx_out):
    bsz, l, _ = px.shape
    lc = pc.shape[1]
    h, kr, _, nk = bias.shape
    dh = gq.shape[-1]
    cb = col_q // dh
    assert (l // GRID_W) % NA_ROW_GROUP == 0

    def head_spec(rows, which):
        return pl.BlockSpec((None, rows, dh), lambda b, hh: (b, 0, cb + which * h + hh))

    in_specs = [head_spec(l, 0), head_spec(l, 1), head_spec(l, 2), head_spec(lc, 1), head_spec(lc, 2),
                pl.BlockSpec((1, dh), lambda b, hh: (0, 0)), pl.BlockSpec((1, dh), lambda b, hh: (0, 0)),
                pl.BlockSpec((None, kr, GRID_W, nk), lambda b, hh: (hh, 0, 0, 0))]
    args = [px, px, px, pc, pc, gq, gk, bias]
    out_specs = [pl.BlockSpec((None, l, dh), lambda b, hh: (b, 0, hh))]
    out_shape = [jax.ShapeDtypeStruct((bsz, l, h * dh), BF16)]
    if with_ctx_out:
        in_specs.append(head_spec(lc, 0))
        args.append(pc)
        out_specs.append(pl.BlockSpec((None, lc, dh), lambda b, hh: (b, 0, hh)))
        out_shape.append(jax.ShapeDtypeStruct((bsz, lc, h * dh), BF16))
    outs = pl.pallas_call(
        functools.partial(_natten_kernel, n_rows=l // GRID_W, with_ctx_out=with_ctx_out),
        grid=(bsz, h),
        in_specs=in_specs, out_specs=out_specs, out_shape=out_shape,
        scratch_shapes=[pltpu.VMEM((l, dh), BF16)] * 3,
        compiler_params=_cparams(("arbitrary", "arbitrary")),
        name="natten",
    )(*args)
    return outs if with_ctx_out else (outs[0], None)


def _cos_sin(n, period):
    k = np.arange(n, dtype=np.int64)
    ang = (2.0 * np.pi / period) * ((k[:, None] * k[None, :]) % period)
    return np.cos(ang), np.sin(ang)


def _fourier_kernel(u_ref, w_ref, cd_ref, sd_ref, cs_ref, o_ref, wc_ref, ws_ref, v_ref, *, scale):
    l, c = u_ref.shape
    dg = cd_ref.shape[0]

    @pl.when(pl.program_id(0) == 0)
    def _():
        wc_ref[...] = jnp.zeros_like(wc_ref)
        ws_ref[...] = jnp.zeros_like(ws_ref)
        for g in range(c // dg):
            sl = slice(g * dg, (g + 1) * dg)
            wg = w_ref[g]
            wc_ref[sl, sl] = jnp.dot(cd_ref[...], wg, precision=lax.Precision.HIGHEST,
                                     preferred_element_type=F32).astype(BF16)
            ws_ref[sl, sl] = jnp.dot(sd_ref[...], wg, precision=lax.Precision.HIGHEST,
                                     preferred_element_type=F32).astype(BF16)

    u = u_ref[...].astype(BF16)
    v_ref[0:l, :] = jnp.dot(u, wc_ref[...], preferred_element_type=F32).astype(BF16)
    v_ref[l:2 * l, :] = jnp.dot(u, ws_ref[...], preferred_element_type=F32).astype(BF16)
    y = jnp.dot(cs_ref[...], v_ref[...], preferred_element_type=F32)
    o_ref[...] = (y * scale).astype(o_ref.dtype)


def fourier_mix(p, col, w):
    bsz, l, _ = p.shape
    g, dg, _ = w.shape
    c = g * dg
    cl, sl = _cos_sin(l, l)
    cd, sd = _cos_sin(dg, dg)
    cs = jnp.asarray(np.concatenate([cl, -sl], axis=1), BF16)
    return pl.pallas_call(
        functools.partial(_fourier_kernel, scale=float((l * dg) ** -0.5)),
        grid=(bsz,),
        in_specs=[pl.BlockSpec((None, l, c), lambda b: (b, 0, col // c)),
                  pl.BlockSpec((g, dg, dg), lambda b: (0, 0, 0)),
                  pl.BlockSpec((dg, dg), lambda b: (0, 0)),
                  pl.BlockSpec((dg, dg), lambda b: (0, 0)),
                  pl.BlockSpec((l, 2 * l), lambda b: (0, 0), pipeline_mode=pl.Buffered(1))],
        out_specs=pl.BlockSpec((None, l, c), lambda b: (b, 0, 0)),
        out_shape=jax.ShapeDtypeStruct((bsz, l, c), BF16),
        scratch_shapes=[pltpu.VMEM((c, c), BF16), pltpu.VMEM((c, c), BF16), pltpu.VMEM((2 * l, c), BF16)],
        compiler_params=_cparams(("arbitrary",)),
        name="fourier_mix",
    )(p, w, jnp.asarray(cd, F32), jnp.asarray(sd, F32), cs)


HY_CBLK = 256
HY_FBLK = 512


def _hyena_dft_matrix(l):
    k = np.arange(l, dtype=np.int64)
    ang = (np.pi / l) * ((k[:, None] * k[None, :]) % (2 * l))
    sn = np.sin(ang)
    sn[0, :] = 1.0 - 2.0 * (k % 2)
    return jnp.asarray(np.concatenate([np.cos(ang), sn], axis=0), BF16)


def _hyena_pos_features(l):
    t = np.linspace(0.0, 1.0, l)[:, None]
    w = (2.0 * np.pi / l) * np.arange(l)[:, None]
    f = np.linspace(1e-4, HY_BANDS - 1, HY_BANDS)[None, :]
    z = np.concatenate([t, np.cos(f * w), -np.sin(f * w)], axis=-1)
    return np.pad(z, ((0, 0), (0, LANES - z.shape[1])))


def _split_bf16(x):
    hi = x.astype(BF16)
    return hi, (x - hi.astype(F32)).astype(BF16)


def _hyena_filter_kernel(z_ref, w1_ref, b1_ref, w2_ref, b2_ref, fr_ref, w3f_ref, w3b_ref, dl_ref, m_ref,
                         p_ref, q_ref):
    l = z_ref.shape[0]
    hp = lax.Precision.HIGHEST
    z = z_ref[...]
    h = jnp.sin(fr_ref[0:1, :] * (jnp.dot(z, w1_ref[...], precision=hp, preferred_element_type=F32)
                                  + b1_ref[...]))
    h = jnp.sin(fr_ref[1:2, :] * (jnp.dot(h, w2_ref[...], precision=hp, preferred_element_type=F32)
                                  + b2_ref[...]))
    window = jnp.exp(-z[:, 0:1] * dl_ref[...]) + HY_DECAY_SHIFT
    hf = jnp.dot(h, w3f_ref[...], precision=hp, preferred_element_type=F32) * window
    hb = jnp.dot(h, w3b_ref[...], precision=hp, preferred_element_type=F32) * window
    norm = jnp.sum(jnp.abs(hf) + jnp.abs(hb), axis=0, keepdims=True) + EPS
    hf = hf / norm
    hb = hb / norm
    g1h, g1l = _split_bf16(hf + hb)
    g2h, g2l = _split_bf16(hb - hf)
    f1 = (jnp.dot(m_ref[...], g1h, preferred_element_type=F32)
          + jnp.dot(m_ref[...], g1l, preferred_element_type=F32))
    f2 = (jnp.dot(m_ref[l:2 * l, :], g2h, preferred_element_type=F32)
          + jnp.dot(m_ref[l:2 * l, :], g2l, preferred_element_type=F32))
    p_ref[...] = f1[0:l]
    row = lax.broadcasted_iota(jnp.int32, f2.shape, 0)
    q_ref[...] = jnp.where(row == 0, f1[l:l + 1], f2)


def hyena_filter_spectrum(l, w1, b1, w2, b2, w3, freq, m):
    c = w3.shape[1] // 2
    hid = w1.shape[1]
    z = jnp.asarray(_hyena_pos_features(l), F32)
    w1p = jnp.pad(w1, ((0, z.shape[1] - w1.shape[0]), (0, 0)))
    deltas = np.abs(np.linspace(math.log(HY_DECAY_TARGET) / HY_SLOW_PCT,
                                math.log(HY_DECAY_TARGET) / HY_FAST_PCT, c))[None, :]
    nb = c // HY_CBLK
    full = lambda shape: pl.BlockSpec(shape, lambda j: (0,) * len(shape))
    return pl.pallas_call(
        _hyena_filter_kernel,
        grid=(nb,),
        in_specs=[full(z.shape), full(w1p.shape), full((1, hid)), full(w2.shape), full((1, hid)),
                  full((2, hid)),
                  pl.BlockSpec((hid, HY_CBLK), lambda j: (0, j)),
                  pl.BlockSpec((hid, HY_CBLK), lambda j: (0, nb + j)),
                  pl.BlockSpec((1, HY_CBLK), lambda j: (0, j)),
                  pl.BlockSpec(m.shape, lambda j: (0, 0), pipeline_mode=pl.Buffered(1))],
        out_specs=[pl.BlockSpec((l, HY_CBLK), lambda j: (0, j))] * 2,
        out_shape=[jax.ShapeDtypeStruct((l, c), F32)] * 2,
        compiler_params=_cparams(("arbitrary",)),
        name="hyena_filter",
    )(z, w1p, b1.reshape(1, hid), w2, b2.reshape(1, hid), freq, w3, w3, jnp.asarray(deltas, F32), m)


def _shift_rows(u, down):
    l = u.shape[0]
    row = lax.broadcasted_iota(jnp.int32, u.shape, 0)
    if down:
        return jnp.where(row == 0, 0.0, pltpu.roll(u, 1, 0))
    return jnp.where(row == l - 1, 0.0, pltpu.roll(u, l - 1, 0))


def _dwconv3(u, taps):
    return (taps[3:4] + _shift_rows(u, True) * taps[0:1] + u * taps[1:2] + _shift_rows(u, False) * taps[2:3])


def _hyena_conv_kernel(x0_ref, x1_ref, v_ref, taps_ref, bias_ref, p_ref, q_ref, m_ref, o_ref):
    l, c = x0_ref.shape
    x1 = _dwconv3(x1_ref[...], taps_ref[1])
    s = _dwconv3(v_ref[...], taps_ref[2]) * x1
    sb = s.astype(BF16)
    fb = min(HY_FBLK, l)
    nblk = l // fb
    inv_l = 1.0 / l

    def forward(i):
        return (jnp.dot(m_ref[i * fb:(i + 1) * fb, :], sb, preferred_element_type=F32),
                jnp.dot(m_ref[l + i * fb:l + (i + 1) * fb, :], sb, preferred_element_type=F32))

    y_cos_acc = y_sin_acc = nyquist = None
    ab = forward(0)
    for i in range(nblk):
        a, b = ab
        if i + 1 < nblk:
            ab = forward(i + 1)
        p, q = p_ref[i * fb:(i + 1) * fb, :], q_ref[i * fb:(i + 1) * fb, :]
        bq = b * q
        if i == 0:
            first = lax.broadcasted_iota(jnp.int32, (fb, c), 0) == 0
            nyquist = bq[0:1] * (0.5 * inv_l)
            y_cos = (a * p + jnp.where(first, 0.0, bq)) * jnp.where(first, 0.5 * inv_l, inv_l)
            y_sin = jnp.where(first, 0.0, (b * p - a * q) * inv_l)
        else:
            y_cos = (a * p + bq) * inv_l
            y_sin = (b * p - a * q) * inv_l
        d_cos = jnp.dot(m_ref[0:l, i * fb:(i + 1) * fb], y_cos.astype(BF16), preferred_element_type=F32)
        d_sin = jnp.dot(m_ref[l:2 * l, i * fb:(i + 1) * fb], y_sin.astype(BF16), preferred_element_type=F32)
        y_cos_acc = d_cos if y_cos_acc is None else y_cos_acc + d_cos
        y_sin_acc = d_sin if y_sin_acc is None else y_sin_acc + d_sin
    row = lax.broadcasted_iota(jnp.int32, (l, c), 0)
    y = (y_cos_acc + jnp.where(row == 0, 0.0, y_sin_acc)
         + jnp.where(row % 2 == 0, 1.0, -1.0) * nyquist)
    x0 = _dwconv3(x0_ref[...], taps_ref[0])
    o_ref[...] = ((y + s * bias_ref[...]) * x0).astype(o_ref.dtype)


def hyena_conv(p, col, conv_w, conv_b, bias, spec_p, spec_q, m):
    bsz, l, _ = p.shape
    c = bias.shape[-1]
    nb = c // HY_CBLK
    cb0 = col // HY_CBLK
    taps = jnp.concatenate([conv_w, conv_b[None]], axis=0)
    taps = taps.reshape(4, 3, nb, HY_CBLK).transpose(2, 1, 0, 3)

    def part(k):
        return pl.BlockSpec((None, l, HY_CBLK), lambda j, b: (b, 0, cb0 + k * nb + j))

    return pl.pallas_call(
        _hyena_conv_kernel,
        grid=(nb, bsz),
        in_specs=[part(0), part(1), part(2),
                  pl.BlockSpec((None, 3, 4, HY_CBLK), lambda j, b: (j, 0, 0, 0)),
                  pl.BlockSpec((1, HY_CBLK), lambda j, b: (0, j)),
                  pl.BlockSpec((l, HY_CBLK), lambda j, b: (0, j)),
                  pl.BlockSpec((l, HY_CBLK), lambda j, b: (0, j)),
                  pl.BlockSpec(m.shape, lambda j, b: (0, 0), pipeline_mode=pl.Buffered(1))],
        out_specs=pl.BlockSpec((None, l, HY_CBLK), lambda j, b: (b, 0, j)),
        out_shape=jax.ShapeDtypeStruct((bsz, l, c), BF16),
        compiler_params=_cparams(("arbitrary", "arbitrary")),
        name="hyena_conv",
    )(p, p, p, taps, bias.reshape(1, c), spec_p, spec_q, m)


GLA_HP = 2
GLA_ROPE_PAIR = 16
GLA_SCAN_UNROLL = 4


def _gla_rope_tables(l, dk):
    half = dk // 2
    nf = half // 2
    assert nf == GLA_ROPE_PAIR
    inv = ROPE_THETA ** (-np.arange(nf, dtype=np.float64) / nf)
    t = np.arange(l)
    ang_r = (t // GRID_W)[:, None] * inv
    ang_c = (t % GRID_W)[:, None] * inv
    cos = np.concatenate([np.cos(ang_r)] * 2 + [np.cos(ang_c)] * 2, axis=1)
    sin = np.concatenate([-np.sin(ang_r), np.sin(ang_r), -np.sin(ang_c), np.sin(ang_c)], axis=1)
    return (jnp.asarray(np.tile(cos, (1, GLA_HP)), F32), jnp.asarray(np.tile(sin, (1, GLA_HP)), F32))


def _rope(x, cos, sin):
    lane = lax.broadcasted_iota(jnp.int32, x.shape, 1)
    lanes = x.shape[1]
    partner = jnp.where(lane % (2 * GLA_ROPE_PAIR) < GLA_ROPE_PAIR,
                        pltpu.roll(x, lanes - GLA_ROPE_PAIR, 1), pltpu.roll(x, GLA_ROPE_PAIR, 1))
    return x * cos + partner * sin


def _log_sigmoid(x):
    return jnp.minimum(x, 0.0) - jnp.log(1.0 + jnp.exp(-jnp.abs(x)))


def _gla_kernel(*refs, with_ctx_out):
    (q_ref, k_ref, v_ref, r_ref, z_ref, cq_ref, ck_ref, cv_ref, cr_ref, cz_ref,
     wz_ref, bz_ref, g_ref, cos_ref, sin_ref) = refs[:15]
    if with_ctx_out:
        o_ref, oc_ref = refs[15:17]
        scratch = refs[17:]
    else:
        o_ref, oc_ref = refs[15], None
        scratch = refs[16:]
    qs, ks, las, ofs, obs, cqs, cks, clas, cofs, cobs, st_f, st_b = scratch
    dk2 = q_ref.shape[1]
    dv2 = v_ref.shape[1]
    dk, dv = dk2 // GLA_HP, dv2 // GLA_HP
    ch = GLA_CHUNK
    hp = lax.Precision.HIGHEST

    def gates(z):
        pre = jnp.dot(z, wz_ref[...], precision=hp, preferred_element_type=F32) + bz_ref[...]
        return _log_sigmoid(pre) * (1.0 / GLA_TAU)

    qs[...] = _rope(q_ref[...] * dk ** -0.5, cos_ref[...], sin_ref[...])
    ks[...] = _rope(k_ref[...], cos_ref[...], sin_ref[...])
    las[...] = gates(z_ref[...])
    cqs[...] = cq_ref[...] * dk ** -0.5
    cks[...] = ck_ref[...]
    clas[...] = gates(cz_ref[...])
    st_f[...] = jnp.zeros_like(st_f)
    st_b[...] = jnp.zeros_like(st_b)

    ri = lax.broadcasted_iota(jnp.int32, (ch, ch), 0)
    ci = lax.broadcasted_iota(jnp.int32, (ch, ch), 1)
    tri = {False: ri >= ci, True: ri <= ci}
    tri_b16 = {d: jnp.where(m, 1.0, 0.0).astype(BF16) for d, m in tri.items()}
    tri2 = {d: jnp.concatenate([m] * GLA_HP, axis=0) for d, m in tri.items()}
    lane_head = lax.broadcasted_iota(jnp.int32, (ch, dk2), 1) // dk
    st_r = lax.broadcasted_iota(jnp.int32, (dv2, dk2), 0) // dv
    st_c = lax.broadcasted_iota(jnp.int32, (dv2, dk2), 1) // dk
    st_diag = st_r == st_c

    def scan(q_s, k_s, la_s, v_in, of_s, ob_s, n):
        unroll = GLA_SCAN_UNROLL

        def body(trip, carry):
            items = []
            for u in range(unroll):
                c = trip * unroll + u
                items.append((False, pl.ds(pl.multiple_of(c * ch, ch), ch)))
                items.append((True, pl.ds(pl.multiple_of((n - 1 - c) * ch, ch), ch)))
            v_c = [v_in[rows, :].astype(BF16) for _, rows in items]
            cum = []
            for bw, rows in items:
                la_hi, la_lo = _split_bf16(la_s[rows, dk2:2 * dk2] if bw else la_s[rows, 0:dk2])
                cum.append(jnp.dot(tri_b16[bw], la_hi, preferred_element_type=F32)
                           + jnp.dot(tri_b16[bw], la_lo, preferred_element_type=F32))
            q_dec, k_end, decay, sc = [], [], [], []
            for i, (bw, rows) in enumerate(items):
                q_c, k_c = q_s[rows, :], k_s[rows, :]
                tot = cum[i][0:1] if bw else cum[i][ch - 1:ch]
                qd = q_c * jnp.exp(cum[i])
                k_inv = (k_c * jnp.exp(-cum[i])).astype(BF16)
                k_end.append((k_c * jnp.exp(tot - cum[i])).astype(BF16))
                decay.append(jnp.exp(tot))
                q_heads = jnp.concatenate([jnp.where(lane_head == h, qd, 0.0) for h in range(GLA_HP)], axis=0)
                sc.append(_dot_nt(q_heads.astype(BF16), k_inv))
                q_dec.append(qd.astype(BF16))
            o_intra, ds_t = [], []
            for i, (bw, rows) in enumerate(items):
                pv = jnp.dot(jnp.where(tri2[bw], sc[i], 0.0).astype(BF16), v_c[i],
                             preferred_element_type=F32)
                o_intra.append(jnp.concatenate(
                    [pv[h * ch:(h + 1) * ch, h * dv:(h + 1) * dv] for h in range(GLA_HP)], axis=1))
                ds_t.append(lax.dot_general(v_c[i], k_end[i], (((0,), (0,)), ((), ())),
                                            preferred_element_type=F32))
            for i, (bw, rows) in enumerate(items):
                st, o_s = (st_b, ob_s) if bw else (st_f, of_s)
                s_t = st[...]
                o_s[rows, :] = o_intra[i] + _dot_nt(q_dec[i], s_t.astype(BF16))
                st[...] = s_t * decay[i] + jnp.where(st_diag, ds_t[i], 0.0)
            return carry

        lax.fori_loop(0, n // unroll, body, 0)

    def finish(of_s, ob_s, gate_ref, out_ref, n_blocks, blk):
        def body(i, carry):
            rows = pl.ds(pl.multiple_of(i * blk, blk), blk)
            o = of_s[rows, :] + ob_s[rows, :]
            gate = gate_ref[rows, :]
            parts = []
            for h in range(GLA_HP):
                oh = o[:, h * dv:(h + 1) * dv]
                gh = gate[:, h * dv:(h + 1) * dv]
                parts.append(_rms(oh, g_ref[...]) * (gh * jax.nn.sigmoid(gh)))
            out_ref[rows, :] = jnp.concatenate(parts, axis=1).astype(out_ref.dtype)
            return carry
        lax.fori_loop(0, n_blocks, body, 0)

    l, lc = q_ref.shape[0], cq_ref.shape[0]
    scan(cqs, cks, clas, cv_ref, cofs, cobs, lc // ch)
    if with_ctx_out:
        finish(cofs, cobs, cr_ref, oc_ref, 1, lc)
    scan(qs, ks, las, v_ref, ofs, obs, l // ch)
    finish(ofs, obs, r_ref, o_ref, l // lc, lc)


def gla(px, pxz, pc, pcz, w_gate, b_gate, out_gain, with_ctx_out):
    bsz, l, _ = px.shape
    lc = pc.shape[1]
    dv = out_gain.shape[-1]
    hdk = w_gate.shape[-1]
    dk = hdk // GLA_HEADS
    dk2, dv2 = GLA_HP * dk, GLA_HP * dv
    nhp = GLA_HEADS // GLA_HP
    zw = pxz.shape[-1]
    rank = w_gate.shape[1]
    wz = jnp.zeros((nhp, zw, 2 * dk2), F32)
    for u in range(2):
        blk = w_gate[u].reshape(rank, nhp, dk2).transpose(1, 0, 2)
        wz = wz.at[:, u * rank:(u + 1) * rank, u * dk2:(u + 1) * dk2].set(blk)
    bz = b_gate.reshape(2, nhp, dk2).transpose(1, 0, 2).reshape(nhp, 1, 2 * dk2)
    cos, sin = _gla_rope_tables(l, dk)
    k_cb, v_cb, r_cb = hdk // dk2, 2 * hdk // dv2, (2 * hdk + GLA_HEADS * dv) // dv2

    def col(rows, width, cb):
        return pl.BlockSpec((None, rows, width), lambda b, j: (b, 0, cb + j))

    def whole(rows, width):
        return pl.BlockSpec((None, rows, width), lambda b, j: (b, 0, 0))

    const = lambda shape: pl.BlockSpec(shape, lambda b, j: (0,) * len(shape))
    in_specs = [col(l, dk2, 0), col(l, dk2, k_cb), col(l, dv2, v_cb), col(l, dv2, r_cb), whole(l, zw),
                col(lc, dk2, 0), col(lc, dk2, k_cb), col(lc, dv2, v_cb), col(lc, dv2, r_cb), whole(lc, zw),
                pl.BlockSpec((None, zw, 2 * dk2), lambda b, j: (j, 0, 0)),
                pl.BlockSpec((None, 1, 2 * dk2), lambda b, j: (j, 0, 0)),
                const((1, dv)), const((l, dk2)), const((l, dk2))]
    out_specs = [pl.BlockSpec((None, l, dv2), lambda b, j: (b, 0, j))]
    out_shape = [jax.ShapeDtypeStruct((bsz, l, GLA_HEADS * dv), BF16)]
    if with_ctx_out:
        out_specs.append(pl.BlockSpec((None, lc, dv2), lambda b, j: (b, 0, j)))
        out_shape.append(jax.ShapeDtypeStruct((bsz, lc, GLA_HEADS * dv), BF16))
    scratch = [pltpu.VMEM((l, dk2), F32), pltpu.VMEM((l, dk2), F32), pltpu.VMEM((l, 2 * dk2), F32),
               pltpu.VMEM((l, dv2), F32), pltpu.VMEM((l, dv2), F32),
               pltpu.VMEM((lc, dk2), F32), pltpu.VMEM((lc, dk2), F32), pltpu.VMEM((lc, 2 * dk2), F32),
               pltpu.VMEM((lc, dv2), F32), pltpu.VMEM((lc, dv2), F32),
               pltpu.VMEM((dv2, dk2), F32), pltpu.VMEM((dv2, dk2), F32)]
    outs = pl.pallas_call(
        functools.partial(_gla_kernel, with_ctx_out=with_ctx_out),
        grid=(bsz, nhp),
        in_specs=in_specs, out_specs=out_specs, out_shape=out_shape, scratch_shapes=scratch,
        compiler_params=_cparams(("arbitrary", "arbitrary")),
        name="gla",
    )(px, px, px, px, pxz, pc, pc, pc, pc, pcz, wz, bz, out_gain.reshape(1, dv), cos, sin)
    return outs if with_ctx_out else (outs[0], None)


def kernel(x, c, ctx, c_ctx, w_mod, b_mod, g_mix, w_in, gla_gate_w, gla_gate_b, gla_out_g,
           hy_conv_w, hy_conv_b, hy_w1, hy_b1, hy_w2, hy_b2, hy_w3, hy_freq, hy_bias, fn_w,
           na_q_g, na_k_g, na_rpb, w_out, g_ffn, ffn_w_up, ffn_conv_w, ffn_conv_b, ffn_w_down):
    bsz, l_lat, d = x.shape
    l_ctx = ctx.shape[1]
    depth = w_mod.shape[0]
    w = d // 4

    n_cond = -(-(bsz + 1) // SUBLANES) * SUBLANES
    cond = jnp.zeros((n_cond, d), F32).at[:bsz].set(c).at[bsz].set(c_ctx)
    mods = mod_vectors(cond, w_mod, b_mod).reshape(depth, n_cond, N_MOD, d)
    mods = jnp.pad(mods, ((0, 0), (0, 0), (0, MOD_ROWS - N_MOD), (0, 0)))

    z0 = 3 * w
    zw = 2 * GLA_GATE_RANK
    w_main = jnp.concatenate([w_in[:, :, :z0], w_in[:, :, z0 + zw:]], axis=-1).astype(BF16)
    w_z = jnp.pad(w_in[:, :, z0:z0 + zw], ((0, 0), (0, 0), (0, LANES - zw))).astype(BF16)
    w_out_b = w_out.astype(BF16)
    w_up_b = ffn_up_tiles(ffn_w_up)
    w_down_b = ffn_w_down.astype(BF16)

    m_lat = _hyena_dft_matrix(l_lat)
    m_ctx = _hyena_dft_matrix(l_ctx)
    ctx_flat = ctx.reshape(1, bsz * l_ctx, d)
    for layer in range(depth):
        last = layer == depth - 1
        mod_x = mods[layer, :bsz]
        mod_c = mods[layer, bsz:bsz + 1]
        g_mix_l = g_mix[layer].reshape(1, d)
        g_ffn_l = g_ffn[layer].reshape(1, d)

        px, pxz = in_proj(x, mod_x, g_mix_l, w_main, w_z, layer, tm=1024)
        pc, pcz = in_proj(ctx_flat, mod_c, g_mix_l, w_main, w_z, layer, tm=1024)
        pc = pc.reshape(bsz, l_ctx, -1)
        pcz = pcz.reshape(bsz, l_ctx, -1)

        y_a, yc_a = gla(px, pxz, pc, pcz, gla_gate_w[layer], gla_gate_b[layer], gla_out_g[layer],
                        with_ctx_out=not last)
        hy_filt = (hy_w1[layer], hy_b1[layer], hy_w2[layer], hy_b2[layer], hy_w3[layer], hy_freq[layer])
        hy_p, hy_q = hyena_filter_spectrum(l_lat, *hy_filt, m_lat)
        y_b = hyena_conv(px, 3 * w, hy_conv_w[layer], hy_conv_b[layer], hy_bias[layer], hy_p, hy_q, m_lat)
        y_c = fourier_mix(px, 6 * w, fn_w[layer])
        na_bias = natten_bias_table(na_rpb[layer], l_lat // GRID_W)
        y_d, yc_d = natten(px, pc, 7 * w, na_q_g[layer].reshape(1, -1), na_k_g[layer].reshape(1, -1),
                           na_bias, with_ctx_out=not last)
        x = out_proj([y_a, y_b, y_c, y_d], w_out_b, layer, x, mod_x, tm=512)
        cw = ffn_conv_w[layer]
        cb = ffn_conv_b[layer].reshape(1, -1)
        x = conv_ffn(x, mod_x, g_ffn_l, w_up_b, cw, cb, w_down_b, layer, tm=1024)

        if not last:
            hc_p, hc_q = hyena_filter_spectrum(l_ctx, *hy_filt, m_ctx)
            yc_b = hyena_conv(pc, 3 * w, hy_conv_w[layer], hy_conv_b[layer], hy_bias[layer], hc_p, hc_q, m_ctx)
            yc_c = fourier_mix(pc, 6 * w, fn_w[layer])
            ycs = [y.reshape(1, bsz * l_ctx, w) for y in (yc_a, yc_b, yc_c, yc_d)]
            ctx_flat = out_proj(ycs, w_out_b, layer, ctx_flat, mod_c, tm=512)
            ctx_flat = conv_ffn(ctx_flat, mod_c, g_ffn_l, w_up_b, cw, cb, w_down_b, layer, tm=1024,
                                seg_len=l_ctx)
    return x
```

```python
import functools
import math

import numpy as np
import jax
import jax.numpy as jnp
from jax import lax
from jax.experimental import pallas as pl
from jax.experimental.pallas import tpu as pltpu

F32 = jnp.float32
BF16 = jnp.bfloat16

GRID_W = 64
GLA_HEADS = 4
GLA_GATE_RANK = 16
GLA_TAU = 16.0
GLA_CHUNK = 64
HY_BANDS = 16
HY_DECAY_TARGET = 1e-2
HY_FAST_PCT = 0.3
HY_SLOW_PCT = 1.5
HY_DECAY_SHIFT = 0.05
FN_GROUPS = 4
NA_HEADS = 4
NA_KR_MAX = 8
NA_KC = 16
ROPE_THETA = 10000.0
N_MOD = 6
EPS = 1e-6

V7X_VMEM_LIMIT = 58 * 1024 * 1024
SUBLANES = 8
LANES = 128
MOD_ROWS = 8
FFN_HALO = 16


def _cparams(sem):
    return pltpu.CompilerParams(dimension_semantics=sem, vmem_limit_bytes=V7X_VMEM_LIMIT)


def _mod_kernel(s_ref, w_ref, b_ref, o_ref):
    s = s_ref[...]
    s = s * jax.nn.sigmoid(s)
    o_ref[...] = jnp.dot(s.astype(BF16), w_ref[...].astype(BF16),
                         preferred_element_type=F32) + b_ref[...]


def mod_vectors(cond, w_mod, b_mod, tn=1024):
    depth, d, n = w_mod.shape
    r = cond.shape[0]
    return pl.pallas_call(
        _mod_kernel,
        grid=(depth, n // tn),
        in_specs=[pl.BlockSpec((r, d), lambda l, j: (0, 0)),
                  pl.BlockSpec((None, d, tn), lambda l, j: (l, 0, j)),
                  pl.BlockSpec((None, 1, tn), lambda l, j: (l, 0, j))],
        out_specs=pl.BlockSpec((None, r, tn), lambda l, j: (l, 0, j)),
        out_shape=jax.ShapeDtypeStruct((depth, r, n), F32),
        compiler_params=_cparams(("arbitrary", "arbitrary")),
        name="mod_vectors",
    )(cond, w_mod, b_mod.reshape(depth, 1, n))


def _norm_mod(x, gain, shift, scale):
    ms = jnp.mean(x * x, axis=-1, keepdims=True)
    return (x * lax.rsqrt(ms + EPS) * gain) * (1.0 + scale) + shift


def _inproj_kernel(x_ref, mod_ref, g_ref, w_ref, wz_ref, o_ref, oz_ref, h_ref):
    j = pl.program_id(2)
    tn = o_ref.shape[1]

    @pl.when(j == 0)
    def _():
        h = _norm_mod(x_ref[...], g_ref[...], mod_ref[0:1, :], mod_ref[1:2, :]).astype(BF16)
        h_ref[...] = h
        oz_ref[...] = jnp.dot(h, wz_ref[...], preferred_element_type=F32)

    w = w_ref[:, pl.ds(pl.multiple_of(j * tn, tn), tn)]
    o_ref[...] = jnp.dot(h_ref[...], w, preferred_element_type=F32)


def in_proj(x, mod, gain, w_main, w_z, layer, tm, tn=1024):
    g, l, d = x.shape
    n = w_main.shape[-1]
    nz = w_z.shape[-1]
    return pl.pallas_call(
        _inproj_kernel,
        grid=(g, l // tm, n // tn),
        in_specs=[pl.BlockSpec((None, tm, d), lambda b, i, j: (b, i, 0)),
                  pl.BlockSpec((None, MOD_ROWS, d), lambda b, i, j: (b, 0, 0)),
                  pl.BlockSpec((1, d), lambda b, i, j: (0, 0)),
                  pl.BlockSpec((None, d, n), lambda b, i, j: (layer, 0, 0), pipeline_mode=pl.Buffered(1)),
                  pl.BlockSpec((None, d, nz), lambda b, i, j: (layer, 0, 0))],
        out_specs=[pl.BlockSpec((None, tm, tn), lambda b, i, j: (b, i, j)),
                   pl.BlockSpec((None, tm, nz), lambda b, i, j: (b, i, 0))],
        out_shape=[jax.ShapeDtypeStruct((g, l, n), F32),
                   jax.ShapeDtypeStruct((g, l, nz), F32)],
        scratch_shapes=[pltpu.VMEM((tm, d), BF16)],
        compiler_params=_cparams(("arbitrary", "arbitrary", "arbitrary")),
        name="in_proj",
    )(x, mod, gain, w_main, w_z)


def _outproj_kernel(ya_ref, yb_ref, yc_ref, yd_ref, w_ref, x_ref, mod_ref, o_ref):
    kw = ya_ref.shape[-1]
    acc = jnp.dot(ya_ref[...], w_ref[0 * kw:1 * kw, :], preferred_element_type=F32)
    acc += jnp.dot(yb_ref[...], w_ref[1 * kw:2 * kw, :], preferred_element_type=F32)
    acc += jnp.dot(yc_ref[...], w_ref[2 * kw:3 * kw, :], preferred_element_type=F32)
    acc += jnp.dot(yd_ref[...], w_ref[3 * kw:4 * kw, :], preferred_element_type=F32)
    o_ref[...] = x_ref[...] + mod_ref[2:3, :] * acc


def out_proj(ys, w_out, layer, x, mod, tm):
    g, l, d = x.shape
    kw = ys[0].shape[-1]
    yspec = pl.BlockSpec((None, tm, kw), lambda b, i: (b, i, 0))
    return pl.pallas_call(
        _outproj_kernel,
        grid=(g, l // tm),
        in_specs=[yspec, yspec, yspec, yspec,
                  pl.BlockSpec((None, 4 * kw, d), lambda b, i: (layer, 0, 0)),
                  pl.BlockSpec((None, tm, d), lambda b, i: (b, i, 0)),
                  pl.BlockSpec((None, MOD_ROWS, d), lambda b, i: (b, 0, 0))],
        out_specs=pl.BlockSpec((None, tm, d), lambda b, i: (b, i, 0)),
        out_shape=jax.ShapeDtypeStruct((g, l, d), F32),
        compiler_params=_cparams(("arbitrary", "arbitrary")),
        name="out_proj",
    )(*ys, w_out, x, mod)


def _ffn_kernel(x_ref, xp_ref, xn_ref, mod_ref, g_ref, wa_ref, wg_ref, cw_ref, cb_ref, wd_ref,
                o_ref, h_ref, act0_ref, act1_ref, *, seg_len, n_hidden_tiles):
    i = pl.program_id(1)
    j = pl.program_id(2)
    nf = n_hidden_tiles
    acts = (act0_ref, act1_ref)
    tm = x_ref.shape[0]
    tf = wa_ref.shape[1]
    hs = FFN_HALO

    def prologue():
        gain, shift, scale = g_ref[...], mod_ref[3:4, :], mod_ref[4:5, :]
        h_ref[hs:hs + tm, :] = _norm_mod(x_ref[...], gain, shift, scale).astype(BF16)
        hp = jnp.where(i > 0, _norm_mod(xp_ref[...], gain, shift, scale), 0.0)
        hn = jnp.where(i < pl.num_programs(1) - 1, _norm_mod(xn_ref[...], gain, shift, scale), 0.0)
        zero = jnp.zeros_like(hp)
        h_ref[0:hs, :] = jnp.concatenate([zero, hp], axis=0).astype(BF16)
        h_ref[hs + tm:2 * hs + tm, :] = jnp.concatenate([hn, zero], axis=0).astype(BF16)
        o_ref[...] = x_ref[...]

    def conv(u, c0, width):
        w = cw_ref[:, c0:c0 + width]
        prev, nxt = u[hs - 1:hs - 1 + tm], u[hs + 1:hs + 1 + tm]
        if seg_len is not None:
            pos = lax.broadcasted_iota(jnp.int32, prev.shape, 0) % seg_len
            prev = jnp.where(pos == 0, 0.0, prev)
            nxt = jnp.where(pos == seg_len - 1, 0.0, nxt)
        return cb_ref[:, c0:c0 + width] + prev * w[0:1] + u[hs:hs + tm] * w[1:2] + nxt * w[2:3]

    def up_matmuls():
        h = h_ref[...]
        return (jnp.dot(h, wa_ref[...], preferred_element_type=F32),
                jnp.dot(h, wg_ref[...], preferred_element_type=F32))

    def gate_to(act_ref, ua, ug):
        a = conv(ua, 0, tf)
        gt = conv(ug, tf, tf)
        act_ref[...] = (a * (gt * jax.nn.sigmoid(gt))).astype(BF16)

    def down_from(act_ref):
        o_ref[...] += mod_ref[5:6, :] * jnp.dot(act_ref[...], wd_ref[...], preferred_element_type=F32)

    middle = jnp.logical_and(j > 0, j < nf)

    @pl.when(j == 0)
    def _():
        prologue()
        ua, ug = up_matmuls()
        gate_to(acts[0], ua, ug)

    for parity in range(2):
        @pl.when(jnp.logical_and(middle, j % 2 == parity))
        def _():
            ua, ug = up_matmuls()
            down_from(acts[1 - parity])
            gate_to(acts[parity], ua, ug)

    @pl.when(j == nf)
    def _():
        down_from(acts[(n_hidden_tiles - 1) % 2])


def conv_ffn(x, mod, gain, w_up, conv_w, conv_b, w_down, layer, tm, tf=512, seg_len=None):
    g, l, d = x.shape
    assert seg_len is None or (tm % seg_len == 0 and l % tm == 0)
    f = w_down.shape[1]
    nf = f // tf
    nb = tm // SUBLANES
    last = l // SUBLANES - 1
    cw = conv_w.reshape(3, 2, nf, tf).transpose(2, 0, 1, 3).reshape(nf, 3, 2 * tf)
    cb = conv_b.reshape(1, 2, nf, tf).transpose(2, 0, 1, 3).reshape(nf, 1, 2 * tf)
    nt = l // tm

    def x_tile(b, i, j):
        n = jnp.minimum(b * nt + i + jnp.minimum(j, 1), g * nt - 1)
        return (n // nt, n % nt, 0)

    def up_tile(j):
        return jnp.where(j < nf, j, 0)

    def down_tile(j):
        return jnp.where(j == 0, nf - 1, j - 1)

    return pl.pallas_call(
        functools.partial(_ffn_kernel, seg_len=seg_len, n_hidden_tiles=nf),
        grid=(g, l // tm, nf + 1),
        in_specs=[pl.BlockSpec((None, tm, d), x_tile),
                  pl.BlockSpec((None, SUBLANES, d), lambda b, i, j: (b, jnp.maximum(i * nb - 1, 0), 0)),
                  pl.BlockSpec((None, SUBLANES, d), lambda b, i, j: (b, jnp.minimum((i + 1) * nb, last), 0)),
                  pl.BlockSpec((None, MOD_ROWS, d), lambda b, i, j: (b, 0, 0)),
                  pl.BlockSpec((1, d), lambda b, i, j: (0, 0)),
                  pl.BlockSpec((None, d, tf), lambda b, i, j: (layer, 0, up_tile(j))),
                  pl.BlockSpec((None, d, tf), lambda b, i, j: (layer, 0, nf + up_tile(j))),
                  pl.BlockSpec((None, 3, 2 * tf), lambda b, i, j: (up_tile(j), 0, 0)),
                  pl.BlockSpec((None, 1, 2 * tf), lambda b, i, j: (up_tile(j), 0, 0)),
                  pl.BlockSpec((None, tf, d), lambda b, i, j: (layer, down_tile(j), 0))],
        out_specs=pl.BlockSpec((None, tm, d), lambda b, i, j: (b, i, 0)),
        out_shape=jax.ShapeDtypeStruct((g, l, d), F32),
        scratch_shapes=[pltpu.VMEM((tm + 2 * FFN_HALO, d), BF16),
                        pltpu.VMEM((tm, tf), BF16), pltpu.VMEM((tm, tf), BF16)],
        compiler_params=_cparams(("arbitrary", "arbitrary", "arbitrary")),
        name="conv_ffn",
    )(x, x, x, mod, gain, w_up, w_up, cw, cb, w_down)


def natten_bias_table(rpb, n_rows):
    kr = min(NA_KR_MAX, n_rows)
    h, n_dr, n_dc = rpb.shape
    col = np.arange(GRID_W)
    col0 = np.clip(col - NA_KC // 2, 0, GRID_W - NA_KC)
    in_win = (col[None, :] >= col0[:, None]) & (col[None, :] < col0[:, None] + NA_KC)
    lo = GRID_W - NA_KC
    ext = jnp.pad(rpb.astype(F32), ((0, 0), (0, 0), (lo, 2 * GRID_W - lo - n_dc)))
    skew = jnp.tile(ext, (1, 1, GRID_W))[:, :, :GRID_W * (2 * GRID_W - 1)]
    skew = skew.reshape(h, n_dr, GRID_W, 2 * GRID_W - 1)[:, :, :, GRID_W - 1:]
    skew = jnp.where(in_win[None, None], skew, -jnp.inf)
    tab = jnp.stack([skew[:, NA_KR_MAX - 1 - s:NA_KR_MAX - 1 - s + kr] for s in range(kr)], axis=1)
    return tab.transpose(0, 1, 3, 2, 4).reshape(h, kr, GRID_W, kr * GRID_W)


NA_ROW_GROUP = 8


def _rms(x, gain):
    return x * lax.rsqrt(jnp.mean(x * x, axis=-1, keepdims=True) + EPS) * gain


def _dot_nt(a, b):
    return lax.dot_general(a, b, (((1,), (1,)), ((), ())), preferred_element_type=F32)


def _natten_kernel(*refs, n_rows, with_ctx_out):
    if with_ctx_out:
        (q_ref, k_ref, v_ref, kc_ref, vc_ref, gq_ref, gk_ref, bias_ref, qc_ref,
         o_ref, oc_ref, qs, ks, vs) = refs
    else:
        q_ref, k_ref, v_ref, kc_ref, vc_ref, gq_ref, gk_ref, bias_ref, o_ref, qs, ks, vs = refs
    dh = q_ref.shape[-1]
    kr = min(NA_KR_MAX, n_rows)
    scale = dh ** -0.5
    qs[...] = (_rms(q_ref[...], gq_ref[...]) * scale).astype(BF16)
    ks[...] = _rms(k_ref[...], gk_ref[...]).astype(BF16)
    vs[...] = v_ref[...].astype(BF16)
    kc = _rms(kc_ref[...], gk_ref[...]).astype(BF16)
    vc = vc_ref[...].astype(BF16)

    grp = NA_ROW_GROUP
    nk = kr * GRID_W

    def rows_group(gi, carry):
        r0 = gi * grp
        q0 = pl.multiple_of(r0 * GRID_W, grp * GRID_W)
        ws = [jnp.clip(r0 + t - kr // 2, 0, n_rows - kr) for t in range(grp)]
        k0 = [pl.multiple_of(w * GRID_W, GRID_W) for w in ws]
        s_loc = [_dot_nt(qs[pl.ds(q0 + t * GRID_W, GRID_W), :], ks[pl.ds(k0[t], nk), :]) for t in range(grp)]
        s_ctx = _dot_nt(qs[pl.ds(q0, grp * GRID_W), :], kc)
        p_loc, p_ctx, den = [], [], []
        for t in range(grp):
            sl = s_loc[t] + bias_ref[r0 + t - ws[t]]
            sc = s_ctx[t * GRID_W:(t + 1) * GRID_W]
            m = jnp.maximum(jnp.max(sl, axis=-1, keepdims=True), jnp.max(sc, axis=-1, keepdims=True))
            pl_t = jnp.exp(sl - m)
            pc_t = jnp.exp(sc - m)
            den.append(jnp.sum(pl_t, axis=-1, keepdims=True) + jnp.sum(pc_t, axis=-1, keepdims=True))
            p_loc.append(pl_t.astype(BF16))
            p_ctx.append(pc_t.astype(BF16))
        o_loc = [jnp.dot(p_loc[t], vs[pl.ds(k0[t], nk), :], preferred_element_type=F32) for t in range(grp)]
        o_ctx = jnp.dot(jnp.concatenate(p_ctx, axis=0), vc, preferred_element_type=F32)
        for t in range(grp):
            o = (o_loc[t] + o_ctx[t * GRID_W:(t + 1) * GRID_W]) / den[t]
            o_ref[pl.ds(q0 + t * GRID_W, GRID_W), :] = o.astype(o_ref.dtype)
        return carry

    lax.fori_loop(0, n_rows // grp, rows_group, 0)

    if with_ctx_out:
        qc = (_rms(qc_ref[...], gq_ref[...]) * scale).astype(BF16)
        s = _dot_nt(qc, kc)
        p = jnp.exp(s - jnp.max(s, axis=-1, keepdims=True))
        o = jnp.dot(p.astype(BF16), vc, preferred_element_type=F32) / jnp.sum(p, axis=-1, keepdims=True)
        oc_ref[...] = o.astype(oc_ref.dtype)


def natten(px, pc, col_q, gq, gk, bias, with_ctx_out):
    bsz, l, _ = px.shape
    lc = pc.shape[1]
    h, kr, _, nk = bias.shape
    dh = gq.shape[-1]
    cb = col_q // dh
    assert (l // GRID_W) % NA_ROW_GROUP == 0

    def head_spec(rows, which):
        return pl.BlockSpec((None, rows, dh), lambda b, hh: (b, 0, cb + which * h + hh))

    in_specs = [head_spec(l, 0), head_spec(l, 1), head_spec(l, 2), head_spec(lc, 1), head_spec(lc, 2),
                pl.BlockSpec((1, dh), lambda b, hh: (0, 0)), pl.BlockSpec((1, dh), lambda b, hh: (0, 0)),
                pl.BlockSpec((None, kr, GRID_W, nk), lambda b, hh: (hh, 0, 0, 0))]
    args = [px, px, px, pc, pc, gq, gk, bias]
    out_specs = [pl.BlockSpec((None, l, dh), lambda b, hh: (b, 0, hh))]
    out_shape = [jax.ShapeDtypeStruct((bsz, l, h * dh), BF16)]
    if with_ctx_out:
        in_specs.append(head_spec(lc, 0))
        args.append(pc)
        out_specs.append(pl.BlockSpec((None, lc, dh), lambda b, hh: (b, 0, hh)))
        out_shape.append(jax.ShapeDtypeStruct((bsz, lc, h * dh), BF16))
    outs = pl.pallas_call(
        functools.partial(_natten_kernel, n_rows=l // GRID_W, with_ctx_out=with_ctx_out),
        grid=(bsz, h),
        in_specs=in_specs, out_specs=out_specs, out_shape=out_shape,
        scratch_shapes=[pltpu.VMEM((l, dh), BF16)] * 3,
        compiler_params=_cparams(("arbitrary", "arbitrary")),
        name="natten",
    )(*args)
    return outs if with_ctx_out else (outs[0], None)


def _cos_sin(n, period):
    k = np.arange(n, dtype=np.int64)
    ang = (2.0 * np.pi / period) * ((k[:, None] * k[None, :]) % period)
    return np.cos(ang), np.sin(ang)


def _fourier_kernel(u_ref, w_ref, cd_ref, sd_ref, cs_ref, o_ref, wc_ref, ws_ref, v_ref, *, scale):
    l, c = u_ref.shape
    dg = cd_ref.shape[0]

    @pl.when(pl.program_id(0) == 0)
    def _():
        wc_ref[...] = jnp.zeros_like(wc_ref)
        ws_ref[...] = jnp.zeros_like(ws_ref)
        for g in range(c // dg):
            sl = slice(g * dg, (g + 1) * dg)
            wg = w_ref[g]
            wc_ref[sl, sl] = jnp.dot(cd_ref[...], wg, precision=lax.Precision.HIGHEST,
                                     preferred_element_type=F32).astype(BF16)
            ws_ref[sl, sl] = jnp.dot(sd_ref[...], wg, precision=lax.Precision.HIGHEST,
                                     preferred_element_type=F32).astype(BF16)

    u = u_ref[...].astype(BF16)
    v_ref[0:l, :] = jnp.dot(u, wc_ref[...], preferred_element_type=F32).astype(BF16)
    v_ref[l:2 * l, :] = jnp.dot(u, ws_ref[...], preferred_element_type=F32).astype(BF16)
    y = jnp.dot(cs_ref[...], v_ref[...], preferred_element_type=F32)
    o_ref[...] = (y * scale).astype(o_ref.dtype)


def fourier_mix(p, col, w):
    bsz, l, _ = p.shape
    g, dg, _ = w.shape
    c = g * dg
    cl, sl = _cos_sin(l, l)
    cd, sd = _cos_sin(dg, dg)
    cs = jnp.asarray(np.concatenate([cl, -sl], axis=1), BF16)
    return pl.pallas_call(
        functools.partial(_fourier_kernel, scale=float((l * dg) ** -0.5)),
        grid=(bsz,),
        in_specs=[pl.BlockSpec((None, l, c), lambda b: (b, 0, col // c)),
                  pl.BlockSpec((g, dg, dg), lambda b: (0, 0, 0)),
                  pl.BlockSpec((dg, dg), lambda b: (0, 0)),
                  pl.BlockSpec((dg, dg), lambda b: (0, 0)),
                  pl.BlockSpec((l, 2 * l), lambda b: (0, 0), pipeline_mode=pl.Buffered(1))],
        out_specs=pl.BlockSpec((None, l, c), lambda b: (b, 0, 0)),
        out_shape=jax.ShapeDtypeStruct((bsz, l, c), BF16),
        scratch_shapes=[pltpu.VMEM((c, c), BF16), pltpu.VMEM((c, c), BF16), pltpu.VMEM((2 * l, c), BF16)],
        compiler_params=_cparams(("arbitrary",)),
        name="fourier_mix",
    )(p, w, jnp.asarray(cd, F32), jnp.asarray(sd, F32), cs)


HY_CBLK = 256
HY_FBLK = 512


def _hyena_dft_matrix(l):
    k = np.arange(l, dtype=np.int64)
    ang = (np.pi / l) * ((k[:, None] * k[None, :]) % (2 * l))
    sn = np.sin(ang)
    sn[0, :] = 1.0 - 2.0 * (k % 2)
    return jnp.asarray(np.concatenate([np.cos(ang), sn], axis=0), BF16)


def _hyena_pos_features(l):
    t = np.linspace(0.0, 1.0, l)[:, None]
    w = (2.0 * np.pi / l) * np.arange(l)[:, None]
    f = np.linspace(1e-4, HY_BANDS - 1, HY_BANDS)[None, :]
    z = np.concatenate([t, np.cos(f * w), -np.sin(f * w)], axis=-1)
    return np.pad(z, ((0, 0), (0, LANES - z.shape[1])))


def _split_bf16(x):
    hi = x.astype(BF16)
    return hi, (x - hi.astype(F32)).astype(BF16)


def _hyena_filter_kernel(z_ref, w1_ref, b1_ref, w2_ref, b2_ref, fr_ref, w3f_ref, w3b_ref, dl_ref, m_ref,
                         p_ref, q_ref):
    l = z_ref.shape[0]
    hp = lax.Precision.HIGHEST
    z = z_ref[...]
    h = jnp.sin(fr_ref[0:1, :] * (jnp.dot(z, w1_ref[...], precision=hp, preferred_element_type=F32)
                                  + b1_ref[...]))
    h = jnp.sin(fr_ref[1:2, :] * (jnp.dot(h, w2_ref[...], precision=hp, preferred_element_type=F32)
                                  + b2_ref[...]))
    window = jnp.exp(-z[:, 0:1] * dl_ref[...]) + HY_DECAY_SHIFT
    hf = jnp.dot(h, w3f_ref[...], precision=hp, preferred_element_type=F32) * window
    hb = jnp.dot(h, w3b_ref[...], precision=hp, preferred_element_type=F32) * window
    norm = jnp.sum(jnp.abs(hf) + jnp.abs(hb), axis=0, keepdims=True) + EPS
    hf = hf / norm
    hb = hb / norm
    g1h, g1l = _split_bf16(hf + hb)
    g2h, g2l = _split_bf16(hb - hf)
    f1 = (jnp.dot(m_ref[...], g1h, preferred_element_type=F32)
          + jnp.dot(m_ref[...], g1l, preferred_element_type=F32))
    f2 = (jnp.dot(m_ref[l:2 * l, :], g2h, preferred_element_type=F32)
          + jnp.dot(m_ref[l:2 * l, :], g2l, preferred_element_type=F32))
    p_ref[...] = f1[0:l]
    row = lax.broadcasted_iota(jnp.int32, f2.shape, 0)
    q_ref[...] = jnp.where(row == 0, f1[l:l + 1], f2)


def hyena_filter_spectrum(l, w1, b1, w2, b2, w3, freq, m):
    c = w3.shape[1] // 2
    hid = w1.shape[1]
    z = jnp.asarray(_hyena_pos_features(l), F32)
    w1p = jnp.pad(w1, ((0, z.shape[1] - w1.shape[0]), (0, 0)))
    deltas = np.abs(np.linspace(math.log(HY_DECAY_TARGET) / HY_SLOW_PCT,
                                math.log(HY_DECAY_TARGET) / HY_FAST_PCT, c))[None, :]
    nb = c // HY_CBLK
    full = lambda shape: pl.BlockSpec(shape, lambda j: (0,) * len(shape))
    return pl.pallas_call(
        _hyena_filter_kernel,
        grid=(nb,),
        in_specs=[full(z.shape), full(w1p.shape), full((1, hid)), full(w2.shape), full((1, hid)),
                  full((2, hid)),
                  pl.BlockSpec((hid, HY_CBLK), lambda j: (0, j)),
                  pl.BlockSpec((hid, HY_CBLK), lambda j: (0, nb + j)),
                  pl.BlockSpec((1, HY_CBLK), lambda j: (0, j)),
                  pl.BlockSpec(m.shape, lambda j: (0, 0), pipeline_mode=pl.Buffered(1))],
        out_specs=[pl.BlockSpec((l, HY_CBLK), lambda j: (0, j))] * 2,
        out_shape=[jax.ShapeDtypeStruct((l, c), F32)] * 2,
        compiler_params=_cparams(("arbitrary",)),
        name="hyena_filter",
    )(z, w1p, b1.reshape(1, hid), w2, b2.reshape(1, hid), freq, w3, w3, jnp.asarray(deltas, F32), m)


def _shift_rows(u, down):
    l = u.shape[0]
    row = lax.broadcasted_iota(jnp.int32, u.shape, 0)
    if down:
        return jnp.where(row == 0, 0.0, pltpu.roll(u, 1, 0))
    return jnp.where(row == l - 1, 0.0, pltpu.roll(u, l - 1, 0))


def _dwconv3(u, taps):
    return (taps[3:4] + _shift_rows(u, True) * taps[0:1] + u * taps[1:2] + _shift_rows(u, False) * taps[2:3])


def _hyena_conv_kernel(x0_ref, x1_ref, v_ref, taps_ref, bias_ref, p_ref, q_ref, m_ref, o_ref):
    l, c = x0_ref.shape
    x1 = _dwconv3(x1_ref[...], taps_ref[1])
    s = _dwconv3(v_ref[...], taps_ref[2]) * x1
    sb = s.astype(BF16)
    fb = min(HY_FBLK, l)
    nblk = l // fb
    inv_l = 1.0 / l

    def forward(i):
        return (jnp.dot(m_ref[i * fb:(i + 1) * fb, :], sb, preferred_element_type=F32),
                jnp.dot(m_ref[l + i * fb:l + (i + 1) * fb, :], sb, preferred_element_type=F32))

    y_cos_acc = y_sin_acc = nyquist = None
    ab = forward(0)
    for i in range(nblk):
        a, b = ab
        if i + 1 < nblk:
            ab = forward(i + 1)
        p, q = p_ref[i * fb:(i + 1) * fb, :], q_ref[i * fb:(i + 1) * fb, :]
        bq = b * q
        if i == 0:
            first = lax.broadcasted_iota(jnp.int32, (fb, c), 0) == 0
            nyquist = bq[0:1] * (0.5 * inv_l)
            y_cos = (a * p + jnp.where(first, 0.0, bq)) * jnp.where(first, 0.5 * inv_l, inv_l)
            y_sin = jnp.where(first, 0.0, (b * p - a * q) * inv_l)
        else:
            y_cos = (a * p + bq) * inv_l
            y_sin = (b * p - a * q) * inv_l
        d_cos = jnp.dot(m_ref[0:l, i * fb:(i + 1) * fb], y_cos.astype(BF16), preferred_element_type=F32)
        d_sin = jnp.dot(m_ref[l:2 * l, i * fb:(i + 1) * fb], y_sin.astype(BF16), preferred_element_type=F32)
        y_cos_acc = d_cos if y_cos_acc is None else y_cos_acc + d_cos
        y_sin_acc = d_sin if y_sin_acc is None else y_sin_acc + d_sin
    row = lax.broadcasted_iota(jnp.int32, (l, c), 0)
    y = (y_cos_acc + jnp.where(row == 0, 0.0, y_sin_acc)
         + jnp.where(row % 2 == 0, 1.0, -1.0) * nyquist)
    x0 = _dwconv3(x0_ref[...], taps_ref[0])
    o_ref[...] = ((y + s * bias_ref[...]) * x0).astype(o_ref.dtype)


def hyena_conv(p, col, conv_w, conv_b, bias, spec_p, spec_q, m):
    bsz, l, _ = p.shape
    c = bias.shape[-1]
    nb = c // HY_CBLK
    cb0 = col // HY_CBLK
    taps = jnp.concatenate([conv_w, conv_b[None]], axis=0)
    taps = taps.reshape(4, 3, nb, HY_CBLK).transpose(2, 1, 0, 3)

    def part(k):
        return pl.BlockSpec((None, l, HY_CBLK), lambda j, b: (b, 0, cb0 + k * nb + j))

    return pl.pallas_call(
        _hyena_conv_kernel,
        grid=(nb, bsz),
        in_specs=[part(0), part(1), part(2),
                  pl.BlockSpec((None, 3, 4, HY_CBLK), lambda j, b: (j, 0, 0, 0)),
                  pl.BlockSpec((1, HY_CBLK), lambda j, b: (0, j)),
                  pl.BlockSpec((l, HY_CBLK), lambda j, b: (0, j)),
                  pl.BlockSpec((l, HY_CBLK), lambda j, b: (0, j)),
                  pl.BlockSpec(m.shape, lambda j, b: (0, 0), pipeline_mode=pl.Buffered(1))],
        out_specs=pl.BlockSpec((None, l, HY_CBLK), lambda j, b: (b, 0, j)),
        out_shape=jax.ShapeDtypeStruct((bsz, l, c), BF16),
        compiler_params=_cparams(("arbitrary", "arbitrary")),
        name="hyena_conv",
    )(p, p, p, taps, bias.reshape(1, c), spec_p, spec_q, m)


GLA_HP = 2
GLA_ROPE_PAIR = 16
GLA_SCAN_UNROLL = 4


def _gla_rope_tables(l, dk):
    half = dk // 2
    nf = half // 2
    assert nf == GLA_ROPE_PAIR
    inv = ROPE_THETA ** (-np.arange(nf, dtype=np.float64) / nf)
    t = np.arange(l)
    ang_r = (t // GRID_W)[:, None] * inv
    ang_c = (t % GRID_W)[:, None] * inv
    cos = np.concatenate([np.cos(ang_r)] * 2 + [np.cos(ang_c)] * 2, axis=1)
    sin = np.concatenate([-np.sin(ang_r), np.sin(ang_r), -np.sin(ang_c), np.sin(ang_c)], axis=1)
    return (jnp.asarray(np.tile(cos, (1, GLA_HP)), F32), jnp.asarray(np.tile(sin, (1, GLA_HP)), F32))


def _rope(x, cos, sin):
    lane = lax.broadcasted_iota(jnp.int32, x.shape, 1)
    lanes = x.shape[1]
    partner = jnp.where(lane % (2 * GLA_ROPE_PAIR) < GLA_ROPE_PAIR,
                        pltpu.roll(x, lanes - GLA_ROPE_PAIR, 1), pltpu.roll(x, GLA_ROPE_PAIR, 1))
    return x * cos + partner * sin


def _log_sigmoid(x):
    return jnp.minimum(x, 0.0) - jnp.log(1.0 + jnp.exp(-jnp.abs(x)))


def _gla_kernel(*refs, with_ctx_out):
    (q_ref, k_ref, v_ref, r_ref, z_ref, cq_ref, ck_ref, cv_ref, cr_ref, cz_ref,
     wz_ref, bz_ref, g_ref, cos_ref, sin_ref) = refs[:15]
    if with_ctx_out:
        o_ref, oc_ref = refs[15:17]
        scratch = refs[17:]
    else:
        o_ref, oc_ref = refs[15], None
        scratch = refs[16:]
    qs, ks, las, ofs, obs, cqs, cks, clas, cofs, cobs, st_f, st_b = scratch
    dk2 = q_ref.shape[1]
    dv2 = v_ref.shape[1]
    dk, dv = dk2 // GLA_HP, dv2 // GLA_HP
    ch = GLA_CHUNK
    hp = lax.Precision.HIGHEST

    def gates(z):
        pre = jnp.dot(z, wz_ref[...], precision=hp, preferred_element_type=F32) + bz_ref[...]
        return _log_sigmoid(pre) * (1.0 / GLA_TAU)

    qs[...] = _rope(q_ref[...] * dk ** -0.5, cos_ref[...], sin_ref[...])
    ks[...] = _rope(k_ref[...], cos_ref[...], sin_ref[...])
    las[...] = gates(z_ref[...])
    cqs[...] = cq_ref[...] * dk ** -0.5
    cks[...] = ck_ref[...]
    clas[...] = gates(cz_ref[...])
    st_f[...] = jnp.zeros_like(st_f)
    st_b[...] = jnp.zeros_like(st_b)

    ri = lax.broadcasted_iota(jnp.int32, (ch, ch), 0)
    ci = lax.broadcasted_iota(jnp.int32, (ch, ch), 1)
    tri = {False: ri >= ci, True: ri <= ci}
    tri_b16 = {d: jnp.where(m, 1.0, 0.0).astype(BF16) for d, m in tri.items()}
    tri2 = {d: jnp.concatenate([m] * GLA_HP, axis=0) for d, m in tri.items()}
    lane_head = lax.broadcasted_iota(jnp.int32, (ch, dk2), 1) // dk
    st_r = lax.broadcasted_iota(jnp.int32, (dv2, dk2), 0) // dv
    st_c = lax.broadcasted_iota(jnp.int32, (dv2, dk2), 1) // dk
    st_diag = st_r == st_c

    def scan(q_s, k_s, la_s, v_in, of_s, ob_s, n):
        unroll = GLA_SCAN_UNROLL

        def body(trip, carry):
            items = []
            for u in range(unroll):
                c = trip * unroll + u
                items.append((False, pl.ds(pl.multiple_of(c * ch, ch), ch)))
                items.append((True, pl.ds(pl.multiple_of((n - 1 - c) * ch, ch), ch)))
            v_c = [v_in[rows, :].astype(BF16) for _, rows in items]
            cum = []
            for bw, rows in items:
                la_hi, la_lo = _split_bf16(la_s[rows, dk2:2 * dk2] if bw else la_s[rows, 0:dk2])
                cum.append(jnp.dot(tri_b16[bw], la_hi, preferred_element_type=F32)
                           + jnp.dot(tri_b16[bw], la_lo, preferred_element_type=F32))
            q_dec, k_end, decay, sc = [], [], [], []
            for i, (bw, rows) in enumerate(items):
                q_c, k_c = q_s[rows, :], k_s[rows, :]
                tot = cum[i][0:1] if bw else cum[i][ch - 1:ch]
                qd = q_c * jnp.exp(cum[i])
                k_inv = (k_c * jnp.exp(-cum[i])).astype(BF16)
                k_end.append((k_c * jnp.exp(tot - cum[i])).astype(BF16))
                decay.append(jnp.exp(tot))
                q_heads = jnp.concatenate([jnp.where(lane_head == h, qd, 0.0) for h in range(GLA_HP)], axis=0)
                sc.append(_dot_nt(q_heads.astype(BF16), k_inv))
                q_dec.append(qd.astype(BF16))
            o_intra, ds_t = [], []
            for i, (bw, rows) in enumerate(items):
                pv = jnp.dot(jnp.where(tri2[bw], sc[i], 0.0).astype(BF16), v_c[i],
                             preferred_element_type=F32)
                o_intra.append(jnp.concatenate(
                    [pv[h * ch:(h + 1) * ch, h * dv:(h + 1) * dv] for h in range(GLA_HP)], axis=1))
                ds_t.append(lax.dot_general(v_c[i], k_end[i], (((0,), (0,)), ((), ())),
                                            preferred_element_type=F32))
            for i, (bw, rows) in enumerate(items):
                st, o_s = (st_b, ob_s) if bw else (st_f, of_s)
                s_t = st[...]
                o_s[rows, :] = o_intra[i] + _dot_nt(q_dec[i], s_t.astype(BF16))
                st[...] = s_t * decay[i] + jnp.where(st_diag, ds_t[i], 0.0)
            return carry

        lax.fori_loop(0, n // unroll, body, 0)

    def finish(of_s, ob_s, gate_ref, out_ref, n_blocks, blk):
        def body(i, carry):
            rows = pl.ds(pl.multiple_of(i * blk, blk), blk)
            o = of_s[rows, :] + ob_s[rows, :]
            gate = gate_ref[rows, :]
            parts = []
            for h in range(GLA_HP):
                oh = o[:, h * dv:(h + 1) * dv]
                gh = gate[:, h * dv:(h + 1) * dv]
                parts.append(_rms(oh, g_ref[...]) * (gh * jax.nn.sigmoid(gh)))
            out_ref[rows, :] = jnp.concatenate(parts, axis=1).astype(out_ref.dtype)
            return carry
        lax.fori_loop(0, n_blocks, body, 0)

    l, lc = q_ref.shape[0], cq_ref.shape[0]
    scan(cqs, cks, clas, cv_ref, cofs, cobs, lc // ch)
    if with_ctx_out:
        finish(cofs, cobs, cr_ref, oc_ref, 1, lc)
    scan(qs, ks, las, v_ref, ofs, obs, l // ch)
    finish(ofs, obs, r_ref, o_ref, l // lc, lc)


def gla(px, pxz, pc, pcz, w_gate, b_gate, out_gain, with_ctx_out):
    bsz, l, _ = px.shape
    lc = pc.shape[1]
    dv = out_gain.shape[-1]
    hdk = w_gate.shape[-1]
    dk = hdk // GLA_HEADS
    dk2, dv2 = GLA_HP * dk, GLA_HP * dv
    nhp = GLA_HEADS // GLA_HP
    zw = pxz.shape[-1]
    rank = w_gate.shape[1]
    wz = jnp.zeros((nhp, zw, 2 * dk2), F32)
    for u in range(2):
        blk = w_gate[u].reshape(rank, nhp, dk2).transpose(1, 0, 2)
        wz = wz.at[:, u * rank:(u + 1) * rank, u * dk2:(u + 1) * dk2].set(blk)
    bz = b_gate.reshape(2, nhp, dk2).transpose(1, 0, 2).reshape(nhp, 1, 2 * dk2)
    cos, sin = _gla_rope_tables(l, dk)
    k_cb, v_cb, r_cb = hdk // dk2, 2 * hdk // dv2, (2 * hdk + GLA_HEADS * dv) // dv2

    def col(rows, width, cb):
        return pl.BlockSpec((None, rows, width), lambda b, j: (b, 0, cb + j))

    def whole(rows, width):
        return pl.BlockSpec((None, rows, width), lambda b, j: (b, 0, 0))

    const = lambda shape: pl.BlockSpec(shape, lambda b, j: (0,) * len(shape))
    in_specs = [col(l, dk2, 0), col(l, dk2, k_cb), col(l, dv2, v_cb), col(l, dv2, r_cb), whole(l, zw),
                col(lc, dk2, 0), col(lc, dk2, k_cb), col(lc, dv2, v_cb), col(lc, dv2, r_cb), whole(lc, zw),
                pl.BlockSpec((None, zw, 2 * dk2), lambda b, j: (j, 0, 0)),
                pl.BlockSpec((None, 1, 2 * dk2), lambda b, j: (j, 0, 0)),
                const((1, dv)), const((l, dk2)), const((l, dk2))]
    out_specs = [pl.BlockSpec((None, l, dv2), lambda b, j: (b, 0, j))]
    out_shape = [jax.ShapeDtypeStruct((bsz, l, GLA_HEADS * dv), BF16)]
    if with_ctx_out:
        out_specs.append(pl.BlockSpec((None, lc, dv2), lambda b, j: (b, 0, j)))
        out_shape.append(jax.ShapeDtypeStruct((bsz, lc, GLA_HEADS * dv), BF16))
    scratch = [pltpu.VMEM((l, dk2), F32), pltpu.VMEM((l, dk2), F32), pltpu.VMEM((l, 2 * dk2), F32),
               pltpu.VMEM((l, dv2), F32), pltpu.VMEM((l, dv2), F32),
               pltpu.VMEM((lc, dk2), F32), pltpu.VMEM((lc, dk2), F32), pltpu.VMEM((lc, 2 * dk2), F32),
               pltpu.VMEM((lc, dv2), F32), pltpu.VMEM((lc, dv2), F32),
               pltpu.VMEM((dv2, dk2), F32), pltpu.VMEM((dv2, dk2), F32)]
    outs = pl.pallas_call(
        functools.partial(_gla_kernel, with_ctx_out=with_ctx_out),
        grid=(bsz, nhp),
        in_specs=in_specs, out_specs=out_specs, out_shape=out_shape, scratch_shapes=scratch,
        compiler_params=_cparams(("arbitrary", "arbitrary")),
        name="gla",
    )(px, px, px, px, pxz, pc, pc, pc, pc, pcz, wz, bz, out_gain.reshape(1, dv), cos, sin)
    return outs if with_ctx_out else (outs[0], None)


def kernel(x, c, ctx, c_ctx, w_mod, b_mod, g_mix, w_in, gla_gate_w, gla_gate_b, gla_out_g,
           hy_conv_w, hy_conv_b, hy_w1, hy_b1, hy_w2, hy_b2, hy_w3, hy_freq, hy_bias, fn_w,
           na_q_g, na_k_g, na_rpb, w_out, g_ffn, ffn_w_up, ffn_conv_w, ffn_conv_b, ffn_w_down):
    bsz, l_lat, d = x.shape
    l_ctx = ctx.shape[1]
    depth = w_mod.shape[0]
    w = d // 4

    n_cond = -(-(bsz + 1) // SUBLANES) * SUBLANES
    cond = jnp.zeros((n_cond, d), F32).at[:bsz].set(c).at[bsz].set(c_ctx)
    mods = mod_vectors(cond, w_mod, b_mod).reshape(depth, n_cond, N_MOD, d)
    mods = jnp.pad(mods, ((0, 0), (0, 0), (0, MOD_ROWS - N_MOD), (0, 0)))

    z0 = 3 * w
    zw = 2 * GLA_GATE_RANK
    w_main = jnp.concatenate([w_in[:, :, :z0], w_in[:, :, z0 + zw:]], axis=-1).astype(BF16)
    w_z = jnp.pad(w_in[:, :, z0:z0 + zw], ((0, 0), (0, 0), (0, LANES - zw))).astype(BF16)
    w_out_b = w_out.astype(BF16)
    w_up_b = ffn_w_up.astype(BF16)
    w_down_b = ffn_w_down.astype(BF16)

    m_lat = _hyena_dft_matrix(l_lat)
    m_ctx = _hyena_dft_matrix(l_ctx)
    ctx_flat = ctx.reshape(1, bsz * l_ctx, d)
    for layer in range(depth):
        last = layer == depth - 1
        mod_x = mods[layer, :bsz]
        mod_c = mods[layer, bsz:bsz + 1]
        g_mix_l = g_mix[layer].reshape(1, d)
        g_ffn_l = g_ffn[layer].reshape(1, d)

        px, pxz = in_proj(x, mod_x, g_mix_l, w_main, w_z, layer, tm=1024)
        pc, pcz = in_proj(ctx_flat, mod_c, g_mix_l, w_main, w_z, layer, tm=1024)
        pc = pc.reshape(bsz, l_ctx, -1)
        pcz = pcz.reshape(bsz, l_ctx, -1)

        y_a, yc_a = gla(px, pxz, pc, pcz, gla_gate_w[layer], gla_gate_b[layer], gla_out_g[layer],
                        with_ctx_out=not last)
        hy_filt = (hy_w1[layer], hy_b1[layer], hy_w2[layer], hy_b2[layer], hy_w3[layer], hy_freq[layer])
        hy_p, hy_q = hyena_filter_spectrum(l_lat, *hy_filt, m_lat)
        y_b = hyena_conv(px, 3 * w, hy_conv_w[layer], hy_conv_b[layer], hy_bias[layer], hy_p, hy_q, m_lat)
        y_c = fourier_mix(px, 6 * w, fn_w[layer])
        na_bias = natten_bias_table(na_rpb[layer], l_lat // GRID_W)
        y_d, yc_d = natten(px, pc, 7 * w, na_q_g[layer].reshape(1, -1), na_k_g[layer].reshape(1, -1),
                           na_bias, with_ctx_out=not last)
        x = out_proj([y_a, y_b, y_c, y_d], w_out_b, layer, x, mod_x, tm=512)
        cw = ffn_conv_w[layer]
        cb = ffn_conv_b[layer].reshape(1, -1)
        x = conv_ffn(x, mod_x, g_ffn_l, w_up_b, cw, cb, w_down_b, layer, tm=1024)

        if not last:
            hc_p, hc_q = hyena_filter_spectrum(l_ctx, *hy_filt, m_ctx)
            yc_b = hyena_conv(pc, 3 * w, hy_conv_w[layer], hy_conv_b[layer], hy_bias[layer], hc_p, hc_q, m_ctx)
            yc_c = fourier_mix(pc, 6 * w, fn_w[layer])
            ycs = [y.reshape(1, bsz * l_ctx, w) for y in (yc_a, yc_b, yc_c, yc_d)]
            ctx_flat = out_proj(ycs, w_out_b, layer, ctx_flat, mod_c, tm=512)
            ctx_flat = conv_ffn(ctx_flat, mod_c, g_ffn_l, w_up_b, cw, cb, w_down_b, layer, tm=1024,
                                seg_len=l_ctx)
    return x
```

```python
import functools
import math

import numpy as np
import jax
import jax.numpy as jnp
from jax import lax
from jax.experimental import pallas as pl
from jax.experimental.pallas import tpu as pltpu

F32 = jnp.float32
BF16 = jnp.bfloat16

GRID_W = 64
GLA_HEADS = 4
GLA_GATE_RANK = 16
GLA_TAU = 16.0
GLA_CHUNK = 64
HY_BANDS = 16
HY_DECAY_TARGET = 1e-2
HY_FAST_PCT = 0.3
HY_SLOW_PCT = 1.5
HY_DECAY_SHIFT = 0.05
FN_GROUPS = 4
NA_HEADS = 4
NA_KR_MAX = 8
NA_KC = 16
ROPE_THETA = 10000.0
N_MOD = 6
EPS = 1e-6

V7X_VMEM_LIMIT = 58 * 1024 * 1024
SUBLANES = 8
LANES = 128
MOD_ROWS = 8
FFN_HALO = 16


def _cparams(sem):
    return pltpu.CompilerParams(dimension_semantics=sem, vmem_limit_bytes=V7X_VMEM_LIMIT)


def _mod_kernel(s_ref, w_ref, b_ref, o_ref):
    s = s_ref[...]
    s = s * jax.nn.sigmoid(s)
    o_ref[...] = jnp.dot(s.astype(BF16), w_ref[...].astype(BF16),
                         preferred_element_type=F32) + b_ref[...]


def mod_vectors(cond, w_mod, b_mod, tn=1024):
    depth, d, n = w_mod.shape
    r = cond.shape[0]
    return pl.pallas_call(
        _mod_kernel,
        grid=(depth, n // tn),
        in_specs=[pl.BlockSpec((r, d), lambda l, j: (0, 0)),
                  pl.BlockSpec((None, d, tn), lambda l, j: (l, 0, j)),
                  pl.BlockSpec((None, 1, tn), lambda l, j: (l, 0, j))],
        out_specs=pl.BlockSpec((None, r, tn), lambda l, j: (l, 0, j)),
        out_shape=jax.ShapeDtypeStruct((depth, r, n), F32),
        compiler_params=_cparams(("arbitrary", "arbitrary")),
        name="mod_vectors",
    )(cond, w_mod, b_mod.reshape(depth, 1, n))


def _norm_mod(x, gain, shift, scale):
    ms = jnp.mean(x * x, axis=-1, keepdims=True)
    return (x * lax.rsqrt(ms + EPS) * gain) * (1.0 + scale) + shift


def _inproj_kernel(x_ref, mod_ref, g_ref, w_ref, wz_ref, o_ref, oz_ref, h_ref):
    j = pl.program_id(2)
    tn = o_ref.shape[1]

    @pl.when(j == 0)
    def _():
        h = _norm_mod(x_ref[...], g_ref[...], mod_ref[0:1, :], mod_ref[1:2, :]).astype(BF16)
        h_ref[...] = h
        oz_ref[...] = jnp.dot(h, wz_ref[...], preferred_element_type=F32)

    w = w_ref[:, pl.ds(pl.multiple_of(j * tn, tn), tn)]
    o_ref[...] = jnp.dot(h_ref[...], w, preferred_element_type=F32)


def in_proj(x, mod, gain, w_main, w_z, layer, tm, tn=1024):
    g, l, d = x.shape
    n = w_main.shape[-1]
    nz = w_z.shape[-1]
    nt = l // tm

    def x_tile(b, i, j):
        t = jnp.minimum(b * nt + i + jnp.minimum(j, 1), g * nt - 1)
        return (t // nt, t % nt, 0)

    return pl.pallas_call(
        _inproj_kernel,
        grid=(g, l // tm, n // tn),
        in_specs=[pl.BlockSpec((None, tm, d), x_tile),
                  pl.BlockSpec((None, MOD_ROWS, d), lambda b, i, j: (b, 0, 0)),
                  pl.BlockSpec((1, d), lambda b, i, j: (0, 0)),
                  pl.BlockSpec((None, d, n), lambda b, i, j: (layer, 0, 0), pipeline_mode=pl.Buffered(1)),
                  pl.BlockSpec((None, d, nz), lambda b, i, j: (layer, 0, 0))],
        out_specs=[pl.BlockSpec((None, tm, tn), lambda b, i, j: (b, i, j)),
                   pl.BlockSpec((None, tm, nz), lambda b, i, j: (b, i, 0))],
        out_shape=[jax.ShapeDtypeStruct((g, l, n), F32),
                   jax.ShapeDtypeStruct((g, l, nz), F32)],
        scratch_shapes=[pltpu.VMEM((tm, d), BF16)],
        compiler_params=_cparams(("arbitrary", "arbitrary", "arbitrary")),
        name="in_proj",
    )(x, mod, gain, w_main, w_z)


def _outproj_kernel(ya_ref, yb_ref, yc_ref, yd_ref, w_ref, x_ref, mod_ref, o_ref):
    kw = ya_ref.shape[-1]
    acc = jnp.dot(ya_ref[...], w_ref[0 * kw:1 * kw, :], preferred_element_type=F32)
    acc += jnp.dot(yb_ref[...], w_ref[1 * kw:2 * kw, :], preferred_element_type=F32)
    acc += jnp.dot(yc_ref[...], w_ref[2 * kw:3 * kw, :], preferred_element_type=F32)
    acc += jnp.dot(yd_ref[...], w_ref[3 * kw:4 * kw, :], preferred_element_type=F32)
    o_ref[...] = x_ref[...] + mod_ref[2:3, :] * acc


def out_proj(ys, w_out, layer, x, mod, tm):
    g, l, d = x.shape
    kw = ys[0].shape[-1]
    yspec = pl.BlockSpec((None, tm, kw), lambda b, i: (b, i, 0))
    return pl.pallas_call(
        _outproj_kernel,
        grid=(g, l // tm),
        in_specs=[yspec, yspec, yspec, yspec,
                  pl.BlockSpec((None, 4 * kw, d), lambda b, i: (layer, 0, 0), pipeline_mode=pl.Buffered(1)),
                  pl.BlockSpec((None, tm, d), lambda b, i: (b, i, 0)),
                  pl.BlockSpec((None, MOD_ROWS, d), lambda b, i: (b, 0, 0))],
        out_specs=pl.BlockSpec((None, tm, d), lambda b, i: (b, i, 0)),
        out_shape=jax.ShapeDtypeStruct((g, l, d), F32),
        compiler_params=_cparams(("arbitrary", "arbitrary")),
        name="out_proj",
    )(*ys, w_out, x, mod)


def _ffn_kernel(x_ref, xp_ref, xn_ref, mod_ref, g_ref, wa_ref, wg_ref, cw_ref, cb_ref, wd_ref,
                o_ref, h_ref, act0_ref, act1_ref, *, seg_len, n_hidden_tiles):
    i = pl.program_id(1)
    j = pl.program_id(2)
    nf = n_hidden_tiles
    acts = (act0_ref, act1_ref)
    tm = x_ref.shape[0]
    tf = wa_ref.shape[1]
    hs = FFN_HALO

    def prologue():
        gain, shift, scale = g_ref[...], mod_ref[3:4, :], mod_ref[4:5, :]
        h_ref[hs:hs + tm, :] = _norm_mod(x_ref[...], gain, shift, scale).astype(BF16)
        hp = jnp.where(i > 0, _norm_mod(xp_ref[...], gain, shift, scale), 0.0)
        hn = jnp.where(i < pl.num_programs(1) - 1, _norm_mod(xn_ref[...], gain, shift, scale), 0.0)
        zero = jnp.zeros_like(hp)
        h_ref[0:hs, :] = jnp.concatenate([zero, hp], axis=0).astype(BF16)
        h_ref[hs + tm:2 * hs + tm, :] = jnp.concatenate([hn, zero], axis=0).astype(BF16)
        o_ref[...] = x_ref[...]

    def conv(u, c0, width):
        w = cw_ref[:, c0:c0 + width]
        prev, nxt = u[hs - 1:hs - 1 + tm], u[hs + 1:hs + 1 + tm]
        if seg_len is not None:
            pos = lax.broadcasted_iota(jnp.int32, prev.shape, 0) % seg_len
            prev = jnp.where(pos == 0, 0.0, prev)
            nxt = jnp.where(pos == seg_len - 1, 0.0, nxt)
        return cb_ref[:, c0:c0 + width] + prev * w[0:1] + u[hs:hs + tm] * w[1:2] + nxt * w[2:3]

    def up_matmuls():
        h = h_ref[...]
        return (jnp.dot(h, wa_ref[...], preferred_element_type=F32),
                jnp.dot(h, wg_ref[...], preferred_element_type=F32))

    def gate_to(act_ref, ua, ug):
        a = conv(ua, 0, tf)
        gt = conv(ug, tf, tf)
        act_ref[...] = (a * (gt * jax.nn.sigmoid(gt))).astype(BF16)

    def down_from(act_ref):
        o_ref[...] += mod_ref[5:6, :] * jnp.dot(act_ref[...], wd_ref[...], preferred_element_type=F32)

    middle = jnp.logical_and(j > 0, j < nf)

    @pl.when(j == 0)
    def _():
        prologue()
        ua, ug = up_matmuls()
        gate_to(acts[0], ua, ug)

    for parity in range(2):
        @pl.when(jnp.logical_and(middle, j % 2 == parity))
        def _():
            ua, ug = up_matmuls()
            down_from(acts[1 - parity])
            gate_to(acts[parity], ua, ug)

    @pl.when(j == nf)
    def _():
        down_from(acts[(n_hidden_tiles - 1) % 2])


def conv_ffn(x, mod, gain, w_up, conv_w, conv_b, w_down, layer, tm, tf=512, seg_len=None):
    g, l, d = x.shape
    assert seg_len is None or (tm % seg_len == 0 and l % tm == 0)
    f = w_down.shape[1]
    nf = f // tf
    nb = tm // SUBLANES
    last = l // SUBLANES - 1
    cw = conv_w.reshape(3, 2, nf, tf).transpose(2, 0, 1, 3).reshape(nf, 3, 2 * tf)
    cb = conv_b.reshape(1, 2, nf, tf).transpose(2, 0, 1, 3).reshape(nf, 1, 2 * tf)
    nt = l // tm

    def x_tile(b, i, j):
        n = jnp.minimum(b * nt + i + jnp.minimum(j, 1), g * nt - 1)
        return (n // nt, n % nt, 0)

    def up_tile(j):
        return jnp.where(j < nf, j, 0)

    def down_tile(j):
        return jnp.where(j == 0, nf - 1, j - 1)

    return pl.pallas_call(
        functools.partial(_ffn_kernel, seg_len=seg_len, n_hidden_tiles=nf),
        grid=(g, l // tm, nf + 1),
        in_specs=[pl.BlockSpec((None, tm, d), x_tile),
                  pl.BlockSpec((None, SUBLANES, d), lambda b, i, j: (b, jnp.maximum(i * nb - 1, 0), 0)),
                  pl.BlockSpec((None, SUBLANES, d), lambda b, i, j: (b, jnp.minimum((i + 1) * nb, last), 0)),
                  pl.BlockSpec((None, MOD_ROWS, d), lambda b, i, j: (b, 0, 0)),
                  pl.BlockSpec((1, d), lambda b, i, j: (0, 0)),
                  pl.BlockSpec((None, d, tf), lambda b, i, j: (layer, 0, up_tile(j))),
                  pl.BlockSpec((None, d, tf), lambda b, i, j: (layer, 0, nf + up_tile(j))),
                  pl.BlockSpec((None, 3, 2 * tf), lambda b, i, j: (up_tile(j), 0, 0)),
                  pl.BlockSpec((None, 1, 2 * tf), lambda b, i, j: (up_tile(j), 0, 0)),
                  pl.BlockSpec((None, tf, d), lambda b, i, j: (layer, down_tile(j), 0))],
        out_specs=pl.BlockSpec((None, tm, d), lambda b, i, j: (b, i, 0)),
        out_shape=jax.ShapeDtypeStruct((g, l, d), F32),
        scratch_shapes=[pltpu.VMEM((tm + 2 * FFN_HALO, d), BF16),
                        pltpu.VMEM((tm, tf), BF16), pltpu.VMEM((tm, tf), BF16)],
        compiler_params=_cparams(("arbitrary", "arbitrary", "arbitrary")),
        name="conv_ffn",
    )(x, x, x, mod, gain, w_up, w_up, cw, cb, w_down)


def natten_bias_table(rpb, n_rows):
    kr = min(NA_KR_MAX, n_rows)
    h, n_dr, n_dc = rpb.shape
    col = np.arange(GRID_W)
    col0 = np.clip(col - NA_KC // 2, 0, GRID_W - NA_KC)
    in_win = (col[None, :] >= col0[:, None]) & (col[None, :] < col0[:, None] + NA_KC)
    lo = GRID_W - NA_KC
    ext = jnp.pad(rpb.astype(F32), ((0, 0), (0, 0), (lo, 2 * GRID_W - lo - n_dc)))
    skew = jnp.tile(ext, (1, 1, GRID_W))[:, :, :GRID_W * (2 * GRID_W - 1)]
    skew = skew.reshape(h, n_dr, GRID_W, 2 * GRID_W - 1)[:, :, :, GRID_W - 1:]
    skew = jnp.where(in_win[None, None], skew, -jnp.inf)
    tab = jnp.stack([skew[:, NA_KR_MAX - 1 - s:NA_KR_MAX - 1 - s + kr] for s in range(kr)], axis=1)
    return tab.transpose(0, 1, 3, 2, 4).reshape(h, kr, GRID_W, kr * GRID_W)


NA_ROW_GROUP = 8


def _rms(x, gain):
    return x * lax.rsqrt(jnp.mean(x * x, axis=-1, keepdims=True) + EPS) * gain


def _dot_nt(a, b):
    return lax.dot_general(a, b, (((1,), (1,)), ((), ())), preferred_element_type=F32)


def _natten_kernel(*refs, n_rows, with_ctx_out):
    if with_ctx_out:
        (q_ref, k_ref, v_ref, kc_ref, vc_ref, gq_ref, gk_ref, bias_ref, qc_ref,
         o_ref, oc_ref, qs, ks, vs) = refs
    else:
        q_ref, k_ref, v_ref, kc_ref, vc_ref, gq_ref, gk_ref, bias_ref, o_ref, qs, ks, vs = refs
    dh = q_ref.shape[-1]
    kr = min(NA_KR_MAX, n_rows)
    scale = dh ** -0.5
    qs[...] = (_rms(q_ref[...], gq_ref[...]) * scale).astype(BF16)
    ks[...] = _rms(k_ref[...], gk_ref[...]).astype(BF16)
    vs[...] = v_ref[...].astype(BF16)
    kc = _rms(kc_ref[...], gk_ref[...]).astype(BF16)
    vc = vc_ref[...].astype(BF16)

    grp = NA_ROW_GROUP
    nk = kr * GRID_W

    def rows_group(gi, carry):
        r0 = gi * grp
        q0 = pl.multiple_of(r0 * GRID_W, grp * GRID_W)
        ws = [jnp.clip(r0 + t - kr // 2, 0, n_rows - kr) for t in range(grp)]
        k0 = [pl.multiple_of(w * GRID_W, GRID_W) for w in ws]
        s_loc = [_dot_nt(qs[pl.ds(q0 + t * GRID_W, GRID_W), :], ks[pl.ds(k0[t], nk), :]) for t in range(grp)]
        s_ctx = _dot_nt(qs[pl.ds(q0, grp * GRID_W), :], kc)
        p_loc, p_ctx, den = [], [], []
        for t in range(grp):
            sl = s_loc[t] + bias_ref[r0 + t - ws[t]]
            sc = s_ctx[t * GRID_W:(t + 1) * GRID_W]
            m = jnp.maximum(jnp.max(sl, axis=-1, keepdims=True), jnp.max(sc, axis=-1, keepdims=True))
            pl_t = jnp.exp(sl - m)
            pc_t = jnp.exp(sc - m)
            den.append(jnp.sum(pl_t, axis=-1, keepdims=True) + jnp.sum(pc_t, axis=-1, keepdims=True))
            p_loc.append(pl_t.astype(BF16))
            p_ctx.append(pc_t.astype(BF16))
        o_loc = [jnp.dot(p_loc[t], vs[pl.ds(k0[t], nk), :], preferred_element_type=F32) for t in range(grp)]
        o_ctx = jnp.dot(jnp.concatenate(p_ctx, axis=0), vc, preferred_element_type=F32)
        for t in range(grp):
            o = (o_loc[t] + o_ctx[t * GRID_W:(t + 1) * GRID_W]) / den[t]
            o_ref[pl.ds(q0 + t * GRID_W, GRID_W), :] = o.astype(o_ref.dtype)
        return carry

    lax.fori_loop(0, n_rows // grp, rows_group, 0)

    if with_ctx_out:
        qc = (_rms(qc_ref[...], gq_ref[...]) * scale).astype(BF16)
        s = _dot_nt(qc, kc)
        p = jnp.exp(s - jnp.max(s, axis=-1, keepdims=True))
        o = jnp.dot(p.astype(BF16), vc, preferred_element_type=F32) / jnp.sum(p, axis=-1, keepdims=True)
        oc_ref[...] = o.astype(oc_ref.dtype)


def natten(px, pc, col_q, gq, gk, bias, with_ctx_out):
    bsz, l, _ = px.shape
    lc = pc.shape[1]
    h, kr, _, nk = bias.shape
    dh = gq.shape[-1]
    cb = col_q // dh
    assert (l // GRID_W) % NA_ROW_GROUP == 0

    def head_spec(rows, which):
        return pl.BlockSpec((None, rows, dh), lambda b, hh: (b, 0, cb + which * h + hh))

    in_specs = [head_spec(l, 0), head_spec(l, 1), head_spec(l, 2), head_spec(lc, 1), head_spec(lc, 2),
                pl.BlockSpec((1, dh), lambda b, hh: (0, 0)), pl.BlockSpec((1, dh), lambda b, hh: (0, 0)),
                pl.BlockSpec((None, kr, GRID_W, nk), lambda b, hh: (hh, 0, 0, 0))]
    args = [px, px, px, pc, pc, gq, gk, bias]
    out_specs = [pl.BlockSpec((None, l, dh), lambda b, hh: (b, 0, hh))]
    out_shape = [jax.ShapeDtypeStruct((bsz, l, h * dh), BF16)]
    if with_ctx_out:
        in_specs.append(head_spec(lc, 0))
        args.append(pc)
        out_specs.append(pl.BlockSpec((None, lc, dh), lambda b, hh: (b, 0, hh)))
        out_shape.append(jax.ShapeDtypeStruct((bsz, lc, h * dh), BF16))
    outs = pl.pallas_call(
        functools.partial(_natten_kernel, n_rows=l // GRID_W, with_ctx_out=with_ctx_out),
        grid=(bsz, h),
        in_specs=in_specs, out_specs=out_specs, out_shape=out_shape,
        scratch_shapes=[pltpu.VMEM((l, dh), BF16)] * 3,
        compiler_params=_cparams(("arbitrary", "arbitrary")),
        name="natten",
    )(*args)
    return outs if with_ctx_out else (outs[0], None)


def _cos_sin(n, period):
    k = np.arange(n, dtype=np.int64)
    ang = (2.0 * np.pi / period) * ((k[:, None] * k[None, :]) % period)
    return np.cos(ang), np.sin(ang)


def _fourier_kernel(u_ref, w_ref, cd_ref, sd_ref, cs_ref, o_ref, wc_ref, ws_ref, v_ref, *, scale):
    l, c = u_ref.shape
    dg = cd_ref.shape[0]

    @pl.when(pl.program_id(0) == 0)
    def _():
        wc_ref[...] = jnp.zeros_like(wc_ref)
        ws_ref[...] = jnp.zeros_like(ws_ref)
        for g in range(c // dg):
            sl = slice(g * dg, (g + 1) * dg)
            wg = w_ref[g]
            wc_ref[sl, sl] = jnp.dot(cd_ref[...], wg, precision=lax.Precision.HIGHEST,
                                     preferred_element_type=F32).astype(BF16)
            ws_ref[sl, sl] = jnp.dot(sd_ref[...], wg, precision=lax.Precision.HIGHEST,
                                     preferred_element_type=F32).astype(BF16)

    u = u_ref[...].astype(BF16)
    v_ref[0:l, :] = jnp.dot(u, wc_ref[...], preferred_element_type=F32).astype(BF16)
    v_ref[l:2 * l, :] = jnp.dot(u, ws_ref[...], preferred_element_type=F32).astype(BF16)
    y = jnp.dot(cs_ref[...], v_ref[...], preferred_element_type=F32)
    o_ref[...] = (y * scale).astype(o_ref.dtype)


def fourier_mix(p, col, w):
    bsz, l, _ = p.shape
    g, dg, _ = w.shape
    c = g * dg
    cl, sl = _cos_sin(l, l)
    cd, sd = _cos_sin(dg, dg)
    cs = jnp.asarray(np.concatenate([cl, -sl], axis=1), BF16)
    return pl.pallas_call(
        functools.partial(_fourier_kernel, scale=float((l * dg) ** -0.5)),
        grid=(bsz,),
        in_specs=[pl.BlockSpec((None, l, c), lambda b: (b, 0, col // c)),
                  pl.BlockSpec((g, dg, dg), lambda b: (0, 0, 0)),
                  pl.BlockSpec((dg, dg), lambda b: (0, 0)),
                  pl.BlockSpec((dg, dg), lambda b: (0, 0)),
                  pl.BlockSpec((l, 2 * l), lambda b: (0, 0), pipeline_mode=pl.Buffered(1))],
        out_specs=pl.BlockSpec((None, l, c), lambda b: (b, 0, 0)),
        out_shape=jax.ShapeDtypeStruct((bsz, l, c), BF16),
        scratch_shapes=[pltpu.VMEM((c, c), BF16), pltpu.VMEM((c, c), BF16), pltpu.VMEM((2 * l, c), BF16)],
        compiler_params=_cparams(("arbitrary",)),
        name="fourier_mix",
    )(p, w, jnp.asarray(cd, F32), jnp.asarray(sd, F32), cs)


HY_CBLK = 256
HY_FBLK = 512


def _hyena_dft_matrix(l):
    k = np.arange(l, dtype=np.int64)
    ang = (np.pi / l) * ((k[:, None] * k[None, :]) % (2 * l))
    sn = np.sin(ang)
    sn[0, :] = 1.0 - 2.0 * (k % 2)
    return jnp.asarray(np.concatenate([np.cos(ang), sn], axis=0), BF16)


def _hyena_pos_features(l):
    t = np.linspace(0.0, 1.0, l)[:, None]
    w = (2.0 * np.pi / l) * np.arange(l)[:, None]
    f = np.linspace(1e-4, HY_BANDS - 1, HY_BANDS)[None, :]
    z = np.concatenate([t, np.cos(f * w), -np.sin(f * w)], axis=-1)
    return np.pad(z, ((0, 0), (0, LANES - z.shape[1])))


def _split_bf16(x):
    hi = x.astype(BF16)
    return hi, (x - hi.astype(F32)).astype(BF16)


def _hyena_filter_kernel(z_ref, w1_ref, b1_ref, w2_ref, b2_ref, fr_ref, w3f_ref, w3b_ref, dl_ref, m_ref,
                         p_ref, q_ref):
    l = z_ref.shape[0]
    hp = lax.Precision.HIGHEST
    z = z_ref[...]
    h = jnp.sin(fr_ref[0:1, :] * (jnp.dot(z, w1_ref[...], precision=hp, preferred_element_type=F32)
                                  + b1_ref[...]))
    h = jnp.sin(fr_ref[1:2, :] * (jnp.dot(h, w2_ref[...], precision=hp, preferred_element_type=F32)
                                  + b2_ref[...]))
    window = jnp.exp(-z[:, 0:1] * dl_ref[...]) + HY_DECAY_SHIFT
    hf = jnp.dot(h, w3f_ref[...], precision=hp, preferred_element_type=F32) * window
    hb = jnp.dot(h, w3b_ref[...], precision=hp, preferred_element_type=F32) * window
    norm = jnp.sum(jnp.abs(hf) + jnp.abs(hb), axis=0, keepdims=True) + EPS
    hf = hf / norm
    hb = hb / norm
    g1h, g1l = _split_bf16(hf + hb)
    g2h, g2l = _split_bf16(hb - hf)
    f1 = (jnp.dot(m_ref[...], g1h, preferred_element_type=F32)
          + jnp.dot(m_ref[...], g1l, preferred_element_type=F32))
    f2 = (jnp.dot(m_ref[l:2 * l, :], g2h, preferred_element_type=F32)
          + jnp.dot(m_ref[l:2 * l, :], g2l, preferred_element_type=F32))
    p_ref[...] = f1[0:l]
    row = lax.broadcasted_iota(jnp.int32, f2.shape, 0)
    q_ref[...] = jnp.where(row == 0, f1[l:l + 1], f2)


def hyena_filter_spectrum(l, w1, b1, w2, b2, w3, freq, m):
    c = w3.shape[1] // 2
    hid = w1.shape[1]
    z = jnp.asarray(_hyena_pos_features(l), F32)
    w1p = jnp.pad(w1, ((0, z.shape[1] - w1.shape[0]), (0, 0)))
    deltas = np.abs(np.linspace(math.log(HY_DECAY_TARGET) / HY_SLOW_PCT,
                                math.log(HY_DECAY_TARGET) / HY_FAST_PCT, c))[None, :]
    nb = c // HY_CBLK
    full = lambda shape: pl.BlockSpec(shape, lambda j: (0,) * len(shape))
    return pl.pallas_call(
        _hyena_filter_kernel,
        grid=(nb,),
        in_specs=[full(z.shape), full(w1p.shape), full((1, hid)), full(w2.shape), full((1, hid)),
                  full((2, hid)),
                  pl.BlockSpec((hid, HY_CBLK), lambda j: (0, j)),
                  pl.BlockSpec((hid, HY_CBLK), lambda j: (0, nb + j)),
                  pl.BlockSpec((1, HY_CBLK), lambda j: (0, j)),
                  pl.BlockSpec(m.shape, lambda j: (0, 0), pipeline_mode=pl.Buffered(1))],
        out_specs=[pl.BlockSpec((l, HY_CBLK), lambda j: (0, j))] * 2,
        out_shape=[jax.ShapeDtypeStruct((l, c), F32)] * 2,
        compiler_params=_cparams(("arbitrary",)),
        name="hyena_filter",
    )(z, w1p, b1.reshape(1, hid), w2, b2.reshape(1, hid), freq, w3, w3, jnp.asarray(deltas, F32), m)


def _shift_rows(u, down):
    l = u.shape[0]
    row = lax.broadcasted_iota(jnp.int32, u.shape, 0)
    if down:
        return jnp.where(row == 0, 0.0, pltpu.roll(u, 1, 0))
    return jnp.where(row == l - 1, 0.0, pltpu.roll(u, l - 1, 0))


def _dwconv3(u, taps):
    return (taps[3:4] + _shift_rows(u, True) * taps[0:1] + u * taps[1:2] + _shift_rows(u, False) * taps[2:3])


def _hyena_conv_kernel(x0_ref, x1_ref, v_ref, taps_ref, bias_ref, p_ref, q_ref, m_ref, o_ref):
    l, c = x0_ref.shape
    x1 = _dwconv3(x1_ref[...], taps_ref[1])
    s = _dwconv3(v_ref[...], taps_ref[2]) * x1
    sb = s.astype(BF16)
    fb = min(HY_FBLK, l)
    nblk = l // fb
    inv_l = 1.0 / l

    def forward(i):
        return (jnp.dot(m_ref[i * fb:(i + 1) * fb, :], sb, preferred_element_type=F32),
                jnp.dot(m_ref[l + i * fb:l + (i + 1) * fb, :], sb, preferred_element_type=F32))

    y_cos_acc = y_sin_acc = nyquist = None
    ab = forward(0)
    x0 = _dwconv3(x0_ref[...], taps_ref[0])
    for i in range(nblk):
        a, b = ab
        if i + 1 < nblk:
            ab = forward(i + 1)
        p, q = p_ref[i * fb:(i + 1) * fb, :], q_ref[i * fb:(i + 1) * fb, :]
        bq = b * q
        if i == 0:
            first = lax.broadcasted_iota(jnp.int32, (fb, c), 0) == 0
            nyquist = bq[0:1] * (0.5 * inv_l)
            y_cos = (a * p + jnp.where(first, 0.0, bq)) * jnp.where(first, 0.5 * inv_l, inv_l)
            y_sin = jnp.where(first, 0.0, (b * p - a * q) * inv_l)
        else:
            y_cos = (a * p + bq) * inv_l
            y_sin = (b * p - a * q) * inv_l
        d_cos = jnp.dot(m_ref[0:l, i * fb:(i + 1) * fb], y_cos.astype(BF16), preferred_element_type=F32)
        d_sin = jnp.dot(m_ref[l:2 * l, i * fb:(i + 1) * fb], y_sin.astype(BF16), preferred_element_type=F32)
        y_cos_acc = d_cos if y_cos_acc is None else y_cos_acc + d_cos
        y_sin_acc = d_sin if y_sin_acc is None else y_sin_acc + d_sin
    row = lax.broadcasted_iota(jnp.int32, (l, c), 0)
    y = (y_cos_acc + jnp.where(row == 0, 0.0, y_sin_acc)
         + jnp.where(row % 2 == 0, 1.0, -1.0) * nyquist)
    o_ref[...] = ((y + s * bias_ref[...]) * x0).astype(o_ref.dtype)


def hyena_conv(p, col, conv_w, conv_b, bias, spec_p, spec_q, m):
    bsz, l, _ = p.shape
    c = bias.shape[-1]
    nb = c // HY_CBLK
    cb0 = col // HY_CBLK
    taps = jnp.concatenate([conv_w, conv_b[None]], axis=0)
    taps = taps.reshape(4, 3, nb, HY_CBLK).transpose(2, 1, 0, 3)

    def part(k):
        return pl.BlockSpec((None, l, HY_CBLK), lambda j, b: (b, 0, cb0 + k * nb + j))

    return pl.pallas_call(
        _hyena_conv_kernel,
        grid=(nb, bsz),
        in_specs=[part(0), part(1), part(2),
                  pl.BlockSpec((None, 3, 4, HY_CBLK), lambda j, b: (j, 0, 0, 0)),
                  pl.BlockSpec((1, HY_CBLK), lambda j, b: (0, j)),
                  pl.BlockSpec((l, HY_CBLK), lambda j, b: (0, j)),
                  pl.BlockSpec((l, HY_CBLK), lambda j, b: (0, j)),
                  pl.BlockSpec(m.shape, lambda j, b: (0, 0), pipeline_mode=pl.Buffered(1))],
        out_specs=pl.BlockSpec((None, l, HY_CBLK), lambda j, b: (b, 0, j)),
        out_shape=jax.ShapeDtypeStruct((bsz, l, c), BF16),
        compiler_params=_cparams(("arbitrary", "arbitrary")),
        name="hyena_conv",
    )(p, p, p, taps, bias.reshape(1, c), spec_p, spec_q, m)


GLA_HP = 2
GLA_ROPE_PAIR = 16
GLA_SCAN_UNROLL = 4


def _gla_rope_tables(l, dk):
    half = dk // 2
    nf = half // 2
    assert nf == GLA_ROPE_PAIR
    inv = ROPE_THETA ** (-np.arange(nf, dtype=np.float64) / nf)
    t = np.arange(l)
    ang_r = (t // GRID_W)[:, None] * inv
    ang_c = (t % GRID_W)[:, None] * inv
    cos = np.concatenate([np.cos(ang_r)] * 2 + [np.cos(ang_c)] * 2, axis=1)
    sin = np.concatenate([-np.sin(ang_r), np.sin(ang_r), -np.sin(ang_c), np.sin(ang_c)], axis=1)
    return (jnp.asarray(np.tile(cos, (1, GLA_HP)), F32), jnp.asarray(np.tile(sin, (1, GLA_HP)), F32))


def _rope(x, cos, sin):
    lane = lax.broadcasted_iota(jnp.int32, x.shape, 1)
    lanes = x.shape[1]
    partner = jnp.where(lane % (2 * GLA_ROPE_PAIR) < GLA_ROPE_PAIR,
                        pltpu.roll(x, lanes - GLA_ROPE_PAIR, 1), pltpu.roll(x, GLA_ROPE_PAIR, 1))
    return x * cos + partner * sin


def _log_sigmoid(x):
    return jnp.minimum(x, 0.0) - jnp.log(1.0 + jnp.exp(-jnp.abs(x)))


def _gla_kernel(*refs, with_ctx_out):
    (q_ref, k_ref, v_ref, r_ref, z_ref, cq_ref, ck_ref, cv_ref, cr_ref, cz_ref,
     wz_ref, bz_ref, g_ref, cos_ref, sin_ref) = refs[:15]
    if with_ctx_out:
        o_ref, oc_ref = refs[15:17]
        scratch = refs[17:]
    else:
        o_ref, oc_ref = refs[15], None
        scratch = refs[16:]
    qs, ks, las, ofs, obs, cqs, cks, clas, cofs, cobs, st_f, st_b = scratch
    dk2 = q_ref.shape[1]
    dv2 = v_ref.shape[1]
    dk, dv = dk2 // GLA_HP, dv2 // GLA_HP
    ch = GLA_CHUNK
    hp = lax.Precision.HIGHEST

    def gates(z):
        pre = jnp.dot(z, wz_ref[...], precision=hp, preferred_element_type=F32) + bz_ref[...]
        return _log_sigmoid(pre) * (1.0 / GLA_TAU)

    qs[...] = _rope(q_ref[...] * dk ** -0.5, cos_ref[...], sin_ref[...])
    ks[...] = _rope(k_ref[...], cos_ref[...], sin_ref[...])
    las[...] = gates(z_ref[...])
    cqs[...] = cq_ref[...] * dk ** -0.5
    cks[...] = ck_ref[...]
    clas[...] = gates(cz_ref[...])
    st_f[...] = jnp.zeros_like(st_f)
    st_b[...] = jnp.zeros_like(st_b)

    ri = lax.broadcasted_iota(jnp.int32, (ch, ch), 0)
    ci = lax.broadcasted_iota(jnp.int32, (ch, ch), 1)
    tri = {False: ri >= ci, True: ri <= ci}
    tri_b16 = {d: jnp.where(m, 1.0, 0.0).astype(BF16) for d, m in tri.items()}
    tri2 = {d: jnp.concatenate([m] * GLA_HP, axis=0) for d, m in tri.items()}
    lane_head = lax.broadcasted_iota(jnp.int32, (ch, dk2), 1) // dk
    st_r = lax.broadcasted_iota(jnp.int32, (dv2, dk2), 0) // dv
    st_c = lax.broadcasted_iota(jnp.int32, (dv2, dk2), 1) // dk
    st_diag = st_r == st_c

    def scan(q_s, k_s, la_s, v_in, of_s, ob_s, n):
        unroll = GLA_SCAN_UNROLL

        def body(trip, carry):
            items = []
            for u in range(unroll):
                c = trip * unroll + u
                items.append((False, pl.ds(pl.multiple_of(c * ch, ch), ch)))
                items.append((True, pl.ds(pl.multiple_of((n - 1 - c) * ch, ch), ch)))
            v_c = [v_in[rows, :].astype(BF16) for _, rows in items]
            cum = []
            for bw, rows in items:
                la_hi, la_lo = _split_bf16(la_s[rows, dk2:2 * dk2] if bw else la_s[rows, 0:dk2])
                cum.append(jnp.dot(tri_b16[bw], la_hi, preferred_element_type=F32)
                           + jnp.dot(tri_b16[bw], la_lo, preferred_element_type=F32))
            q_dec, k_end, decay, sc = [], [], [], []
            for i, (bw, rows) in enumerate(items):
                q_c, k_c = q_s[rows, :], k_s[rows, :]
                tot = cum[i][0:1] if bw else cum[i][ch - 1:ch]
                qd = q_c * jnp.exp(cum[i])
                k_inv = (k_c * jnp.exp(-cum[i])).astype(BF16)
                k_end.append((k_c * jnp.exp(tot - cum[i])).astype(BF16))
                decay.append(jnp.exp(tot))
                q_heads = jnp.concatenate([jnp.where(lane_head == h, qd, 0.0) for h in range(GLA_HP)], axis=0)
                sc.append(_dot_nt(q_heads.astype(BF16), k_inv))
                q_dec.append(qd.astype(BF16))
            o_intra, ds_t = [], []
            for i, (bw, rows) in enumerate(items):
                pv = jnp.dot(jnp.where(tri2[bw], sc[i], 0.0).astype(BF16), v_c[i],
                             preferred_element_type=F32)
                o_intra.append(jnp.concatenate(
                    [pv[h * ch:(h + 1) * ch, h * dv:(h + 1) * dv] for h in range(GLA_HP)], axis=1))
                ds_t.append(lax.dot_general(v_c[i], k_end[i], (((0,), (0,)), ((), ())),
                                            preferred_element_type=F32))
            for i, (bw, rows) in enumerate(items):
                st, o_s = (st_b, ob_s) if bw else (st_f, of_s)
                s_t = st[...]
                o_s[rows, :] = o_intra[i] + _dot_nt(q_dec[i], s_t.astype(BF16))
                st[...] = s_t * decay[i] + jnp.where(st_diag, ds_t[i], 0.0)
            return carry

        lax.fori_loop(0, n // unroll, body, 0)

    def finish(of_s, ob_s, gate_ref, out_ref, n_blocks, blk):
        def body(i, carry):
            rows = pl.ds(pl.multiple_of(i * blk, blk), blk)
            o = of_s[rows, :] + ob_s[rows, :]
            gate = gate_ref[rows, :]
            parts = []
            for h in range(GLA_HP):
                oh = o[:, h * dv:(h + 1) * dv]
                gh = gate[:, h * dv:(h + 1) * dv]
                parts.append(_rms(oh, g_ref[...]) * (gh * jax.nn.sigmoid(gh)))
            out_ref[rows, :] = jnp.concatenate(parts, axis=1).astype(out_ref.dtype)
            return carry
        lax.fori_loop(0, n_blocks, body, 0)

    l, lc = q_ref.shape[0], cq_ref.shape[0]
    scan(cqs, cks, clas, cv_ref, cofs, cobs, lc // ch)
    if with_ctx_out:
        finish(cofs, cobs, cr_ref, oc_ref, 1, lc)
    scan(qs, ks, las, v_ref, ofs, obs, l // ch)
    finish(ofs, obs, r_ref, o_ref, l // lc, lc)


def gla(px, pxz, pc, pcz, w_gate, b_gate, out_gain, with_ctx_out):
    bsz, l, _ = px.shape
    lc = pc.shape[1]
    dv = out_gain.shape[-1]
    hdk = w_gate.shape[-1]
    dk = hdk // GLA_HEADS
    dk2, dv2 = GLA_HP * dk, GLA_HP * dv
    nhp = GLA_HEADS // GLA_HP
    zw = pxz.shape[-1]
    rank = w_gate.shape[1]
    wz = jnp.zeros((nhp, zw, 2 * dk2), F32)
    for u in range(2):
        blk = w_gate[u].reshape(rank, nhp, dk2).transpose(1, 0, 2)
        wz = wz.at[:, u * rank:(u + 1) * rank, u * dk2:(u + 1) * dk2].set(blk)
    bz = b_gate.reshape(2, nhp, dk2).transpose(1, 0, 2).reshape(nhp, 1, 2 * dk2)
    cos, sin = _gla_rope_tables(l, dk)
    k_cb, v_cb, r_cb = hdk // dk2, 2 * hdk // dv2, (2 * hdk + GLA_HEADS * dv) // dv2

    def col(rows, width, cb):
        return pl.BlockSpec((None, rows, width), lambda b, j: (b, 0, cb + j))

    def whole(rows, width):
        return pl.BlockSpec((None, rows, width), lambda b, j: (b, 0, 0))

    const = lambda shape: pl.BlockSpec(shape, lambda b, j: (0,) * len(shape))
    in_specs = [col(l, dk2, 0), col(l, dk2, k_cb), col(l, dv2, v_cb), col(l, dv2, r_cb), whole(l, zw),
                col(lc, dk2, 0), col(lc, dk2, k_cb), col(lc, dv2, v_cb), col(lc, dv2, r_cb), whole(lc, zw),
                pl.BlockSpec((None, zw, 2 * dk2), lambda b, j: (j, 0, 0)),
                pl.BlockSpec((None, 1, 2 * dk2), lambda b, j: (j, 0, 0)),
                const((1, dv)), const((l, dk2)), const((l, dk2))]
    out_specs = [pl.BlockSpec((None, l, dv2), lambda b, j: (b, 0, j))]
    out_shape = [jax.ShapeDtypeStruct((bsz, l, GLA_HEADS * dv), BF16)]
    if with_ctx_out:
        out_specs.append(pl.BlockSpec((None, lc, dv2), lambda b, j: (b, 0, j)))
        out_shape.append(jax.ShapeDtypeStruct((bsz, lc, GLA_HEADS * dv), BF16))
    scratch = [pltpu.VMEM((l, dk2), F32), pltpu.VMEM((l, dk2), F32), pltpu.VMEM((l, 2 * dk2), F32),
               pltpu.VMEM((l, dv2), F32), pltpu.VMEM((l, dv2), F32),
               pltpu.VMEM((lc, dk2), F32), pltpu.VMEM((lc, dk2), F32), pltpu.VMEM((lc, 2 * dk2), F32),
               pltpu.VMEM((lc, dv2), F32), pltpu.VMEM((lc, dv2), F32),
               pltpu.VMEM((dv2, dk2), F32), pltpu.VMEM((dv2, dk2), F32)]
    outs = pl.pallas_call(
        functools.partial(_gla_kernel, with_ctx_out=with_ctx_out),
        grid=(bsz, nhp),
        in_specs=in_specs, out_specs=out_specs, out_shape=out_shape, scratch_shapes=scratch,
        compiler_params=_cparams(("arbitrary", "arbitrary")),
        name="gla",
    )(px, px, px, px, pxz, pc, pc, pc, pc, pcz, wz, bz, out_gain.reshape(1, dv), cos, sin)
    return outs if with_ctx_out else (outs[0], None)


def kernel(x, c, ctx, c_ctx, w_mod, b_mod, g_mix, w_in, gla_gate_w, gla_gate_b, gla_out_g,
           hy_conv_w, hy_conv_b, hy_w1, hy_b1, hy_w2, hy_b2, hy_w3, hy_freq, hy_bias, fn_w,
           na_q_g, na_k_g, na_rpb, w_out, g_ffn, ffn_w_up, ffn_conv_w, ffn_conv_b, ffn_w_down):
    bsz, l_lat, d = x.shape
    l_ctx = ctx.shape[1]
    depth = w_mod.shape[0]
    w = d // 4

    n_cond = -(-(bsz + 1) // SUBLANES) * SUBLANES
    cond = jnp.zeros((n_cond, d), F32).at[:bsz].set(c).at[bsz].set(c_ctx)
    mods = mod_vectors(cond, w_mod, b_mod).reshape(depth, n_cond, N_MOD, d)
    mods = jnp.pad(mods, ((0, 0), (0, 0), (0, MOD_ROWS - N_MOD), (0, 0)))

    z0 = 3 * w
    zw = 2 * GLA_GATE_RANK
    w_main = jnp.concatenate([w_in[:, :, :z0], w_in[:, :, z0 + zw:]], axis=-1).astype(BF16)
    w_z = jnp.pad(w_in[:, :, z0:z0 + zw], ((0, 0), (0, 0), (0, LANES - zw))).astype(BF16)
    w_out_b = w_out.astype(BF16)
    w_up_b = ffn_w_up.astype(BF16)
    w_down_b = ffn_w_down.astype(BF16)

    m_lat = _hyena_dft_matrix(l_lat)
    m_ctx = _hyena_dft_matrix(l_ctx)
    ctx_flat = ctx.reshape(1, bsz * l_ctx, d)
    for layer in range(depth):
        last = layer == depth - 1
        mod_x = mods[layer, :bsz]
        mod_c = mods[layer, bsz:bsz + 1]
        g_mix_l = g_mix[layer].reshape(1, d)
        g_ffn_l = g_ffn[layer].reshape(1, d)

        px, pxz = in_proj(x, mod_x, g_mix_l, w_main, w_z, layer, tm=1024)
        pc, pcz = in_proj(ctx_flat, mod_c, g_mix_l, w_main, w_z, layer, tm=1024)
        pc = pc.reshape(bsz, l_ctx, -1)
        pcz = pcz.reshape(bsz, l_ctx, -1)

        y_a, yc_a = gla(px, pxz, pc, pcz, gla_gate_w[layer], gla_gate_b[layer], gla_out_g[layer],
                        with_ctx_out=not last)
        hy_filt = (hy_w1[layer], hy_b1[layer], hy_w2[layer], hy_b2[layer], hy_w3[layer], hy_freq[layer])
        hy_p, hy_q = hyena_filter_spectrum(l_lat, *hy_filt, m_lat)
        y_b = hyena_conv(px, 3 * w, hy_conv_w[layer], hy_conv_b[layer], hy_bias[layer], hy_p, hy_q, m_lat)
        y_c = fourier_mix(px, 6 * w, fn_w[layer])
        na_bias = natten_bias_table(na_rpb[layer], l_lat // GRID_W)
        y_d, yc_d = natten(px, pc, 7 * w, na_q_g[layer].reshape(1, -1), na_k_g[layer].reshape(1, -1),
                           na_bias, with_ctx_out=not last)
        x = out_proj([y_a, y_b, y_c, y_d], w_out_b, layer, x, mod_x, tm=1024)
        cw = ffn_conv_w[layer]
        cb = ffn_conv_b[layer].reshape(1, -1)
        x = conv_ffn(x, mod_x, g_ffn_l, w_up_b, cw, cb, w_down_b, layer, tm=1024)

        if not last:
            hc_p, hc_q = hyena_filter_spectrum(l_ctx, *hy_filt, m_ctx)
            yc_b = hyena_conv(pc, 3 * w, hy_conv_w[layer], hy_conv_b[layer], hy_bias[layer], hc_p, hc_q, m_ctx)
            yc_c = fourier_mix(pc, 6 * w, fn_w[layer])
            ycs = [y.reshape(1, bsz * l_ctx, w) for y in (yc_a, yc_b, yc_c, yc_d)]
            ctx_flat = out_proj(ycs, w_out_b, layer, ctx_flat, mod_c, tm=1024)
            ctx_flat = conv_ffn(ctx_flat, mod_c, g_ffn_l, w_up_b, cw, cb, w_down_b, layer, tm=1024,
                                seg_len=l_ctx)
    return x
```

```python
import functools
import math

import numpy as np
import jax
import jax.numpy as jnp
from jax import lax
from jax.experimental import pallas as pl
from jax.experimental.pallas import tpu as pltpu

F32 = jnp.float32
BF16 = jnp.bfloat16

GRID_W = 64
GLA_HEADS = 4
GLA_GATE_RANK = 16
GLA_TAU = 16.0
GLA_CHUNK = 64
HY_BANDS = 16
HY_DECAY_TARGET = 1e-2
HY_FAST_PCT = 0.3
HY_SLOW_PCT = 1.5
HY_DECAY_SHIFT = 0.05
FN_GROUPS = 4
NA_HEADS = 4
NA_KR_MAX = 8
NA_KC = 16
ROPE_THETA = 10000.0
N_MOD = 6
EPS = 1e-6

V7X_VMEM_LIMIT = 58 * 1024 * 1024
SUBLANES = 8
LANES = 128
MOD_ROWS = 8
FFN_HALO = 16


def _cparams(sem):
    return pltpu.CompilerParams(dimension_semantics=sem, vmem_limit_bytes=V7X_VMEM_LIMIT)


def _mod_kernel(s_ref, w_ref, b_ref, o_ref):
    s = s_ref[...]
    s = s * jax.nn.sigmoid(s)
    o_ref[...] = jnp.dot(s.astype(BF16), w_ref[...].astype(BF16),
                         preferred_element_type=F32) + b_ref[...]


def mod_vectors(cond, w_mod, b_mod, tn=1024):
    depth, d, n = w_mod.shape
    r = cond.shape[0]
    return pl.pallas_call(
        _mod_kernel,
        grid=(depth, n // tn),
        in_specs=[pl.BlockSpec((r, d), lambda l, j: (0, 0)),
                  pl.BlockSpec((None, d, tn), lambda l, j: (l, 0, j)),
                  pl.BlockSpec((None, 1, tn), lambda l, j: (l, 0, j))],
        out_specs=pl.BlockSpec((None, r, tn), lambda l, j: (l, 0, j)),
        out_shape=jax.ShapeDtypeStruct((depth, r, n), F32),
        compiler_params=_cparams(("arbitrary", "arbitrary")),
        name="mod_vectors",
    )(cond, w_mod, b_mod.reshape(depth, 1, n))


def _norm_mod(x, gain, shift, scale):
    ms = jnp.mean(x * x, axis=-1, keepdims=True)
    return (x * lax.rsqrt(ms + EPS) * gain) * (1.0 + scale) + shift


def _inproj_kernel(x_ref, mod_ref, g_ref, w_ref, wz_ref, o_ref, oz_ref, h_ref):
    j = pl.program_id(2)
    tn = o_ref.shape[1]

    @pl.when(j == 0)
    def _():
        h = _norm_mod(x_ref[...], g_ref[...], mod_ref[0:1, :], mod_ref[1:2, :]).astype(BF16)
        h_ref[...] = h
        oz_ref[...] = jnp.dot(h, wz_ref[...], preferred_element_type=F32)

    w = w_ref[:, pl.ds(pl.multiple_of(j * tn, tn), tn)]
    o_ref[...] = jnp.dot(h_ref[...], w, preferred_element_type=F32)


def in_proj(x, mod, gain, w_main, w_z, layer, tm, tn=1024):
    g, l, d = x.shape
    n = w_main.shape[-1]
    nz = w_z.shape[-1]
    nt = l // tm

    def x_tile(b, i, j):
        t = jnp.minimum(b * nt + i + jnp.minimum(j, 1), g * nt - 1)
        return (t // nt, t % nt, 0)

    return pl.pallas_call(
        _inproj_kernel,
        grid=(g, l // tm, n // tn),
        in_specs=[pl.BlockSpec((None, tm, d), x_tile),
                  pl.BlockSpec((None, MOD_ROWS, d), lambda b, i, j: (b, 0, 0)),
                  pl.BlockSpec((1, d), lambda b, i, j: (0, 0)),
                  pl.BlockSpec((None, d, n), lambda b, i, j: (layer, 0, 0), pipeline_mode=pl.Buffered(1)),
                  pl.BlockSpec((None, d, nz), lambda b, i, j: (layer, 0, 0))],
        out_specs=[pl.BlockSpec((None, tm, tn), lambda b, i, j: (b, i, j)),
                   pl.BlockSpec((None, tm, nz), lambda b, i, j: (b, i, 0))],
        out_shape=[jax.ShapeDtypeStruct((g, l, n), F32),
                   jax.ShapeDtypeStruct((g, l, nz), F32)],
        scratch_shapes=[pltpu.VMEM((tm, d), BF16)],
        compiler_params=_cparams(("arbitrary", "arbitrary", "arbitrary")),
        name="in_proj",
    )(x, mod, gain, w_main, w_z)


def _outproj_kernel(ya_ref, yb_ref, yc_ref, yd_ref, w_ref, x_ref, mod_ref, o_ref):
    kw = ya_ref.shape[-1]
    acc = jnp.dot(ya_ref[...], w_ref[0 * kw:1 * kw, :], preferred_element_type=F32)
    acc += jnp.dot(yb_ref[...], w_ref[1 * kw:2 * kw, :], preferred_element_type=F32)
    acc += jnp.dot(yc_ref[...], w_ref[2 * kw:3 * kw, :], preferred_element_type=F32)
    acc += jnp.dot(yd_ref[...], w_ref[3 * kw:4 * kw, :], preferred_element_type=F32)
    o_ref[...] = x_ref[...] + mod_ref[2:3, :] * acc


def out_proj(ys, w_out, layer, x, mod, tm):
    g, l, d = x.shape
    kw = ys[0].shape[-1]
    yspec = pl.BlockSpec((None, tm, kw), lambda b, i: (b, i, 0))
    return pl.pallas_call(
        _outproj_kernel,
        grid=(g, l // tm),
        in_specs=[yspec, yspec, yspec, yspec,
                  pl.BlockSpec((None, 4 * kw, d), lambda b, i: (layer, 0, 0), pipeline_mode=pl.Buffered(1)),
                  pl.BlockSpec((None, tm, d), lambda b, i: (b, i, 0)),
                  pl.BlockSpec((None, MOD_ROWS, d), lambda b, i: (b, 0, 0))],
        out_specs=pl.BlockSpec((None, tm, d), lambda b, i: (b, i, 0)),
        out_shape=jax.ShapeDtypeStruct((g, l, d), F32),
        compiler_params=_cparams(("arbitrary", "arbitrary")),
        name="out_proj",
    )(*ys, w_out, x, mod)


def _ffn_kernel(x_ref, xp_ref, xn_ref, mod_ref, g_ref, wa_ref, wg_ref, cw_ref, cb_ref, wd_ref,
                o_ref, h_ref, act0_ref, act1_ref, *, seg_len, n_hidden_tiles):
    i = pl.program_id(1)
    j = pl.program_id(2)
    nf = n_hidden_tiles
    acts = (act0_ref, act1_ref)
    tm = x_ref.shape[0]
    tf = wa_ref.shape[1]
    hs = FFN_HALO

    def prologue():
        gain, shift, scale = g_ref[...], mod_ref[3:4, :], mod_ref[4:5, :]
        h_ref[hs:hs + tm, :] = _norm_mod(x_ref[...], gain, shift, scale).astype(BF16)
        hp = jnp.where(i > 0, _norm_mod(xp_ref[...], gain, shift, scale), 0.0)
        hn = jnp.where(i < pl.num_programs(1) - 1, _norm_mod(xn_ref[...], gain, shift, scale), 0.0)
        zero = jnp.zeros_like(hp)
        h_ref[0:hs, :] = jnp.concatenate([zero, hp], axis=0).astype(BF16)
        h_ref[hs + tm:2 * hs + tm, :] = jnp.concatenate([hn, zero], axis=0).astype(BF16)
        o_ref[...] = x_ref[...]

    def conv(u, c0, width):
        w = cw_ref[:, c0:c0 + width]
        prev, nxt = u[hs - 1:hs - 1 + tm], u[hs + 1:hs + 1 + tm]
        if seg_len is not None:
            pos = lax.broadcasted_iota(jnp.int32, prev.shape, 0) % seg_len
            prev = jnp.where(pos == 0, 0.0, prev)
            nxt = jnp.where(pos == seg_len - 1, 0.0, nxt)
        return cb_ref[:, c0:c0 + width] + prev * w[0:1] + u[hs:hs + tm] * w[1:2] + nxt * w[2:3]

    def up_matmuls():
        h = h_ref[...]
        ug = jnp.dot(h, wg_ref[...], preferred_element_type=F32)
        ua = jnp.dot(h, wa_ref[...], preferred_element_type=F32)
        return ua, ug

    def gate_to(act_ref, ua, ug):
        gt = conv(ug, tf, tf)
        swish = gt * jax.nn.sigmoid(gt)
        act_ref[...] = (conv(ua, 0, tf) * swish).astype(BF16)

    def down_from(act_ref):
        o_ref[...] += mod_ref[5:6, :] * jnp.dot(act_ref[...], wd_ref[...], preferred_element_type=F32)

    middle = jnp.logical_and(j > 0, j < nf)

    @pl.when(j == 0)
    def _():
        prologue()
        ua, ug = up_matmuls()
        gate_to(acts[0], ua, ug)

    for parity in range(2):
        @pl.when(jnp.logical_and(middle, j % 2 == parity))
        def _():
            ua, ug = up_matmuls()
            down_from(acts[1 - parity])
            gate_to(acts[parity], ua, ug)

    @pl.when(j == nf)
    def _():
        down_from(acts[(n_hidden_tiles - 1) % 2])


def conv_ffn(x, mod, gain, w_up, conv_w, conv_b, w_down, layer, tm, tf=512, seg_len=None):
    g, l, d = x.shape
    assert seg_len is None or (tm % seg_len == 0 and l % tm == 0)
    f = w_down.shape[1]
    nf = f // tf
    nb = tm // SUBLANES
    last = l // SUBLANES - 1
    cw = conv_w.reshape(3, 2, nf, tf).transpose(2, 0, 1, 3).reshape(nf, 3, 2 * tf)
    cb = conv_b.reshape(1, 2, nf, tf).transpose(2, 0, 1, 3).reshape(nf, 1, 2 * tf)
    nt = l // tm

    def x_tile(b, i, j):
        n = jnp.minimum(b * nt + i + jnp.minimum(j, 1), g * nt - 1)
        return (n // nt, n % nt, 0)

    def up_tile(j):
        return jnp.where(j < nf, j, 0)

    def down_tile(j):
        return jnp.where(j == 0, nf - 1, j - 1)

    return pl.pallas_call(
        functools.partial(_ffn_kernel, seg_len=seg_len, n_hidden_tiles=nf),
        grid=(g, l // tm, nf + 1),
        in_specs=[pl.BlockSpec((None, tm, d), x_tile),
                  pl.BlockSpec((None, SUBLANES, d), lambda b, i, j: (b, jnp.maximum(i * nb - 1, 0), 0)),
                  pl.BlockSpec((None, SUBLANES, d), lambda b, i, j: (b, jnp.minimum((i + 1) * nb, last), 0)),
                  pl.BlockSpec((None, MOD_ROWS, d), lambda b, i, j: (b, 0, 0)),
                  pl.BlockSpec((1, d), lambda b, i, j: (0, 0)),
                  pl.BlockSpec((None, d, tf), lambda b, i, j: (layer, 0, up_tile(j))),
                  pl.BlockSpec((None, d, tf), lambda b, i, j: (layer, 0, nf + up_tile(j))),
                  pl.BlockSpec((None, 3, 2 * tf), lambda b, i, j: (up_tile(j), 0, 0)),
                  pl.BlockSpec((None, 1, 2 * tf), lambda b, i, j: (up_tile(j), 0, 0)),
                  pl.BlockSpec((None, tf, d), lambda b, i, j: (layer, down_tile(j), 0))],
        out_specs=pl.BlockSpec((None, tm, d), lambda b, i, j: (b, i, 0)),
        out_shape=jax.ShapeDtypeStruct((g, l, d), F32),
        scratch_shapes=[pltpu.VMEM((tm + 2 * FFN_HALO, d), BF16),
                        pltpu.VMEM((tm, tf), BF16), pltpu.VMEM((tm, tf), BF16)],
        compiler_params=_cparams(("arbitrary", "arbitrary", "arbitrary")),
        name="conv_ffn",
    )(x, x, x, mod, gain, w_up, w_up, cw, cb, w_down)


def natten_bias_table(rpb, n_rows):
    kr = min(NA_KR_MAX, n_rows)
    h, n_dr, n_dc = rpb.shape
    col = np.arange(GRID_W)
    col0 = np.clip(col - NA_KC // 2, 0, GRID_W - NA_KC)
    in_win = (col[None, :] >= col0[:, None]) & (col[None, :] < col0[:, None] + NA_KC)
    lo = GRID_W - NA_KC
    ext = jnp.pad(rpb.astype(F32), ((0, 0), (0, 0), (lo, 2 * GRID_W - lo - n_dc)))
    skew = jnp.tile(ext, (1, 1, GRID_W))[:, :, :GRID_W * (2 * GRID_W - 1)]
    skew = skew.reshape(h, n_dr, GRID_W, 2 * GRID_W - 1)[:, :, :, GRID_W - 1:]
    skew = jnp.where(in_win[None, None], skew, -jnp.inf)
    tab = jnp.stack([skew[:, NA_KR_MAX - 1 - s:NA_KR_MAX - 1 - s + kr] for s in range(kr)], axis=1)
    return tab.transpose(0, 1, 3, 2, 4).reshape(h, kr, GRID_W, kr * GRID_W)


NA_ROW_GROUP = 8


def _rms(x, gain):
    return x * lax.rsqrt(jnp.mean(x * x, axis=-1, keepdims=True) + EPS) * gain


def _dot_nt(a, b):
    return lax.dot_general(a, b, (((1,), (1,)), ((), ())), preferred_element_type=F32)


def _natten_kernel(*refs, n_rows, with_ctx_out):
    if with_ctx_out:
        (q_ref, k_ref, v_ref, kc_ref, vc_ref, gq_ref, gk_ref, bias_ref, qc_ref,
         o_ref, oc_ref, qs, ks, vs) = refs
    else:
        q_ref, k_ref, v_ref, kc_ref, vc_ref, gq_ref, gk_ref, bias_ref, o_ref, qs, ks, vs = refs
    dh = q_ref.shape[-1]
    kr = min(NA_KR_MAX, n_rows)
    scale = dh ** -0.5
    qs[...] = (_rms(q_ref[...], gq_ref[...]) * scale).astype(BF16)
    ks[...] = _rms(k_ref[...], gk_ref[...]).astype(BF16)
    vs[...] = v_ref[...].astype(BF16)
    kc = _rms(kc_ref[...], gk_ref[...]).astype(BF16)
    vc = vc_ref[...].astype(BF16)

    grp = NA_ROW_GROUP
    nk = kr * GRID_W

    def rows_group(gi, carry):
        r0 = gi * grp
        q0 = pl.multiple_of(r0 * GRID_W, grp * GRID_W)
        ws = [jnp.clip(r0 + t - kr // 2, 0, n_rows - kr) for t in range(grp)]
        k0 = [pl.multiple_of(w * GRID_W, GRID_W) for w in ws]
        s_loc = [_dot_nt(qs[pl.ds(q0 + t * GRID_W, GRID_W), :], ks[pl.ds(k0[t], nk), :]) for t in range(grp)]
        s_ctx = _dot_nt(qs[pl.ds(q0, grp * GRID_W), :], kc)
        p_loc, p_ctx, den = [], [], []
        for t in range(grp):
            sl = s_loc[t] + bias_ref[r0 + t - ws[t]]
            sc = s_ctx[t * GRID_W:(t + 1) * GRID_W]
            m = jnp.maximum(jnp.max(sl, axis=-1, keepdims=True), jnp.max(sc, axis=-1, keepdims=True))
            pl_t = jnp.exp(sl - m)
            pc_t = jnp.exp(sc - m)
            den.append(jnp.sum(pl_t, axis=-1, keepdims=True) + jnp.sum(pc_t, axis=-1, keepdims=True))
            p_loc.append(pl_t.astype(BF16))
            p_ctx.append(pc_t.astype(BF16))
        o_loc = [jnp.dot(p_loc[t], vs[pl.ds(k0[t], nk), :], preferred_element_type=F32) for t in range(grp)]
        o_ctx = jnp.dot(jnp.concatenate(p_ctx, axis=0), vc, preferred_element_type=F32)
        for t in range(grp):
            o = (o_loc[t] + o_ctx[t * GRID_W:(t + 1) * GRID_W]) / den[t]
            o_ref[pl.ds(q0 + t * GRID_W, GRID_W), :] = o.astype(o_ref.dtype)
        return carry

    lax.fori_loop(0, n_rows // grp, rows_group, 0)

    if with_ctx_out:
        qc = (_rms(qc_ref[...], gq_ref[...]) * scale).astype(BF16)
        s = _dot_nt(qc, kc)
        p = jnp.exp(s - jnp.max(s, axis=-1, keepdims=True))
        o = jnp.dot(p.astype(BF16), vc, preferred_element_type=F32) / jnp.sum(p, axis=-1, keepdims=True)
        oc_ref[...] = o.astype(oc_ref.dtype)


def natten(px, pc, col_q, gq, gk, bias, with_ctx_out):
    bsz, l, _ = px.shape
    lc = pc.shape[1]
    h, kr, _, nk = bias.shape
    dh = gq.shape[-1]
    cb = col_q // dh
    assert (l // GRID_W) % NA_ROW_GROUP == 0

    def head_spec(rows, which):
        return pl.BlockSpec((None, rows, dh), lambda b, hh: (b, 0, cb + which * h + hh))

    in_specs = [head_spec(l, 0), head_spec(l, 1), head_spec(l, 2), head_spec(lc, 1), head_spec(lc, 2),
                pl.BlockSpec((1, dh), lambda b, hh: (0, 0)), pl.BlockSpec((1, dh), lambda b, hh: (0, 0)),
                pl.BlockSpec((None, kr, GRID_W, nk), lambda b, hh: (hh, 0, 0, 0))]
    args = [px, px, px, pc, pc, gq, gk, bias]
    out_specs = [pl.BlockSpec((None, l, dh), lambda b, hh: (b, 0, hh))]
    out_shape = [jax.ShapeDtypeStruct((bsz, l, h * dh), BF16)]
    if with_ctx_out:
        in_specs.append(head_spec(lc, 0))
        args.append(pc)
        out_specs.append(pl.BlockSpec((None, lc, dh), lambda b, hh: (b, 0, hh)))
        out_shape.append(jax.ShapeDtypeStruct((bsz, lc, h * dh), BF16))
    outs = pl.pallas_call(
        functools.partial(_natten_kernel, n_rows=l // GRID_W, with_ctx_out=with_ctx_out),
        grid=(bsz, h),
        in_specs=in_specs, out_specs=out_specs, out_shape=out_shape,
        scratch_shapes=[pltpu.VMEM((l, dh), BF16)] * 3,
        compiler_params=_cparams(("arbitrary", "arbitrary")),
        name="natten",
    )(*args)
    return outs if with_ctx_out else (outs[0], None)


def _cos_sin(n, period):
    k = np.arange(n, dtype=np.int64)
    ang = (2.0 * np.pi / period) * ((k[:, None] * k[None, :]) % period)
    return np.cos(ang), np.sin(ang)


def _fourier_kernel(u_ref, w_ref, cd_ref, sd_ref, cs_ref, o_ref, wc_ref, ws_ref, v_ref, *, scale):
    l, c = u_ref.shape
    dg = cd_ref.shape[0]

    @pl.when(pl.program_id(0) == 0)
    def _():
        wc_ref[...] = jnp.zeros_like(wc_ref)
        ws_ref[...] = jnp.zeros_like(ws_ref)
        for g in range(c // dg):
            sl = slice(g * dg, (g + 1) * dg)
            wg = w_ref[g]
            wc_ref[sl, sl] = jnp.dot(cd_ref[...], wg, precision=lax.Precision.HIGHEST,
                                     preferred_element_type=F32).astype(BF16)
            ws_ref[sl, sl] = jnp.dot(sd_ref[...], wg, precision=lax.Precision.HIGHEST,
                                     preferred_element_type=F32).astype(BF16)

    u = u_ref[...].astype(BF16)
    v_ref[0:l, :] = jnp.dot(u, wc_ref[...], preferred_element_type=F32).astype(BF16)
    v_ref[l:2 * l, :] = jnp.dot(u, ws_ref[...], preferred_element_type=F32).astype(BF16)
    y = jnp.dot(cs_ref[...], v_ref[...], preferred_element_type=F32)
    o_ref[...] = (y * scale).astype(o_ref.dtype)


def fourier_mix(p, col, w):
    bsz, l, _ = p.shape
    g, dg, _ = w.shape
    c = g * dg
    cl, sl = _cos_sin(l, l)
    cd, sd = _cos_sin(dg, dg)
    cs = jnp.asarray(np.concatenate([cl, -sl], axis=1), BF16)
    return pl.pallas_call(
        functools.partial(_fourier_kernel, scale=float((l * dg) ** -0.5)),
        grid=(bsz,),
        in_specs=[pl.BlockSpec((None, l, c), lambda b: (b, 0, col // c)),
                  pl.BlockSpec((g, dg, dg), lambda b: (0, 0, 0)),
                  pl.BlockSpec((dg, dg), lambda b: (0, 0)),
                  pl.BlockSpec((dg, dg), lambda b: (0, 0)),
                  pl.BlockSpec((l, 2 * l), lambda b: (0, 0), pipeline_mode=pl.Buffered(1))],
        out_specs=pl.BlockSpec((None, l, c), lambda b: (b, 0, 0)),
        out_shape=jax.ShapeDtypeStruct((bsz, l, c), BF16),
        scratch_shapes=[pltpu.VMEM((c, c), BF16), pltpu.VMEM((c, c), BF16), pltpu.VMEM((2 * l, c), BF16)],
        compiler_params=_cparams(("arbitrary",)),
        name="fourier_mix",
    )(p, w, jnp.asarray(cd, F32), jnp.asarray(sd, F32), cs)


HY_CBLK = 256
HY_FBLK = 512


def _hyena_dft_matrix(l):
    k = np.arange(l, dtype=np.int64)
    ang = (np.pi / l) * ((k[:, None] * k[None, :]) % (2 * l))
    sn = np.sin(ang)
    sn[0, :] = 1.0 - 2.0 * (k % 2)
    return jnp.asarray(np.concatenate([np.cos(ang), sn], axis=0), BF16)


def _hyena_pos_features(l):
    t = np.linspace(0.0, 1.0, l)[:, None]
    w = (2.0 * np.pi / l) * np.arange(l)[:, None]
    f = np.linspace(1e-4, HY_BANDS - 1, HY_BANDS)[None, :]
    z = np.concatenate([t, np.cos(f * w), -np.sin(f * w)], axis=-1)
    return np.pad(z, ((0, 0), (0, LANES - z.shape[1])))


def _split_bf16(x):
    hi = x.astype(BF16)
    return hi, (x - hi.astype(F32)).astype(BF16)


def _hyena_filter_kernel(z_ref, w1_ref, b1_ref, w2_ref, b2_ref, fr_ref, w3f_ref, w3b_ref, dl_ref, m_ref,
                         p_ref, q_ref):
    l = z_ref.shape[0]
    hp = lax.Precision.HIGHEST
    z = z_ref[...]
    h = jnp.sin(fr_ref[0:1, :] * (jnp.dot(z, w1_ref[...], precision=hp, preferred_element_type=F32)
                                  + b1_ref[...]))
    h = jnp.sin(fr_ref[1:2, :] * (jnp.dot(h, w2_ref[...], precision=hp, preferred_element_type=F32)
                                  + b2_ref[...]))
    window = jnp.exp(-z[:, 0:1] * dl_ref[...]) + HY_DECAY_SHIFT
    hf = jnp.dot(h, w3f_ref[...], precision=hp, preferred_element_type=F32) * window
    hb = jnp.dot(h, w3b_ref[...], precision=hp, preferred_element_type=F32) * window
    norm = jnp.sum(jnp.abs(hf) + jnp.abs(hb), axis=0, keepdims=True) + EPS
    hf = hf / norm
    hb = hb / norm
    g1h, g1l = _split_bf16(hf + hb)
    g2h, g2l = _split_bf16(hb - hf)
    f1 = (jnp.dot(m_ref[...], g1h, preferred_element_type=F32)
          + jnp.dot(m_ref[...], g1l, preferred_element_type=F32))
    f2 = (jnp.dot(m_ref[l:2 * l, :], g2h, preferred_element_type=F32)
          + jnp.dot(m_ref[l:2 * l, :], g2l, preferred_element_type=F32))
    p_ref[...] = f1[0:l]
    row = lax.broadcasted_iota(jnp.int32, f2.shape, 0)
    q_ref[...] = jnp.where(row == 0, f1[l:l + 1], f2)


def hyena_filter_spectrum(l, w1, b1, w2, b2, w3, freq, m):
    c = w3.shape[1] // 2
    hid = w1.shape[1]
    z = jnp.asarray(_hyena_pos_features(l), F32)
    w1p = jnp.pad(w1, ((0, z.shape[1] - w1.shape[0]), (0, 0)))
    deltas = np.abs(np.linspace(math.log(HY_DECAY_TARGET) / HY_SLOW_PCT,
                                math.log(HY_DECAY_TARGET) / HY_FAST_PCT, c))[None, :]
    nb = c // HY_CBLK
    full = lambda shape: pl.BlockSpec(shape, lambda j: (0,) * len(shape))
    return pl.pallas_call(
        _hyena_filter_kernel,
        grid=(nb,),
        in_specs=[full(z.shape), full(w1p.shape), full((1, hid)), full(w2.shape), full((1, hid)),
                  full((2, hid)),
                  pl.BlockSpec((hid, HY_CBLK), lambda j: (0, j)),
                  pl.BlockSpec((hid, HY_CBLK), lambda j: (0, nb + j)),
                  pl.BlockSpec((1, HY_CBLK), lambda j: (0, j)),
                  pl.BlockSpec(m.shape, lambda j: (0, 0), pipeline_mode=pl.Buffered(1))],
        out_specs=[pl.BlockSpec((l, HY_CBLK), lambda j: (0, j))] * 2,
        out_shape=[jax.ShapeDtypeStruct((l, c), F32)] * 2,
        compiler_params=_cparams(("arbitrary",)),
        name="hyena_filter",
    )(z, w1p, b1.reshape(1, hid), w2, b2.reshape(1, hid), freq, w3, w3, jnp.asarray(deltas, F32), m)


def _shift_rows(u, down):
    l = u.shape[0]
    row = lax.broadcasted_iota(jnp.int32, u.shape, 0)
    if down:
        return jnp.where(row == 0, 0.0, pltpu.roll(u, 1, 0))
    return jnp.where(row == l - 1, 0.0, pltpu.roll(u, l - 1, 0))


def _dwconv3(u, taps):
    return (taps[3:4] + _shift_rows(u, True) * taps[0:1] + u * taps[1:2] + _shift_rows(u, False) * taps[2:3])


def _hyena_conv_kernel(x0_ref, x1_ref, v_ref, taps_ref, bias_ref, p_ref, q_ref, m_ref, o_ref):
    l, c = x0_ref.shape
    x1 = _dwconv3(x1_ref[...], taps_ref[1])
    s = _dwconv3(v_ref[...], taps_ref[2]) * x1
    sb = s.astype(BF16)
    fb = min(HY_FBLK, l)
    nblk = l // fb
    inv_l = 1.0 / l

    def forward(i):
        return (jnp.dot(m_ref[i * fb:(i + 1) * fb, :], sb, preferred_element_type=F32),
                jnp.dot(m_ref[l + i * fb:l + (i + 1) * fb, :], sb, preferred_element_type=F32))

    y_cos_acc = y_sin_acc = nyquist = None
    ab = forward(0)
    x0 = _dwconv3(x0_ref[...], taps_ref[0])
    for i in range(nblk):
        a, b = ab
        if i + 1 < nblk:
            ab = forward(i + 1)
        p, q = p_ref[i * fb:(i + 1) * fb, :], q_ref[i * fb:(i + 1) * fb, :]
        bq = b * q
        if i == 0:
            first = lax.broadcasted_iota(jnp.int32, (fb, c), 0) == 0
            nyquist = bq[0:1] * (0.5 * inv_l)
            y_cos = (a * p + jnp.where(first, 0.0, bq)) * jnp.where(first, 0.5 * inv_l, inv_l)
            y_sin = jnp.where(first, 0.0, (b * p - a * q) * inv_l)
        else:
            y_cos = (a * p + bq) * inv_l
            y_sin = (b * p - a * q) * inv_l
        d_cos = jnp.dot(m_ref[0:l, i * fb:(i + 1) * fb], y_cos.astype(BF16), preferred_element_type=F32)
        d_sin = jnp.dot(m_ref[l:2 * l, i * fb:(i + 1) * fb], y_sin.astype(BF16), preferred_element_type=F32)
        y_cos_acc = d_cos if y_cos_acc is None else y_cos_acc + d_cos
        y_sin_acc = d_sin if y_sin_acc is None else y_sin_acc + d_sin
    row = lax.broadcasted_iota(jnp.int32, (l, c), 0)
    y = (y_cos_acc + jnp.where(row == 0, 0.0, y_sin_acc)
         + jnp.where(row % 2 == 0, 1.0, -1.0) * nyquist)
    o_ref[...] = ((y + s * bias_ref[...]) * x0).astype(o_ref.dtype)


def hyena_conv(p, col, conv_w, conv_b, bias, spec_p, spec_q, m):
    bsz, l, _ = p.shape
    c = bias.shape[-1]
    nb = c // HY_CBLK
    cb0 = col // HY_CBLK
    taps = jnp.concatenate([conv_w, conv_b[None]], axis=0)
    taps = taps.reshape(4, 3, nb, HY_CBLK).transpose(2, 1, 0, 3)

    def part(k):
        return pl.BlockSpec((None, l, HY_CBLK), lambda j, b: (b, 0, cb0 + k * nb + j))

    return pl.pallas_call(
        _hyena_conv_kernel,
        grid=(nb, bsz),
        in_specs=[part(0), part(1), part(2),
                  pl.BlockSpec((None, 3, 4, HY_CBLK), lambda j, b: (j, 0, 0, 0)),
                  pl.BlockSpec((1, HY_CBLK), lambda j, b: (0, j)),
                  pl.BlockSpec((l, HY_CBLK), lambda j, b: (0, j)),
                  pl.BlockSpec((l, HY_CBLK), lambda j, b: (0, j)),
                  pl.BlockSpec(m.shape, lambda j, b: (0, 0), pipeline_mode=pl.Buffered(1))],
        out_specs=pl.BlockSpec((None, l, HY_CBLK), lambda j, b: (b, 0, j)),
        out_shape=jax.ShapeDtypeStruct((bsz, l, c), BF16),
        compiler_params=_cparams(("arbitrary", "arbitrary")),
        name="hyena_conv",
    )(p, p, p, taps, bias.reshape(1, c), spec_p, spec_q, m)


GLA_HP = 2
GLA_ROPE_PAIR = 16
GLA_SCAN_UNROLL = 4


def _gla_rope_tables(l, dk):
    half = dk // 2
    nf = half // 2
    assert nf == GLA_ROPE_PAIR
    inv = ROPE_THETA ** (-np.arange(nf, dtype=np.float64) / nf)
    t = np.arange(l)
    ang_r = (t // GRID_W)[:, None] * inv
    ang_c = (t % GRID_W)[:, None] * inv
    cos = np.concatenate([np.cos(ang_r)] * 2 + [np.cos(ang_c)] * 2, axis=1)
    sin = np.concatenate([-np.sin(ang_r), np.sin(ang_r), -np.sin(ang_c), np.sin(ang_c)], axis=1)
    return (jnp.asarray(np.tile(cos, (1, GLA_HP)), F32), jnp.asarray(np.tile(sin, (1, GLA_HP)), F32))


def _rope(x, cos, sin):
    lane = lax.broadcasted_iota(jnp.int32, x.shape, 1)
    lanes = x.shape[1]
    partner = jnp.where(lane % (2 * GLA_ROPE_PAIR) < GLA_ROPE_PAIR,
                        pltpu.roll(x, lanes - GLA_ROPE_PAIR, 1), pltpu.roll(x, GLA_ROPE_PAIR, 1))
    return x * cos + partner * sin


def _log_sigmoid(x):
    return jnp.minimum(x, 0.0) - jnp.log(1.0 + jnp.exp(-jnp.abs(x)))


def _gla_kernel(*refs, with_ctx_out):
    (q_ref, k_ref, v_ref, r_ref, z_ref, cq_ref, ck_ref, cv_ref, cr_ref, cz_ref,
     wz_ref, bz_ref, g_ref, cos_ref, sin_ref) = refs[:15]
    if with_ctx_out:
        o_ref, oc_ref = refs[15:17]
        scratch = refs[17:]
    else:
        o_ref, oc_ref = refs[15], None
        scratch = refs[16:]
    qs, ks, las, ofs, obs, cqs, cks, clas, cofs, cobs, st_f, st_b = scratch
    dk2 = q_ref.shape[1]
    dv2 = v_ref.shape[1]
    dk, dv = dk2 // GLA_HP, dv2 // GLA_HP
    ch = GLA_CHUNK
    hp = lax.Precision.HIGHEST

    def gates(z):
        pre = jnp.dot(z, wz_ref[...], precision=hp, preferred_element_type=F32) + bz_ref[...]
        return _log_sigmoid(pre) * (1.0 / GLA_TAU)

    qs[...] = _rope(q_ref[...] * dk ** -0.5, cos_ref[...], sin_ref[...])
    ks[...] = _rope(k_ref[...], cos_ref[...], sin_ref[...])
    las[...] = gates(z_ref[...])
    cqs[...] = cq_ref[...] * dk ** -0.5
    cks[...] = ck_ref[...]
    clas[...] = gates(cz_ref[...])
    st_f[...] = jnp.zeros_like(st_f)
    st_b[...] = jnp.zeros_like(st_b)

    ri = lax.broadcasted_iota(jnp.int32, (ch, ch), 0)
    ci = lax.broadcasted_iota(jnp.int32, (ch, ch), 1)
    tri = {False: ri >= ci, True: ri <= ci}
    tri_b16 = {d: jnp.where(m, 1.0, 0.0).astype(BF16) for d, m in tri.items()}
    tri2 = {d: jnp.concatenate([m] * GLA_HP, axis=0) for d, m in tri.items()}
    lane_head = lax.broadcasted_iota(jnp.int32, (ch, dk2), 1) // dk
    st_r = lax.broadcasted_iota(jnp.int32, (dv2, dk2), 0) // dv
    st_c = lax.broadcasted_iota(jnp.int32, (dv2, dk2), 1) // dk
    st_diag = st_r == st_c

    def scan(q_s, k_s, la_s, v_in, of_s, ob_s, n):
        unroll = GLA_SCAN_UNROLL

        def body(trip, carry):
            items = []
            for u in range(unroll):
                c = trip * unroll + u
                items.append((False, pl.ds(pl.multiple_of(c * ch, ch), ch)))
                items.append((True, pl.ds(pl.multiple_of((n - 1 - c) * ch, ch), ch)))
            v_c = [v_in[rows, :].astype(BF16) for _, rows in items]
            cum = []
            for bw, rows in items:
                la_hi, la_lo = _split_bf16(la_s[rows, dk2:2 * dk2] if bw else la_s[rows, 0:dk2])
                cum.append(jnp.dot(tri_b16[bw], la_hi, preferred_element_type=F32)
                           + jnp.dot(tri_b16[bw], la_lo, preferred_element_type=F32))
            q_dec, k_end, decay, sc = [], [], [], []
            for i, (bw, rows) in enumerate(items):
                q_c, k_c = q_s[rows, :], k_s[rows, :]
                tot = cum[i][0:1] if bw else cum[i][ch - 1:ch]
                qd = q_c * jnp.exp(cum[i])
                k_inv = (k_c * jnp.exp(-cum[i])).astype(BF16)
                k_end.append((k_c * jnp.exp(tot - cum[i])).astype(BF16))
                decay.append(jnp.exp(tot))
                q_heads = jnp.concatenate([jnp.where(lane_head == h, qd, 0.0) for h in range(GLA_HP)], axis=0)
                sc.append(_dot_nt(q_heads.astype(BF16), k_inv))
                q_dec.append(qd.astype(BF16))
            o_intra, ds_t = [], []
            for i, (bw, rows) in enumerate(items):
                pv = jnp.dot(jnp.where(tri2[bw], sc[i], 0.0).astype(BF16), v_c[i],
                             preferred_element_type=F32)
                o_intra.append(jnp.concatenate(
                    [pv[h * ch:(h + 1) * ch, h * dv:(h + 1) * dv] for h in range(GLA_HP)], axis=1))
                ds_t.append(lax.dot_general(v_c[i], k_end[i], (((0,), (0,)), ((), ())),
                                            preferred_element_type=F32))
            for i, (bw, rows) in enumerate(items):
                st, o_s = (st_b, ob_s) if bw else (st_f, of_s)
                s_t = st[...]
                o_s[rows, :] = o_intra[i] + _dot_nt(q_dec[i], s_t.astype(BF16))
                st[...] = s_t * decay[i] + jnp.where(st_diag, ds_t[i], 0.0)
            return carry

        lax.fori_loop(0, n // unroll, body, 0)

    def finish(of_s, ob_s, gate_ref, out_ref, n_blocks, blk):
        def body(i, carry):
            rows = pl.ds(pl.multiple_of(i * blk, blk), blk)
            o = of_s[rows, :] + ob_s[rows, :]
            gate = gate_ref[rows, :]
            parts = []
            for h in range(GLA_HP):
                oh = o[:, h * dv:(h + 1) * dv]
                gh = gate[:, h * dv:(h + 1) * dv]
                parts.append(_rms(oh, g_ref[...]) * (gh * jax.nn.sigmoid(gh)))
            out_ref[rows, :] = jnp.concatenate(parts, axis=1).astype(out_ref.dtype)
            return carry
        lax.fori_loop(0, n_blocks, body, 0)

    l, lc = q_ref.shape[0], cq_ref.shape[0]
    scan(cqs, cks, clas, cv_ref, cofs, cobs, lc // ch)
    if with_ctx_out:
        finish(cofs, cobs, cr_ref, oc_ref, 1, lc)
    scan(qs, ks, las, v_ref, ofs, obs, l // ch)
    finish(ofs, obs, r_ref, o_ref, l // lc, lc)


def gla(px, pxz, pc, pcz, w_gate, b_gate, out_gain, with_ctx_out):
    bsz, l, _ = px.shape
    lc = pc.shape[1]
    dv = out_gain.shape[-1]
    hdk = w_gate.shape[-1]
    dk = hdk // GLA_HEADS
    dk2, dv2 = GLA_HP * dk, GLA_HP * dv
    nhp = GLA_HEADS // GLA_HP
    zw = pxz.shape[-1]
    rank = w_gate.shape[1]
    wz = jnp.zeros((nhp, zw, 2 * dk2), F32)
    for u in range(2):
        blk = w_gate[u].reshape(rank, nhp, dk2).transpose(1, 0, 2)
        wz = wz.at[:, u * rank:(u + 1) * rank, u * dk2:(u + 1) * dk2].set(blk)
    bz = b_gate.reshape(2, nhp, dk2).transpose(1, 0, 2).reshape(nhp, 1, 2 * dk2)
    cos, sin = _gla_rope_tables(l, dk)
    k_cb, v_cb, r_cb = hdk // dk2, 2 * hdk // dv2, (2 * hdk + GLA_HEADS * dv) // dv2

    def col(rows, width, cb):
        return pl.BlockSpec((None, rows, width), lambda b, j: (b, 0, cb + j))

    def whole(rows, width):
        return pl.BlockSpec((None, rows, width), lambda b, j: (b, 0, 0))

    const = lambda shape: pl.BlockSpec(shape, lambda b, j: (0,) * len(shape))
    in_specs = [col(l, dk2, 0), col(l, dk2, k_cb), col(l, dv2, v_cb), col(l, dv2, r_cb), whole(l, zw),
                col(lc, dk2, 0), col(lc, dk2, k_cb), col(lc, dv2, v_cb), col(lc, dv2, r_cb), whole(lc, zw),
                pl.BlockSpec((None, zw, 2 * dk2), lambda b, j: (j, 0, 0)),
                pl.BlockSpec((None, 1, 2 * dk2), lambda b, j: (j, 0, 0)),
                const((1, dv)), const((l, dk2)), const((l, dk2))]
    out_specs = [pl.BlockSpec((None, l, dv2), lambda b, j: (b, 0, j))]
    out_shape = [jax.ShapeDtypeStruct((bsz, l, GLA_HEADS * dv), BF16)]
    if with_ctx_out:
        out_specs.append(pl.BlockSpec((None, lc, dv2), lambda b, j: (b, 0, j)))
        out_shape.append(jax.ShapeDtypeStruct((bsz, lc, GLA_HEADS * dv), BF16))
    scratch = [pltpu.VMEM((l, dk2), F32), pltpu.VMEM((l, dk2), F32), pltpu.VMEM((l, 2 * dk2), F32),
               pltpu.VMEM((l, dv2), F32), pltpu.VMEM((l, dv2), F32),
               pltpu.VMEM((lc, dk2), F32), pltpu.VMEM((lc, dk2), F32), pltpu.VMEM((lc, 2 * dk2), F32),
               pltpu.VMEM((lc, dv2), F32), pltpu.VMEM((lc, dv2), F32),
               pltpu.VMEM((dv2, dk2), F32), pltpu.VMEM((dv2, dk2), F32)]
    outs = pl.pallas_call(
        functools.partial(_gla_kernel, with_ctx_out=with_ctx_out),
        grid=(bsz, nhp),
        in_specs=in_specs, out_specs=out_specs, out_shape=out_shape, scratch_shapes=scratch,
        compiler_params=_cparams(("arbitrary", "arbitrary")),
        name="gla",
    )(px, px, px, px, pxz, pc, pc, pc, pc, pcz, wz, bz, out_gain.reshape(1, dv), cos, sin)
    return outs if with_ctx_out else (outs[0], None)


def kernel(x, c, ctx, c_ctx, w_mod, b_mod, g_mix, w_in, gla_gate_w, gla_gate_b, gla_out_g,
           hy_conv_w, hy_conv_b, hy_w1, hy_b1, hy_w2, hy_b2, hy_w3, hy_freq, hy_bias, fn_w,
           na_q_g, na_k_g, na_rpb, w_out, g_ffn, ffn_w_up, ffn_conv_w, ffn_conv_b, ffn_w_down):
    bsz, l_lat, d = x.shape
    l_ctx = ctx.shape[1]
    depth = w_mod.shape[0]
    w = d // 4

    n_cond = -(-(bsz + 1) // SUBLANES) * SUBLANES
    cond = jnp.zeros((n_cond, d), F32).at[:bsz].set(c).at[bsz].set(c_ctx)
    mods = mod_vectors(cond, w_mod, b_mod).reshape(depth, n_cond, N_MOD, d)
    mods = jnp.pad(mods, ((0, 0), (0, 0), (0, MOD_ROWS - N_MOD), (0, 0)))

    z0 = 3 * w
    zw = 2 * GLA_GATE_RANK
    w_main = jnp.concatenate([w_in[:, :, :z0], w_in[:, :, z0 + zw:]], axis=-1).astype(BF16)
    w_z = jnp.pad(w_in[:, :, z0:z0 + zw], ((0, 0), (0, 0), (0, LANES - zw))).astype(BF16)
    w_out_b = w_out.astype(BF16)
    w_up_b = ffn_w_up.astype(BF16)
    w_down_b = ffn_w_down.astype(BF16)

    m_lat = _hyena_dft_matrix(l_lat)
    m_ctx = _hyena_dft_matrix(l_ctx)
    ctx_flat = ctx.reshape(1, bsz * l_ctx, d)
    for layer in range(depth):
        last = layer == depth - 1
        mod_x = mods[layer, :bsz]
        mod_c = mods[layer, bsz:bsz + 1]
        g_mix_l = g_mix[layer].reshape(1, d)
        g_ffn_l = g_ffn[layer].reshape(1, d)

        px, pxz = in_proj(x, mod_x, g_mix_l, w_main, w_z, layer, tm=1024)
        pc, pcz = in_proj(ctx_flat, mod_c, g_mix_l, w_main, w_z, layer, tm=1024)
        pc = pc.reshape(bsz, l_ctx, -1)
        pcz = pcz.reshape(bsz, l_ctx, -1)

        y_a, yc_a = gla(px, pxz, pc, pcz, gla_gate_w[layer], gla_gate_b[layer], gla_out_g[layer],
                        with_ctx_out=not last)
        hy_filt = (hy_w1[layer], hy_b1[layer], hy_w2[layer], hy_b2[layer], hy_w3[layer], hy_freq[layer])
        hy_p, hy_q = hyena_filter_spectrum(l_lat, *hy_filt, m_lat)
        y_b = hyena_conv(px, 3 * w, hy_conv_w[layer], hy_conv_b[layer], hy_bias[layer], hy_p, hy_q, m_lat)
        y_c = fourier_mix(px, 6 * w, fn_w[layer])
        na_bias = natten_bias_table(na_rpb[layer], l_lat // GRID_W)
        y_d, yc_d = natten(px, pc, 7 * w, na_q_g[layer].reshape(1, -1), na_k_g[layer].reshape(1, -1),
                           na_bias, with_ctx_out=not last)
        x = out_proj([y_a, y_b, y_c, y_d], w_out_b, layer, x, mod_x, tm=1024)
        cw = ffn_conv_w[layer]
        cb = ffn_conv_b[layer].reshape(1, -1)
        x = conv_ffn(x, mod_x, g_ffn_l, w_up_b, cw, cb, w_down_b, layer, tm=1024)

        if not last:
            hc_p, hc_q = hyena_filter_spectrum(l_ctx, *hy_filt, m_ctx)
            yc_b = hyena_conv(pc, 3 * w, hy_conv_w[layer], hy_conv_b[layer], hy_bias[layer], hc_p, hc_q, m_ctx)
            yc_c = fourier_mix(pc, 6 * w, fn_w[layer])
            ycs = [y.reshape(1, bsz * l_ctx, w) for y in (yc_a, yc_b, yc_c, yc_d)]
            ctx_flat = out_proj(ycs, w_out_b, layer, ctx_flat, mod_c, tm=1024)
            ctx_flat = conv_ffn(ctx_flat, mod_c, g_ffn_l, w_up_b, cw, cb, w_down_b, layer, tm=1024,
                                seg_len=l_ctx)
    return x
```

```python
import functools
import math

import numpy as np
import jax
import jax.numpy as jnp
from jax import lax
from jax.experimental import pallas as pl
from jax.experimental.pallas import tpu as pltpu

F32 = jnp.float32
BF16 = jnp.bfloat16

GRID_W = 64
GLA_HEADS = 4
GLA_GATE_RANK = 16
GLA_TAU = 16.0
GLA_CHUNK = 64
HY_BANDS = 16
HY_DECAY_TARGET = 1e-2
HY_FAST_PCT = 0.3
HY_SLOW_PCT = 1.5
HY_DECAY_SHIFT = 0.05
FN_GROUPS = 4
NA_HEADS = 4
NA_KR_MAX = 8
NA_KC = 16
ROPE_THETA = 10000.0
N_MOD = 6
EPS = 1e-6

V7X_VMEM_LIMIT = 58 * 1024 * 1024
SUBLANES = 8
LANES = 128
MOD_ROWS = 8
FFN_HALO = 16


def _cparams(sem):
    return pltpu.CompilerParams(dimension_semantics=sem, vmem_limit_bytes=V7X_VMEM_LIMIT)


def _mod_kernel(s_ref, w_ref, b_ref, o_ref):
    s = s_ref[...]
    s = s * jax.nn.sigmoid(s)
    o_ref[...] = jnp.dot(s.astype(BF16), w_ref[...].astype(BF16),
                         preferred_element_type=F32) + b_ref[...]


def mod_vectors(cond, w_mod, b_mod, tn=1024):
    depth, d, n = w_mod.shape
    r = cond.shape[0]
    return pl.pallas_call(
        _mod_kernel,
        grid=(depth, n // tn),
        in_specs=[pl.BlockSpec((r, d), lambda l, j: (0, 0)),
                  pl.BlockSpec((None, d, tn), lambda l, j: (l, 0, j)),
                  pl.BlockSpec((None, 1, tn), lambda l, j: (l, 0, j))],
        out_specs=pl.BlockSpec((None, r, tn), lambda l, j: (l, 0, j)),
        out_shape=jax.ShapeDtypeStruct((depth, r, n), F32),
        compiler_params=_cparams(("arbitrary", "arbitrary")),
        name="mod_vectors",
    )(cond, w_mod, b_mod.reshape(depth, 1, n))


def _norm_mod(x, gain, shift, scale):
    ms = jnp.mean(x * x, axis=-1, keepdims=True)
    return (x * lax.rsqrt(ms + EPS) * gain) * (1.0 + scale) + shift


def _inproj_kernel(x_ref, mod_ref, g_ref, w_ref, wz_ref, o_ref, oz_ref, h_ref):
    j = pl.program_id(2)
    tn = o_ref.shape[1]

    @pl.when(j == 0)
    def _():
        h = _norm_mod(x_ref[...], g_ref[...], mod_ref[0:1, :], mod_ref[1:2, :]).astype(BF16)
        h_ref[...] = h
        oz_ref[...] = jnp.dot(h, wz_ref[...], preferred_element_type=F32)

    w = w_ref[:, pl.ds(pl.multiple_of(j * tn, tn), tn)]
    o_ref[...] = jnp.dot(h_ref[...], w, preferred_element_type=F32)


def in_proj(x, mod, gain, w_main, w_z, layer, tm, tn=1024):
    g, l, d = x.shape
    n = w_main.shape[-1]
    nz = w_z.shape[-1]
    nt = l // tm

    def x_tile(b, i, j):
        t = jnp.minimum(b * nt + i + jnp.minimum(j, 1), g * nt - 1)
        return (t // nt, t % nt, 0)

    return pl.pallas_call(
        _inproj_kernel,
        grid=(g, l // tm, n // tn),
        in_specs=[pl.BlockSpec((None, tm, d), x_tile),
                  pl.BlockSpec((None, MOD_ROWS, d), lambda b, i, j: (b, 0, 0)),
                  pl.BlockSpec((1, d), lambda b, i, j: (0, 0)),
                  pl.BlockSpec((None, d, n), lambda b, i, j: (layer, 0, 0), pipeline_mode=pl.Buffered(1)),
                  pl.BlockSpec((None, d, nz), lambda b, i, j: (layer, 0, 0))],
        out_specs=[pl.BlockSpec((None, tm, tn), lambda b, i, j: (b, i, j)),
                   pl.BlockSpec((None, tm, nz), lambda b, i, j: (b, i, 0))],
        out_shape=[jax.ShapeDtypeStruct((g, l, n), F32),
                   jax.ShapeDtypeStruct((g, l, nz), F32)],
        scratch_shapes=[pltpu.VMEM((tm, d), BF16)],
        compiler_params=_cparams(("arbitrary", "arbitrary", "arbitrary")),
        name="in_proj",
    )(x, mod, gain, w_main, w_z)


def _outproj_kernel(ya_ref, yb_ref, yc_ref, yd_ref, w_ref, x_ref, mod_ref, o_ref):
    kw = ya_ref.shape[-1]
    acc = jnp.dot(ya_ref[...], w_ref[0 * kw:1 * kw, :], preferred_element_type=F32)
    acc += jnp.dot(yb_ref[...], w_ref[1 * kw:2 * kw, :], preferred_element_type=F32)
    acc += jnp.dot(yc_ref[...], w_ref[2 * kw:3 * kw, :], preferred_element_type=F32)
    acc += jnp.dot(yd_ref[...], w_ref[3 * kw:4 * kw, :], preferred_element_type=F32)
    o_ref[...] = x_ref[...] + mod_ref[2:3, :] * acc


def out_proj(ys, w_out, layer, x, mod, tm):
    g, l, d = x.shape
    kw = ys[0].shape[-1]
    yspec = pl.BlockSpec((None, tm, kw), lambda b, i: (b, i, 0))
    return pl.pallas_call(
        _outproj_kernel,
        grid=(g, l // tm),
        in_specs=[yspec, yspec, yspec, yspec,
                  pl.BlockSpec((None, 4 * kw, d), lambda b, i: (layer, 0, 0), pipeline_mode=pl.Buffered(1)),
                  pl.BlockSpec((None, tm, d), lambda b, i: (b, i, 0)),
                  pl.BlockSpec((None, MOD_ROWS, d), lambda b, i: (b, 0, 0))],
        out_specs=pl.BlockSpec((None, tm, d), lambda b, i: (b, i, 0)),
        out_shape=jax.ShapeDtypeStruct((g, l, d), F32),
        compiler_params=_cparams(("arbitrary", "arbitrary")),
        name="out_proj",
    )(*ys, w_out, x, mod)


def _ffn_kernel(x_ref, xp_ref, xn_ref, mod_ref, g_ref, wa_ref, wg_ref, cw_ref, cb_ref, wd_ref,
                o_ref, h_ref, act0_ref, act1_ref, *, seg_len, n_hidden_tiles):
    i = pl.program_id(1)
    j = pl.program_id(2)
    nf = n_hidden_tiles
    acts = (act0_ref, act1_ref)
    tm = x_ref.shape[0]
    tf = wa_ref.shape[1]
    hs = FFN_HALO

    def prologue():
        gain, shift, scale = g_ref[...], mod_ref[3:4, :], mod_ref[4:5, :]
        h_ref[hs:hs + tm, :] = _norm_mod(x_ref[...], gain, shift, scale).astype(BF16)
        hp = jnp.where(i > 0, _norm_mod(xp_ref[...], gain, shift, scale), 0.0)
        hn = jnp.where(i < pl.num_programs(1) - 1, _norm_mod(xn_ref[...], gain, shift, scale), 0.0)
        zero = jnp.zeros_like(hp)
        h_ref[0:hs, :] = jnp.concatenate([zero, hp], axis=0).astype(BF16)
        h_ref[hs + tm:2 * hs + tm, :] = jnp.concatenate([hn, zero], axis=0).astype(BF16)
        o_ref[...] = x_ref[...]

    def conv(u, c0, width):
        w = cw_ref[:, c0:c0 + width]
        prev, nxt = u[hs - 1:hs - 1 + tm], u[hs + 1:hs + 1 + tm]
        if seg_len is not None:
            pos = lax.broadcasted_iota(jnp.int32, prev.shape, 0) % seg_len
            prev = jnp.where(pos == 0, 0.0, prev)
            nxt = jnp.where(pos == seg_len - 1, 0.0, nxt)
        return cb_ref[:, c0:c0 + width] + prev * w[0:1] + u[hs:hs + tm] * w[1:2] + nxt * w[2:3]

    def up_matmuls():
        h = h_ref[...]
        ug = jnp.dot(h, wg_ref[...], preferred_element_type=F32)
        ua = jnp.dot(h, wa_ref[...], preferred_element_type=F32)
        return ua, ug

    def gate_to(act_ref, ua, ug):
        gt = conv(ug, tf, tf)
        swish = gt * jax.nn.sigmoid(gt)
        act_ref[...] = (conv(ua, 0, tf) * swish).astype(BF16)

    def down_from(act_ref):
        o_ref[...] += mod_ref[5:6, :] * jnp.dot(act_ref[...], wd_ref[...], preferred_element_type=F32)

    middle = jnp.logical_and(j > 0, j < nf)

    @pl.when(j == 0)
    def _():
        prologue()
        ua, ug = up_matmuls()
        gate_to(acts[0], ua, ug)

    for parity in range(2):
        @pl.when(jnp.logical_and(middle, j % 2 == parity))
        def _():
            ua, ug = up_matmuls()
            down_from(acts[1 - parity])
            gate_to(acts[parity], ua, ug)

    @pl.when(j == nf)
    def _():
        down_from(acts[(n_hidden_tiles - 1) % 2])


def conv_ffn(x, mod, gain, w_up, conv_w, conv_b, w_down, layer, tm, tf=512, seg_len=None):
    g, l, d = x.shape
    assert seg_len is None or (tm % seg_len == 0 and l % tm == 0)
    f = w_down.shape[1]
    nf = f // tf
    nb = tm // SUBLANES
    last = l // SUBLANES - 1
    cw = conv_w.reshape(3, 2, nf, tf).transpose(2, 0, 1, 3).reshape(nf, 3, 2 * tf)
    cb = conv_b.reshape(1, 2, nf, tf).transpose(2, 0, 1, 3).reshape(nf, 1, 2 * tf)
    nt = l // tm

    def x_tile(b, i, j):
        n = jnp.minimum(b * nt + i + jnp.minimum(j, 1), g * nt - 1)
        return (n // nt, n % nt, 0)

    def up_tile(j):
        return jnp.where(j < nf, j, 0)

    def down_tile(j):
        return jnp.where(j == 0, nf - 1, j - 1)

    return pl.pallas_call(
        functools.partial(_ffn_kernel, seg_len=seg_len, n_hidden_tiles=nf),
        grid=(g, l // tm, nf + 1),
        in_specs=[pl.BlockSpec((None, tm, d), x_tile),
                  pl.BlockSpec((None, SUBLANES, d), lambda b, i, j: (b, jnp.maximum(i * nb - 1, 0), 0)),
                  pl.BlockSpec((None, SUBLANES, d), lambda b, i, j: (b, jnp.minimum((i + 1) * nb, last), 0)),
                  pl.BlockSpec((None, MOD_ROWS, d), lambda b, i, j: (b, 0, 0)),
                  pl.BlockSpec((1, d), lambda b, i, j: (0, 0)),
                  pl.BlockSpec((None, d, tf), lambda b, i, j: (layer, 0, up_tile(j))),
                  pl.BlockSpec((None, d, tf), lambda b, i, j: (layer, 0, nf + up_tile(j))),
                  pl.BlockSpec((None, 3, 2 * tf), lambda b, i, j: (up_tile(j), 0, 0)),
                  pl.BlockSpec((None, 1, 2 * tf), lambda b, i, j: (up_tile(j), 0, 0)),
                  pl.BlockSpec((None, tf, d), lambda b, i, j: (layer, down_tile(j), 0))],
        out_specs=pl.BlockSpec((None, tm, d), lambda b, i, j: (b, i, 0)),
        out_shape=jax.ShapeDtypeStruct((g, l, d), F32),
        scratch_shapes=[pltpu.VMEM((tm + 2 * FFN_HALO, d), BF16),
                        pltpu.VMEM((tm, tf), BF16), pltpu.VMEM((tm, tf), BF16)],
        compiler_params=_cparams(("arbitrary", "arbitrary", "arbitrary")),
        name="conv_ffn",
    )(x, x, x, mod, gain, w_up, w_up, cw, cb, w_down)


def natten_bias_table(rpb, n_rows):
    kr = min(NA_KR_MAX, n_rows)
    h, n_dr, n_dc = rpb.shape
    col = np.arange(GRID_W)
    col0 = np.clip(col - NA_KC // 2, 0, GRID_W - NA_KC)
    in_win = (col[None, :] >= col0[:, None]) & (col[None, :] < col0[:, None] + NA_KC)
    lo = GRID_W - NA_KC
    ext = jnp.pad(rpb.astype(F32), ((0, 0), (0, 0), (lo, 2 * GRID_W - lo - n_dc)))
    skew = jnp.tile(ext, (1, 1, GRID_W))[:, :, :GRID_W * (2 * GRID_W - 1)]
    skew = skew.reshape(h, n_dr, GRID_W, 2 * GRID_W - 1)[:, :, :, GRID_W - 1:]
    skew = jnp.where(in_win[None, None], skew, -jnp.inf)
    tab = jnp.stack([skew[:, NA_KR_MAX - 1 - s:NA_KR_MAX - 1 - s + kr] for s in range(kr)], axis=1)
    return tab.transpose(0, 1, 3, 2, 4).reshape(h, kr, GRID_W, kr * GRID_W)


NA_ROW_GROUP = 16


def _rms(x, gain):
    return x * lax.rsqrt(jnp.mean(x * x, axis=-1, keepdims=True) + EPS) * gain


def _dot_nt(a, b):
    return lax.dot_general(a, b, (((1,), (1,)), ((), ())), preferred_element_type=F32)


def _natten_kernel(*refs, n_rows, with_ctx_out):
    if with_ctx_out:
        (q_ref, k_ref, v_ref, kc_ref, vc_ref, gq_ref, gk_ref, bias_ref, qc_ref,
         o_ref, oc_ref, qs, ks, vs) = refs
    else:
        q_ref, k_ref, v_ref, kc_ref, vc_ref, gq_ref, gk_ref, bias_ref, o_ref, qs, ks, vs = refs
    dh = q_ref.shape[-1]
    kr = min(NA_KR_MAX, n_rows)
    scale = dh ** -0.5
    qs[...] = (_rms(q_ref[...], gq_ref[...]) * scale).astype(BF16)
    ks[...] = _rms(k_ref[...], gk_ref[...]).astype(BF16)
    vs[...] = v_ref[...].astype(BF16)
    kc = _rms(kc_ref[...], gk_ref[...]).astype(BF16)
    vc = vc_ref[...].astype(BF16)

    grp = NA_ROW_GROUP
    nk = kr * GRID_W

    def rows_group(gi, carry):
        r0 = gi * grp
        q0 = pl.multiple_of(r0 * GRID_W, grp * GRID_W)
        ws = [jnp.clip(r0 + t - kr // 2, 0, n_rows - kr) for t in range(grp)]
        k0 = [pl.multiple_of(w * GRID_W, GRID_W) for w in ws]
        s_loc = [_dot_nt(qs[pl.ds(q0 + t * GRID_W, GRID_W), :], ks[pl.ds(k0[t], nk), :]) for t in range(grp)]
        s_ctx = _dot_nt(qs[pl.ds(q0, grp * GRID_W), :], kc)
        p_loc, p_ctx, den = [], [], []
        for t in range(grp):
            sl = s_loc[t] + bias_ref[r0 + t - ws[t]]
            sc = s_ctx[t * GRID_W:(t + 1) * GRID_W]
            m = jnp.maximum(jnp.max(sl, axis=-1, keepdims=True), jnp.max(sc, axis=-1, keepdims=True))
            pl_t = jnp.exp(sl - m)
            pc_t = jnp.exp(sc - m)
            den.append(jnp.sum(pl_t, axis=-1, keepdims=True) + jnp.sum(pc_t, axis=-1, keepdims=True))
            p_loc.append(pl_t.astype(BF16))
            p_ctx.append(pc_t.astype(BF16))
        o_loc = [jnp.dot(p_loc[t], vs[pl.ds(k0[t], nk), :], preferred_element_type=F32) for t in range(grp)]
        o_ctx = jnp.dot(jnp.concatenate(p_ctx, axis=0), vc, preferred_element_type=F32)
        for t in range(grp):
            o = (o_loc[t] + o_ctx[t * GRID_W:(t + 1) * GRID_W]) / den[t]
            o_ref[pl.ds(q0 + t * GRID_W, GRID_W), :] = o.astype(o_ref.dtype)
        return carry

    lax.fori_loop(0, n_rows // grp, rows_group, 0)

    if with_ctx_out:
        qc = (_rms(qc_ref[...], gq_ref[...]) * scale).astype(BF16)
        s = _dot_nt(qc, kc)
        p = jnp.exp(s - jnp.max(s, axis=-1, keepdims=True))
        o = jnp.dot(p.astype(BF16), vc, preferred_element_type=F32) / jnp.sum(p, axis=-1, keepdims=True)
        oc_ref[...] = o.astype(oc_ref.dtype)


def natten(px, pc, col_q, gq, gk, bias, with_ctx_out):
    bsz, l, _ = px.shape
    lc = pc.shape[1]
    h, kr, _, nk = bias.shape
    dh = gq.shape[-1]
    cb = col_q // dh
    assert (l // GRID_W) % NA_ROW_GROUP == 0

    def head_spec(rows, which):
        return pl.BlockSpec((None, rows, dh), lambda b, hh: (b, 0, cb + which * h + hh))

    in_specs = [head_spec(l, 0), head_spec(l, 1), head_spec(l, 2), head_spec(lc, 1), head_spec(lc, 2),
                pl.BlockSpec((1, dh), lambda b, hh: (0, 0)), pl.BlockSpec((1, dh), lambda b, hh: (0, 0)),
                pl.BlockSpec((None, kr, GRID_W, nk), lambda b, hh: (hh, 0, 0, 0))]
    args = [px, px, px, pc, pc, gq, gk, bias]
    out_specs = [pl.BlockSpec((None, l, dh), lambda b, hh: (b, 0, hh))]
    out_shape = [jax.ShapeDtypeStruct((bsz, l, h * dh), BF16)]
    if with_ctx_out:
        in_specs.append(head_spec(lc, 0))
        args.append(pc)
        out_specs.append(pl.BlockSpec((None, lc, dh), lambda b, hh: (b, 0, hh)))
        out_shape.append(jax.ShapeDtypeStruct((bsz, lc, h * dh), BF16))
    outs = pl.pallas_call(
        functools.partial(_natten_kernel, n_rows=l // GRID_W, with_ctx_out=with_ctx_out),
        grid=(bsz, h),
        in_specs=in_specs, out_specs=out_specs, out_shape=out_shape,
        scratch_shapes=[pltpu.VMEM((l, dh), BF16)] * 3,
        compiler_params=_cparams(("arbitrary", "arbitrary")),
        name="natten",
    )(*args)
    return outs if with_ctx_out else (outs[0], None)


def _cos_sin(n, period):
    k = np.arange(n, dtype=np.int64)
    ang = (2.0 * np.pi / period) * ((k[:, None] * k[None, :]) % period)
    return np.cos(ang), np.sin(ang)


def _fourier_kernel(u_ref, w_ref, cd_ref, sd_ref, cs_ref, o_ref, wc_ref, ws_ref, v_ref, *, scale):
    l, c = u_ref.shape
    dg = cd_ref.shape[0]

    @pl.when(pl.program_id(0) == 0)
    def _():
        wc_ref[...] = jnp.zeros_like(wc_ref)
        ws_ref[...] = jnp.zeros_like(ws_ref)
        for g in range(c // dg):
            sl = slice(g * dg, (g + 1) * dg)
            wg = w_ref[g]
            wc_ref[sl, sl] = jnp.dot(cd_ref[...], wg, precision=lax.Precision.HIGHEST,
                                     preferred_element_type=F32).astype(BF16)
            ws_ref[sl, sl] = jnp.dot(sd_ref[...], wg, precision=lax.Precision.HIGHEST,
                                     preferred_element_type=F32).astype(BF16)

    u = u_ref[...].astype(BF16)
    v_ref[0:l, :] = jnp.dot(u, wc_ref[...], preferred_element_type=F32).astype(BF16)
    v_ref[l:2 * l, :] = jnp.dot(u, ws_ref[...], preferred_element_type=F32).astype(BF16)
    y = jnp.dot(cs_ref[...], v_ref[...], preferred_element_type=F32)
    o_ref[...] = (y * scale).astype(o_ref.dtype)


def fourier_mix(p, col, w):
    bsz, l, _ = p.shape
    g, dg, _ = w.shape
    c = g * dg
    cl, sl = _cos_sin(l, l)
    cd, sd = _cos_sin(dg, dg)
    cs = jnp.asarray(np.concatenate([cl, -sl], axis=1), BF16)
    return pl.pallas_call(
        functools.partial(_fourier_kernel, scale=float((l * dg) ** -0.5)),
        grid=(bsz,),
        in_specs=[pl.BlockSpec((None, l, c), lambda b: (b, 0, col // c)),
                  pl.BlockSpec((g, dg, dg), lambda b: (0, 0, 0)),
                  pl.BlockSpec((dg, dg), lambda b: (0, 0)),
                  pl.BlockSpec((dg, dg), lambda b: (0, 0)),
                  pl.BlockSpec((l, 2 * l), lambda b: (0, 0), pipeline_mode=pl.Buffered(1))],
        out_specs=pl.BlockSpec((None, l, c), lambda b: (b, 0, 0)),
        out_shape=jax.ShapeDtypeStruct((bsz, l, c), BF16),
        scratch_shapes=[pltpu.VMEM((c, c), BF16), pltpu.VMEM((c, c), BF16), pltpu.VMEM((2 * l, c), BF16)],
        compiler_params=_cparams(("arbitrary",)),
        name="fourier_mix",
    )(p, w, jnp.asarray(cd, F32), jnp.asarray(sd, F32), cs)


HY_CBLK = 256
HY_FBLK = 512


def _hyena_dft_matrix(l):
    k = np.arange(l, dtype=np.int64)
    ang = (np.pi / l) * ((k[:, None] * k[None, :]) % (2 * l))
    sn = np.sin(ang)
    sn[0, :] = 1.0 - 2.0 * (k % 2)
    return jnp.asarray(np.concatenate([np.cos(ang), sn], axis=0), BF16)


def _hyena_pos_features(l):
    t = np.linspace(0.0, 1.0, l)[:, None]
    w = (2.0 * np.pi / l) * np.arange(l)[:, None]
    f = np.linspace(1e-4, HY_BANDS - 1, HY_BANDS)[None, :]
    z = np.concatenate([t, np.cos(f * w), -np.sin(f * w)], axis=-1)
    return np.pad(z, ((0, 0), (0, LANES - z.shape[1])))


def _split_bf16(x):
    hi = x.astype(BF16)
    return hi, (x - hi.astype(F32)).astype(BF16)


def _hyena_filter_kernel(z_ref, w1_ref, b1_ref, w2_ref, b2_ref, fr_ref, w3f_ref, w3b_ref, dl_ref, m_ref,
                         p_ref, q_ref):
    l = z_ref.shape[0]
    hp = lax.Precision.HIGHEST
    z = z_ref[...]
    h = jnp.sin(fr_ref[0:1, :] * (jnp.dot(z, w1_ref[...], precision=hp, preferred_element_type=F32)
                                  + b1_ref[...]))
    h = jnp.sin(fr_ref[1:2, :] * (jnp.dot(h, w2_ref[...], precision=hp, preferred_element_type=F32)
                                  + b2_ref[...]))
    window = jnp.exp(-z[:, 0:1] * dl_ref[...]) + HY_DECAY_SHIFT
    hf = jnp.dot(h, w3f_ref[...], precision=hp, preferred_element_type=F32) * window
    hb = jnp.dot(h, w3b_ref[...], precision=hp, preferred_element_type=F32) * window
    norm = jnp.sum(jnp.abs(hf) + jnp.abs(hb), axis=0, keepdims=True) + EPS
    hf = hf / norm
    hb = hb / norm
    g1h, g1l = _split_bf16(hf + hb)
    g2h, g2l = _split_bf16(hb - hf)
    f1 = (jnp.dot(m_ref[...], g1h, preferred_element_type=F32)
          + jnp.dot(m_ref[...], g1l, preferred_element_type=F32))
    f2 = (jnp.dot(m_ref[l:2 * l, :], g2h, preferred_element_type=F32)
          + jnp.dot(m_ref[l:2 * l, :], g2l, preferred_element_type=F32))
    p_ref[...] = f1[0:l]
    row = lax.broadcasted_iota(jnp.int32, f2.shape, 0)
    q_ref[...] = jnp.where(row == 0, f1[l:l + 1], f2)


def hyena_filter_spectrum(l, w1, b1, w2, b2, w3, freq, m):
    c = w3.shape[1] // 2
    hid = w1.shape[1]
    z = jnp.asarray(_hyena_pos_features(l), F32)
    w1p = jnp.pad(w1, ((0, z.shape[1] - w1.shape[0]), (0, 0)))
    deltas = np.abs(np.linspace(math.log(HY_DECAY_TARGET) / HY_SLOW_PCT,
                                math.log(HY_DECAY_TARGET) / HY_FAST_PCT, c))[None, :]
    nb = c // HY_CBLK
    full = lambda shape: pl.BlockSpec(shape, lambda j: (0,) * len(shape))
    return pl.pallas_call(
        _hyena_filter_kernel,
        grid=(nb,),
        in_specs=[full(z.shape), full(w1p.shape), full((1, hid)), full(w2.shape), full((1, hid)),
                  full((2, hid)),
                  pl.BlockSpec((hid, HY_CBLK), lambda j: (0, j)),
                  pl.BlockSpec((hid, HY_CBLK), lambda j: (0, nb + j)),
                  pl.BlockSpec((1, HY_CBLK), lambda j: (0, j)),
                  pl.BlockSpec(m.shape, lambda j: (0, 0), pipeline_mode=pl.Buffered(1))],
        out_specs=[pl.BlockSpec((l, HY_CBLK), lambda j: (0, j))] * 2,
        out_shape=[jax.ShapeDtypeStruct((l, c), F32)] * 2,
        compiler_params=_cparams(("arbitrary",)),
        name="hyena_filter",
    )(z, w1p, b1.reshape(1, hid), w2, b2.reshape(1, hid), freq, w3, w3, jnp.asarray(deltas, F32), m)


def _shift_rows(u, down):
    l = u.shape[0]
    row = lax.broadcasted_iota(jnp.int32, u.shape, 0)
    if down:
        return jnp.where(row == 0, 0.0, pltpu.roll(u, 1, 0))
    return jnp.where(row == l - 1, 0.0, pltpu.roll(u, l - 1, 0))


def _dwconv3(u, taps):
    return (taps[3:4] + _shift_rows(u, True) * taps[0:1] + u * taps[1:2] + _shift_rows(u, False) * taps[2:3])


def _hyena_conv_kernel(x0_ref, x1_ref, v_ref, taps_ref, bias_ref, p_ref, q_ref, m_ref, o_ref):
    l, c = x0_ref.shape
    x1 = _dwconv3(x1_ref[...], taps_ref[1])
    s = _dwconv3(v_ref[...], taps_ref[2]) * x1
    sb = s.astype(BF16)
    fb = min(HY_FBLK, l)
    nblk = l // fb
    inv_l = 1.0 / l

    def forward(i):
        return (jnp.dot(m_ref[i * fb:(i + 1) * fb, :], sb, preferred_element_type=F32),
                jnp.dot(m_ref[l + i * fb:l + (i + 1) * fb, :], sb, preferred_element_type=F32))

    y_cos_acc = y_sin_acc = nyquist = None
    ab = forward(0)
    x0 = _dwconv3(x0_ref[...], taps_ref[0])
    for i in range(nblk):
        a, b = ab
        if i + 1 < nblk:
            ab = forward(i + 1)
        p, q = p_ref[i * fb:(i + 1) * fb, :], q_ref[i * fb:(i + 1) * fb, :]
        bq = b * q
        if i == 0:
            first = lax.broadcasted_iota(jnp.int32, (fb, c), 0) == 0
            nyquist = bq[0:1] * (0.5 * inv_l)
            y_cos = (a * p + jnp.where(first, 0.0, bq)) * jnp.where(first, 0.5 * inv_l, inv_l)
            y_sin = jnp.where(first, 0.0, (b * p - a * q) * inv_l)
        else:
            y_cos = (a * p + bq) * inv_l
            y_sin = (b * p - a * q) * inv_l
        d_cos = jnp.dot(m_ref[0:l, i * fb:(i + 1) * fb], y_cos.astype(BF16), preferred_element_type=F32)
        d_sin = jnp.dot(m_ref[l:2 * l, i * fb:(i + 1) * fb], y_sin.astype(BF16), preferred_element_type=F32)
        y_cos_acc = d_cos if y_cos_acc is None else y_cos_acc + d_cos
        y_sin_acc = d_sin if y_sin_acc is None else y_sin_acc + d_sin
    row = lax.broadcasted_iota(jnp.int32, (l, c), 0)
    y = (y_cos_acc + jnp.where(row == 0, 0.0, y_sin_acc)
         + jnp.where(row % 2 == 0, 1.0, -1.0) * nyquist)
    o_ref[...] = ((y + s * bias_ref[...]) * x0).astype(o_ref.dtype)


def hyena_conv(p, col, conv_w, conv_b, bias, spec_p, spec_q, m):
    bsz, l, _ = p.shape
    c = bias.shape[-1]
    nb = c // HY_CBLK
    cb0 = col // HY_CBLK
    taps = jnp.concatenate([conv_w, conv_b[None]], axis=0)
    taps = taps.reshape(4, 3, nb, HY_CBLK).transpose(2, 1, 0, 3)

    def part(k):
        return pl.BlockSpec((None, l, HY_CBLK), lambda j, b: (b, 0, cb0 + k * nb + j))

    return pl.pallas_call(
        _hyena_conv_kernel,
        grid=(nb, bsz),
        in_specs=[part(0), part(1), part(2),
                  pl.BlockSpec((None, 3, 4, HY_CBLK), lambda j, b: (j, 0, 0, 0)),
                  pl.BlockSpec((1, HY_CBLK), lambda j, b: (0, j)),
                  pl.BlockSpec((l, HY_CBLK), lambda j, b: (0, j)),
                  pl.BlockSpec((l, HY_CBLK), lambda j, b: (0, j)),
                  pl.BlockSpec(m.shape, lambda j, b: (0, 0), pipeline_mode=pl.Buffered(1))],
        out_specs=pl.BlockSpec((None, l, HY_CBLK), lambda j, b: (b, 0, j)),
        out_shape=jax.ShapeDtypeStruct((bsz, l, c), BF16),
        compiler_params=_cparams(("arbitrary", "arbitrary")),
        name="hyena_conv",
    )(p, p, p, taps, bias.reshape(1, c), spec_p, spec_q, m)


GLA_HP = 2
GLA_ROPE_PAIR = 16
GLA_SCAN_UNROLL = 8


def _gla_rope_tables(l, dk):
    half = dk // 2
    nf = half // 2
    assert nf == GLA_ROPE_PAIR
    inv = ROPE_THETA ** (-np.arange(nf, dtype=np.float64) / nf)
    t = np.arange(l)
    ang_r = (t // GRID_W)[:, None] * inv
    ang_c = (t % GRID_W)[:, None] * inv
    cos = np.concatenate([np.cos(ang_r)] * 2 + [np.cos(ang_c)] * 2, axis=1)
    sin = np.concatenate([-np.sin(ang_r), np.sin(ang_r), -np.sin(ang_c), np.sin(ang_c)], axis=1)
    return (jnp.asarray(np.tile(cos, (1, GLA_HP)), F32), jnp.asarray(np.tile(sin, (1, GLA_HP)), F32))


def _rope(x, cos, sin):
    lane = lax.broadcasted_iota(jnp.int32, x.shape, 1)
    lanes = x.shape[1]
    partner = jnp.where(lane % (2 * GLA_ROPE_PAIR) < GLA_ROPE_PAIR,
                        pltpu.roll(x, lanes - GLA_ROPE_PAIR, 1), pltpu.roll(x, GLA_ROPE_PAIR, 1))
    return x * cos + partner * sin


def _log_sigmoid(x):
    return jnp.minimum(x, 0.0) - jnp.log(1.0 + jnp.exp(-jnp.abs(x)))


def _gla_kernel(*refs, with_ctx_out):
    (q_ref, k_ref, v_ref, r_ref, z_ref, cq_ref, ck_ref, cv_ref, cr_ref, cz_ref,
     wz_ref, bz_ref, g_ref, cos_ref, sin_ref) = refs[:15]
    if with_ctx_out:
        o_ref, oc_ref = refs[15:17]
        scratch = refs[17:]
    else:
        o_ref, oc_ref = refs[15], None
        scratch = refs[16:]
    qs, ks, las, ofs, obs, cqs, cks, clas, cofs, cobs, st_f, st_b = scratch
    dk2 = q_ref.shape[1]
    dv2 = v_ref.shape[1]
    dk, dv = dk2 // GLA_HP, dv2 // GLA_HP
    ch = GLA_CHUNK
    hp = lax.Precision.HIGHEST

    def gates(z):
        pre = jnp.dot(z, wz_ref[...], precision=hp, preferred_element_type=F32) + bz_ref[...]
        return _log_sigmoid(pre) * (1.0 / GLA_TAU)

    qs[...] = _rope(q_ref[...] * dk ** -0.5, cos_ref[...], sin_ref[...])
    ks[...] = _rope(k_ref[...], cos_ref[...], sin_ref[...])
    las[...] = gates(z_ref[...])
    cqs[...] = cq_ref[...] * dk ** -0.5
    cks[...] = ck_ref[...]
    clas[...] = gates(cz_ref[...])
    st_f[...] = jnp.zeros_like(st_f)
    st_b[...] = jnp.zeros_like(st_b)

    ri = lax.broadcasted_iota(jnp.int32, (ch, ch), 0)
    ci = lax.broadcasted_iota(jnp.int32, (ch, ch), 1)
    tri = {False: ri >= ci, True: ri <= ci}
    tri_b16 = {d: jnp.where(m, 1.0, 0.0).astype(BF16) for d, m in tri.items()}
    tri2 = {d: jnp.concatenate([m] * GLA_HP, axis=0) for d, m in tri.items()}
    lane_head = lax.broadcasted_iota(jnp.int32, (ch, dk2), 1) // dk
    st_r = lax.broadcasted_iota(jnp.int32, (dv2, dk2), 0) // dv
    st_c = lax.broadcasted_iota(jnp.int32, (dv2, dk2), 1) // dk
    st_diag = st_r == st_c

    def scan(q_s, k_s, la_s, v_in, of_s, ob_s, n):
        unroll = min(GLA_SCAN_UNROLL, n)

        def body(trip, carry):
            items = []
            for u in range(unroll):
                c = trip * unroll + u
                items.append((False, pl.ds(pl.multiple_of(c * ch, ch), ch)))
                items.append((True, pl.ds(pl.multiple_of((n - 1 - c) * ch, ch), ch)))
            v_c = [v_in[rows, :].astype(BF16) for _, rows in items]
            cum = []
            for bw, rows in items:
                la_hi, la_lo = _split_bf16(la_s[rows, dk2:2 * dk2] if bw else la_s[rows, 0:dk2])
                cum.append(jnp.dot(tri_b16[bw], la_hi, preferred_element_type=F32)
                           + jnp.dot(tri_b16[bw], la_lo, preferred_element_type=F32))
            q_dec, k_end, decay, sc = [], [], [], []
            for i, (bw, rows) in enumerate(items):
                q_c, k_c = q_s[rows, :], k_s[rows, :]
                tot = cum[i][0:1] if bw else cum[i][ch - 1:ch]
                qd = q_c * jnp.exp(cum[i])
                k_inv = (k_c * jnp.exp(-cum[i])).astype(BF16)
                k_end.append((k_c * jnp.exp(tot - cum[i])).astype(BF16))
                decay.append(jnp.exp(tot))
                q_heads = jnp.concatenate([jnp.where(lane_head == h, qd, 0.0) for h in range(GLA_HP)], axis=0)
                sc.append(_dot_nt(q_heads.astype(BF16), k_inv))
                q_dec.append(qd.astype(BF16))
            o_intra, ds_t = [], []
            for i, (bw, rows) in enumerate(items):
                pv = jnp.dot(jnp.where(tri2[bw], sc[i], 0.0).astype(BF16), v_c[i],
                             preferred_element_type=F32)
                o_intra.append(jnp.concatenate(
                    [pv[h * ch:(h + 1) * ch, h * dv:(h + 1) * dv] for h in range(GLA_HP)], axis=1))
                ds_t.append(lax.dot_general(v_c[i], k_end[i], (((0,), (0,)), ((), ())),
                                            preferred_element_type=F32))
            for i, (bw, rows) in enumerate(items):
                st, o_s = (st_b, ob_s) if bw else (st_f, of_s)
                s_t = st[...]
                o_s[rows, :] = o_intra[i] + _dot_nt(q_dec[i], s_t.astype(BF16))
                st[...] = s_t * decay[i] + jnp.where(st_diag, ds_t[i], 0.0)
            return carry

        lax.fori_loop(0, n // unroll, body, 0)

    def finish(of_s, ob_s, gate_ref, out_ref, n_blocks, blk):
        def body(i, carry):
            rows = pl.ds(pl.multiple_of(i * blk, blk), blk)
            o = of_s[rows, :] + ob_s[rows, :]
            gate = gate_ref[rows, :]
            parts = []
            for h in range(GLA_HP):
                oh = o[:, h * dv:(h + 1) * dv]
                gh = gate[:, h * dv:(h + 1) * dv]
                parts.append(_rms(oh, g_ref[...]) * (gh * jax.nn.sigmoid(gh)))
            out_ref[rows, :] = jnp.concatenate(parts, axis=1).astype(out_ref.dtype)
            return carry
        lax.fori_loop(0, n_blocks, body, 0)

    l, lc = q_ref.shape[0], cq_ref.shape[0]
    scan(cqs, cks, clas, cv_ref, cofs, cobs, lc // ch)
    if with_ctx_out:
        finish(cofs, cobs, cr_ref, oc_ref, 1, lc)
    scan(qs, ks, las, v_ref, ofs, obs, l // ch)
    finish(ofs, obs, r_ref, o_ref, l // lc, lc)


def gla(px, pxz, pc, pcz, w_gate, b_gate, out_gain, with_ctx_out):
    bsz, l, _ = px.shape
    lc = pc.shape[1]
    dv = out_gain.shape[-1]
    hdk = w_gate.shape[-1]
    dk = hdk // GLA_HEADS
    dk2, dv2 = GLA_HP * dk, GLA_HP * dv
    nhp = GLA_HEADS // GLA_HP
    zw = pxz.shape[-1]
    rank = w_gate.shape[1]
    wz = jnp.zeros((nhp, zw, 2 * dk2), F32)
    for u in range(2):
        blk = w_gate[u].reshape(rank, nhp, dk2).transpose(1, 0, 2)
        wz = wz.at[:, u * rank:(u + 1) * rank, u * dk2:(u + 1) * dk2].set(blk)
    bz = b_gate.reshape(2, nhp, dk2).transpose(1, 0, 2).reshape(nhp, 1, 2 * dk2)
    cos, sin = _gla_rope_tables(l, dk)
    k_cb, v_cb, r_cb = hdk // dk2, 2 * hdk // dv2, (2 * hdk + GLA_HEADS * dv) // dv2

    def col(rows, width, cb):
        return pl.BlockSpec((None, rows, width), lambda b, j: (b, 0, cb + j))

    def whole(rows, width):
        return pl.BlockSpec((None, rows, width), lambda b, j: (b, 0, 0))

    const = lambda shape: pl.BlockSpec(shape, lambda b, j: (0,) * len(shape))
    in_specs = [col(l, dk2, 0), col(l, dk2, k_cb), col(l, dv2, v_cb), col(l, dv2, r_cb), whole(l, zw),
                col(lc, dk2, 0), col(lc, dk2, k_cb), col(lc, dv2, v_cb), col(lc, dv2, r_cb), whole(lc, zw),
                pl.BlockSpec((None, zw, 2 * dk2), lambda b, j: (j, 0, 0)),
                pl.BlockSpec((None, 1, 2 * dk2), lambda b, j: (j, 0, 0)),
                const((1, dv)), const((l, dk2)), const((l, dk2))]
    out_specs = [pl.BlockSpec((None, l, dv2), lambda b, j: (b, 0, j))]
    out_shape = [jax.ShapeDtypeStruct((bsz, l, GLA_HEADS * dv), BF16)]
    if with_ctx_out:
        out_specs.append(pl.BlockSpec((None, lc, dv2), lambda b, j: (b, 0, j)))
        out_shape.append(jax.ShapeDtypeStruct((bsz, lc, GLA_HEADS * dv), BF16))
    scratch = [pltpu.VMEM((l, dk2), F32), pltpu.VMEM((l, dk2), F32), pltpu.VMEM((l, 2 * dk2), F32),
               pltpu.VMEM((l, dv2), F32), pltpu.VMEM((l, dv2), F32),
               pltpu.VMEM((lc, dk2), F32), pltpu.VMEM((lc, dk2), F32), pltpu.VMEM((lc, 2 * dk2), F32),
               pltpu.VMEM((lc, dv2), F32), pltpu.VMEM((lc, dv2), F32),
               pltpu.VMEM((dv2, dk2), F32), pltpu.VMEM((dv2, dk2), F32)]
    outs = pl.pallas_call(
        functools.partial(_gla_kernel, with_ctx_out=with_ctx_out),
        grid=(bsz, nhp),
        in_specs=in_specs, out_specs=out_specs, out_shape=out_shape, scratch_shapes=scratch,
        compiler_params=_cparams(("arbitrary", "arbitrary")),
        name="gla",
    )(px, px, px, px, pxz, pc, pc, pc, pc, pcz, wz, bz, out_gain.reshape(1, dv), cos, sin)
    return outs if with_ctx_out else (outs[0], None)


def kernel(x, c, ctx, c_ctx, w_mod, b_mod, g_mix, w_in, gla_gate_w, gla_gate_b, gla_out_g,
           hy_conv_w, hy_conv_b, hy_w1, hy_b1, hy_w2, hy_b2, hy_w3, hy_freq, hy_bias, fn_w,
           na_q_g, na_k_g, na_rpb, w_out, g_ffn, ffn_w_up, ffn_conv_w, ffn_conv_b, ffn_w_down):
    bsz, l_lat, d = x.shape
    l_ctx = ctx.shape[1]
    depth = w_mod.shape[0]
    w = d // 4

    n_cond = -(-(bsz + 1) // SUBLANES) * SUBLANES
    cond = jnp.zeros((n_cond, d), F32).at[:bsz].set(c).at[bsz].set(c_ctx)
    mods = mod_vectors(cond, w_mod, b_mod).reshape(depth, n_cond, N_MOD, d)
    mods = jnp.pad(mods, ((0, 0), (0, 0), (0, MOD_ROWS - N_MOD), (0, 0)))

    z0 = 3 * w
    zw = 2 * GLA_GATE_RANK
    w_main = jnp.concatenate([w_in[:, :, :z0], w_in[:, :, z0 + zw:]], axis=-1).astype(BF16)
    w_z = jnp.pad(w_in[:, :, z0:z0 + zw], ((0, 0), (0, 0), (0, LANES - zw))).astype(BF16)
    w_out_b = w_out.astype(BF16)
    w_up_b = ffn_w_up.astype(BF16)
    w_down_b = ffn_w_down.astype(BF16)

    m_lat = _hyena_dft_matrix(l_lat)
    m_ctx = _hyena_dft_matrix(l_ctx)
    ctx_flat = ctx.reshape(1, bsz * l_ctx, d)
    for layer in range(depth):
        last = layer == depth - 1
        mod_x = mods[layer, :bsz]
        mod_c = mods[layer, bsz:bsz + 1]
        g_mix_l = g_mix[layer].reshape(1, d)
        g_ffn_l = g_ffn[layer].reshape(1, d)

        px, pxz = in_proj(x, mod_x, g_mix_l, w_main, w_z, layer, tm=1024)
        pc, pcz = in_proj(ctx_flat, mod_c, g_mix_l, w_main, w_z, layer, tm=1024)
        pc = pc.reshape(bsz, l_ctx, -1)
        pcz = pcz.reshape(bsz, l_ctx, -1)

        y_a, yc_a = gla(px, pxz, pc, pcz, gla_gate_w[layer], gla_gate_b[layer], gla_out_g[layer],
                        with_ctx_out=not last)
        hy_filt = (hy_w1[layer], hy_b1[layer], hy_w2[layer], hy_b2[layer], hy_w3[layer], hy_freq[layer])
        hy_p, hy_q = hyena_filter_spectrum(l_lat, *hy_filt, m_lat)
        y_b = hyena_conv(px, 3 * w, hy_conv_w[layer], hy_conv_b[layer], hy_bias[layer], hy_p, hy_q, m_lat)
        y_c = fourier_mix(px, 6 * w, fn_w[layer])
        na_bias = natten_bias_table(na_rpb[layer], l_lat // GRID_W)
        y_d, yc_d = natten(px, pc, 7 * w, na_q_g[layer].reshape(1, -1), na_k_g[layer].reshape(1, -1),
                           na_bias, with_ctx_out=not last)
        x = out_proj([y_a, y_b, y_c, y_d], w_out_b, layer, x, mod_x, tm=1024)
        cw = ffn_conv_w[layer]
        cb = ffn_conv_b[layer].reshape(1, -1)
        x = conv_ffn(x, mod_x, g_ffn_l, w_up_b, cw, cb, w_down_b, layer, tm=1024)

        if not last:
            hc_p, hc_q = hyena_filter_spectrum(l_ctx, *hy_filt, m_ctx)
            yc_b = hyena_conv(pc, 3 * w, hy_conv_w[layer], hy_conv_b[layer], hy_bias[layer], hc_p, hc_q, m_ctx)
            yc_c = fourier_mix(pc, 6 * w, fn_w[layer])
            ycs = [y.reshape(1, bsz * l_ctx, w) for y in (yc_a, yc_b, yc_c, yc_d)]
            ctx_flat = out_proj(ycs, w_out_b, layer, ctx_flat, mod_c, tm=1024)
            ctx_flat = conv_ffn(ctx_flat, mod_c, g_ffn_l, w_up_b, cw, cb, w_down_b, layer, tm=1024,
                                seg_len=l_ctx)
    return x
```

```python
import functools
import math

import numpy as np
import jax
import jax.numpy as jnp
from jax import lax
from jax.experimental import pallas as pl
from jax.experimental.pallas import tpu as pltpu

F32 = jnp.float32
BF16 = jnp.bfloat16

GRID_W = 64
GLA_HEADS = 4
GLA_GATE_RANK = 16
GLA_TAU = 16.0
GLA_CHUNK = 64
HY_BANDS = 16
HY_DECAY_TARGET = 1e-2
HY_FAST_PCT = 0.3
HY_SLOW_PCT = 1.5
HY_DECAY_SHIFT = 0.05
FN_GROUPS = 4
NA_HEADS = 4
NA_KR_MAX = 8
NA_KC = 16
ROPE_THETA = 10000.0
N_MOD = 6
EPS = 1e-6

V7X_VMEM_LIMIT = 58 * 1024 * 1024
SUBLANES = 8
LANES = 128
MOD_ROWS = 8
FFN_HALO = 16


def _cparams(sem):
    return pltpu.CompilerParams(dimension_semantics=sem, vmem_limit_bytes=V7X_VMEM_LIMIT)


def _mod_kernel(s_ref, w_ref, b_ref, o_ref):
    s = s_ref[...]
    s = s * jax.nn.sigmoid(s)
    o_ref[...] = jnp.dot(s.astype(BF16), w_ref[...].astype(BF16),
                         preferred_element_type=F32) + b_ref[...]


def mod_vectors(cond, w_mod, b_mod, tn=1024):
    depth, d, n = w_mod.shape
    r = cond.shape[0]
    return pl.pallas_call(
        _mod_kernel,
        grid=(depth, n // tn),
        in_specs=[pl.BlockSpec((r, d), lambda l, j: (0, 0)),
                  pl.BlockSpec((None, d, tn), lambda l, j: (l, 0, j)),
                  pl.BlockSpec((None, 1, tn), lambda l, j: (l, 0, j))],
        out_specs=pl.BlockSpec((None, r, tn), lambda l, j: (l, 0, j)),
        out_shape=jax.ShapeDtypeStruct((depth, r, n), F32),
        compiler_params=_cparams(("arbitrary", "arbitrary")),
        name="mod_vectors",
    )(cond, w_mod, b_mod.reshape(depth, 1, n))


def _norm_mod(x, gain, shift, scale):
    ms = jnp.mean(x * x, axis=-1, keepdims=True)
    return (x * lax.rsqrt(ms + EPS) * gain) * (1.0 + scale) + shift


def _inproj_kernel(x_ref, mod_ref, g_ref, w_ref, wz_ref, o_ref, oz_ref, h_ref):
    j = pl.program_id(2)
    tn = o_ref.shape[1]

    @pl.when(j == 0)
    def _():
        h = _norm_mod(x_ref[...], g_ref[...], mod_ref[0:1, :], mod_ref[1:2, :]).astype(BF16)
        h_ref[...] = h
        oz_ref[...] = jnp.dot(h, wz_ref[...], preferred_element_type=F32)

    w = w_ref[:, pl.ds(pl.multiple_of(j * tn, tn), tn)]
    o_ref[...] = jnp.dot(h_ref[...], w, preferred_element_type=F32)


def in_proj(x, mod, gain, w_main, w_z, layer, tm, tn=1024):
    g, l, d = x.shape
    n = w_main.shape[-1]
    nz = w_z.shape[-1]
    nt = l // tm

    def x_tile(b, i, j):
        t = jnp.minimum(b * nt + i + jnp.minimum(j, 1), g * nt - 1)
        return (t // nt, t % nt, 0)

    return pl.pallas_call(
        _inproj_kernel,
        grid=(g, l // tm, n // tn),
        in_specs=[pl.BlockSpec((None, tm, d), x_tile),
                  pl.BlockSpec((None, MOD_ROWS, d), lambda b, i, j: (b, 0, 0)),
                  pl.BlockSpec((1, d), lambda b, i, j: (0, 0)),
                  pl.BlockSpec((None, d, n), lambda b, i, j: (layer, 0, 0), pipeline_mode=pl.Buffered(1)),
                  pl.BlockSpec((None, d, nz), lambda b, i, j: (layer, 0, 0))],
        out_specs=[pl.BlockSpec((None, tm, tn), lambda b, i, j: (b, i, j)),
                   pl.BlockSpec((None, tm, nz), lambda b, i, j: (b, i, 0))],
        out_shape=[jax.ShapeDtypeStruct((g, l, n), F32),
                   jax.ShapeDtypeStruct((g, l, nz), F32)],
        scratch_shapes=[pltpu.VMEM((tm, d), BF16)],
        compiler_params=_cparams(("arbitrary", "arbitrary", "arbitrary")),
        name="in_proj",
    )(x, mod, gain, w_main, w_z)


def _outproj_kernel(ya_ref, yb_ref, yc_ref, yd_ref, w_ref, x_ref, mod_ref, o_ref):
    kw = ya_ref.shape[-1]
    acc = jnp.dot(ya_ref[...], w_ref[0 * kw:1 * kw, :], preferred_element_type=F32)
    acc += jnp.dot(yb_ref[...], w_ref[1 * kw:2 * kw, :], preferred_element_type=F32)
    acc += jnp.dot(yc_ref[...], w_ref[2 * kw:3 * kw, :], preferred_element_type=F32)
    acc += jnp.dot(yd_ref[...], w_ref[3 * kw:4 * kw, :], preferred_element_type=F32)
    o_ref[...] = x_ref[...] + mod_ref[2:3, :] * acc


def out_proj(ys, w_out, layer, x, mod, tm):
    g, l, d = x.shape
    kw = ys[0].shape[-1]
    yspec = pl.BlockSpec((None, tm, kw), lambda b, i: (b, i, 0))
    return pl.pallas_call(
        _outproj_kernel,
        grid=(g, l // tm),
        in_specs=[yspec, yspec, yspec, yspec,
                  pl.BlockSpec((None, 4 * kw, d), lambda b, i: (layer, 0, 0), pipeline_mode=pl.Buffered(1)),
                  pl.BlockSpec((None, tm, d), lambda b, i: (b, i, 0)),
                  pl.BlockSpec((None, MOD_ROWS, d), lambda b, i: (b, 0, 0))],
        out_specs=pl.BlockSpec((None, tm, d), lambda b, i: (b, i, 0)),
        out_shape=jax.ShapeDtypeStruct((g, l, d), F32),
        compiler_params=_cparams(("arbitrary", "arbitrary")),
        name="out_proj",
    )(*ys, w_out, x, mod)


def _ffn_kernel(x_ref, xp_ref, xn_ref, mod_ref, g_ref, wa_ref, wg_ref, cw_ref, cb_ref, wd_ref,
                o_ref, h_ref, act0_ref, act1_ref, *, seg_len, n_hidden_tiles):
    i = pl.program_id(1)
    j = pl.program_id(2)
    nf = n_hidden_tiles
    acts = (act0_ref, act1_ref)
    tm = x_ref.shape[0]
    tf = wa_ref.shape[1]
    hs = FFN_HALO

    def prologue():
        gain, shift, scale = g_ref[...], mod_ref[3:4, :], mod_ref[4:5, :]
        h_ref[hs:hs + tm, :] = _norm_mod(x_ref[...], gain, shift, scale).astype(BF16)
        hp = jnp.where(i > 0, _norm_mod(xp_ref[...], gain, shift, scale), 0.0)
        hn = jnp.where(i < pl.num_programs(1) - 1, _norm_mod(xn_ref[...], gain, shift, scale), 0.0)
        zero = jnp.zeros_like(hp)
        h_ref[0:hs, :] = jnp.concatenate([zero, hp], axis=0).astype(BF16)
        h_ref[hs + tm:2 * hs + tm, :] = jnp.concatenate([hn, zero], axis=0).astype(BF16)
        o_ref[...] = x_ref[...]

    def conv(u, c0, width):
        w = cw_ref[:, c0:c0 + width]
        prev, nxt = u[hs - 1:hs - 1 + tm], u[hs + 1:hs + 1 + tm]
        if seg_len is not None:
            pos = lax.broadcasted_iota(jnp.int32, prev.shape, 0) % seg_len
            prev = jnp.where(pos == 0, 0.0, prev)
            nxt = jnp.where(pos == seg_len - 1, 0.0, nxt)
        return cb_ref[:, c0:c0 + width] + prev * w[0:1] + u[hs:hs + tm] * w[1:2] + nxt * w[2:3]

    def up_matmuls():
        h = h_ref[...]
        ug = jnp.dot(h, wg_ref[...], preferred_element_type=F32)
        ua = jnp.dot(h, wa_ref[...], preferred_element_type=F32)
        return ua, ug

    def gate_to(act_ref, ua, ug):
        gt = conv(ug, tf, tf)
        swish = gt * jax.nn.sigmoid(gt)
        act_ref[...] = (conv(ua, 0, tf) * swish).astype(BF16)

    def down_from(act_ref):
        o_ref[...] += mod_ref[5:6, :] * jnp.dot(act_ref[...], wd_ref[...], preferred_element_type=F32)

    middle = jnp.logical_and(j > 0, j < nf)

    @pl.when(j == 0)
    def _():
        prologue()
        ua, ug = up_matmuls()
        gate_to(acts[0], ua, ug)

    for parity in range(2):
        @pl.when(jnp.logical_and(middle, j % 2 == parity))
        def _():
            ua, ug = up_matmuls()
            down_from(acts[1 - parity])
            gate_to(acts[parity], ua, ug)

    @pl.when(j == nf)
    def _():
        down_from(acts[(n_hidden_tiles - 1) % 2])


def conv_ffn(x, mod, gain, w_up, conv_w, conv_b, w_down, layer, tm, tf=512, seg_len=None):
    g, l, d = x.shape
    assert seg_len is None or (tm % seg_len == 0 and l % tm == 0)
    f = w_down.shape[1]
    nf = f // tf
    nb = tm // SUBLANES
    last = l // SUBLANES - 1
    cw = conv_w.reshape(3, 2, nf, tf).transpose(2, 0, 1, 3).reshape(nf, 3, 2 * tf)
    cb = conv_b.reshape(1, 2, nf, tf).transpose(2, 0, 1, 3).reshape(nf, 1, 2 * tf)
    nt = l // tm

    def x_tile(b, i, j):
        n = jnp.minimum(b * nt + i + jnp.minimum(j, 1), g * nt - 1)
        return (n // nt, n % nt, 0)

    def up_tile(j):
        return jnp.where(j < nf, j, 0)

    def down_tile(j):
        return jnp.where(j == 0, nf - 1, j - 1)

    return pl.pallas_call(
        functools.partial(_ffn_kernel, seg_len=seg_len, n_hidden_tiles=nf),
        grid=(g, l // tm, nf + 1),
        in_specs=[pl.BlockSpec((None, tm, d), x_tile),
                  pl.BlockSpec((None, SUBLANES, d), lambda b, i, j: (b, jnp.maximum(i * nb - 1, 0), 0)),
                  pl.BlockSpec((None, SUBLANES, d), lambda b, i, j: (b, jnp.minimum((i + 1) * nb, last), 0)),
                  pl.BlockSpec((None, MOD_ROWS, d), lambda b, i, j: (b, 0, 0)),
                  pl.BlockSpec((1, d), lambda b, i, j: (0, 0)),
                  pl.BlockSpec((None, d, tf), lambda b, i, j: (layer, 0, up_tile(j))),
                  pl.BlockSpec((None, d, tf), lambda b, i, j: (layer, 0, nf + up_tile(j))),
                  pl.BlockSpec((None, 3, 2 * tf), lambda b, i, j: (up_tile(j), 0, 0)),
                  pl.BlockSpec((None, 1, 2 * tf), lambda b, i, j: (up_tile(j), 0, 0)),
                  pl.BlockSpec((None, tf, d), lambda b, i, j: (layer, down_tile(j), 0))],
        out_specs=pl.BlockSpec((None, tm, d), lambda b, i, j: (b, i, 0)),
        out_shape=jax.ShapeDtypeStruct((g, l, d), F32),
        scratch_shapes=[pltpu.VMEM((tm + 2 * FFN_HALO, d), BF16),
                        pltpu.VMEM((tm, tf), BF16), pltpu.VMEM((tm, tf), BF16)],
        compiler_params=_cparams(("arbitrary", "arbitrary", "arbitrary")),
        name="conv_ffn",
    )(x, x, x, mod, gain, w_up, w_up, cw, cb, w_down)


def natten_bias_table(rpb, n_rows):
    kr = min(NA_KR_MAX, n_rows)
    h, n_dr, n_dc = rpb.shape
    col = np.arange(GRID_W)
    col0 = np.clip(col - NA_KC // 2, 0, GRID_W - NA_KC)
    in_win = (col[None, :] >= col0[:, None]) & (col[None, :] < col0[:, None] + NA_KC)
    lo = GRID_W - NA_KC
    ext = jnp.pad(rpb.astype(F32), ((0, 0), (0, 0), (lo, 2 * GRID_W - lo - n_dc)))
    skew = jnp.tile(ext, (1, 1, GRID_W))[:, :, :GRID_W * (2 * GRID_W - 1)]
    skew = skew.reshape(h, n_dr, GRID_W, 2 * GRID_W - 1)[:, :, :, GRID_W - 1:]
    skew = jnp.where(in_win[None, None], skew, -jnp.inf)
    tab = jnp.stack([skew[:, NA_KR_MAX - 1 - s:NA_KR_MAX - 1 - s + kr] for s in range(kr)], axis=1)
    return tab.transpose(0, 1, 3, 2, 4).reshape(h, kr, GRID_W, kr * GRID_W)


NA_ROW_GROUP = 32


def _rms(x, gain):
    return x * lax.rsqrt(jnp.mean(x * x, axis=-1, keepdims=True) + EPS) * gain


def _dot_nt(a, b):
    return lax.dot_general(a, b, (((1,), (1,)), ((), ())), preferred_element_type=F32)


def _natten_kernel(*refs, n_rows, with_ctx_out):
    if with_ctx_out:
        (q_ref, k_ref, v_ref, kc_ref, vc_ref, gq_ref, gk_ref, bias_ref, qc_ref,
         o_ref, oc_ref, qs, ks, vs) = refs
    else:
        q_ref, k_ref, v_ref, kc_ref, vc_ref, gq_ref, gk_ref, bias_ref, o_ref, qs, ks, vs = refs
    dh = q_ref.shape[-1]
    kr = min(NA_KR_MAX, n_rows)
    scale = dh ** -0.5
    qs[...] = (_rms(q_ref[...], gq_ref[...]) * scale).astype(BF16)
    ks[...] = _rms(k_ref[...], gk_ref[...]).astype(BF16)
    vs[...] = v_ref[...].astype(BF16)
    kc = _rms(kc_ref[...], gk_ref[...]).astype(BF16)
    vc = vc_ref[...].astype(BF16)

    grp = NA_ROW_GROUP
    nk = kr * GRID_W

    def rows_group(gi, carry):
        r0 = gi * grp
        q0 = pl.multiple_of(r0 * GRID_W, grp * GRID_W)
        ws = [jnp.clip(r0 + t - kr // 2, 0, n_rows - kr) for t in range(grp)]
        k0 = [pl.multiple_of(w * GRID_W, GRID_W) for w in ws]
        s_loc = [_dot_nt(qs[pl.ds(q0 + t * GRID_W, GRID_W), :], ks[pl.ds(k0[t], nk), :]) for t in range(grp)]
        s_ctx = _dot_nt(qs[pl.ds(q0, grp * GRID_W), :], kc)
        p_loc, p_ctx, den = [], [], []
        for t in range(grp):
            sl = s_loc[t] + bias_ref[r0 + t - ws[t]]
            sc = s_ctx[t * GRID_W:(t + 1) * GRID_W]
            m = jnp.maximum(jnp.max(sl, axis=-1, keepdims=True), jnp.max(sc, axis=-1, keepdims=True))
            pl_t = jnp.exp(sl - m)
            pc_t = jnp.exp(sc - m)
            den.append(jnp.sum(pl_t, axis=-1, keepdims=True) + jnp.sum(pc_t, axis=-1, keepdims=True))
            p_loc.append(pl_t.astype(BF16))
            p_ctx.append(pc_t.astype(BF16))
        o_loc = [jnp.dot(p_loc[t], vs[pl.ds(k0[t], nk), :], preferred_element_type=F32) for t in range(grp)]
        o_ctx = jnp.dot(jnp.concatenate(p_ctx, axis=0), vc, preferred_element_type=F32)
        for t in range(grp):
            o = (o_loc[t] + o_ctx[t * GRID_W:(t + 1) * GRID_W]) / den[t]
            o_ref[pl.ds(q0 + t * GRID_W, GRID_W), :] = o.astype(o_ref.dtype)
        return carry

    lax.fori_loop(0, n_rows // grp, rows_group, 0)

    if with_ctx_out:
        qc = (_rms(qc_ref[...], gq_ref[...]) * scale).astype(BF16)
        s = _dot_nt(qc, kc)
        p = jnp.exp(s - jnp.max(s, axis=-1, keepdims=True))
        o = jnp.dot(p.astype(BF16), vc, preferred_element_type=F32) / jnp.sum(p, axis=-1, keepdims=True)
        oc_ref[...] = o.astype(oc_ref.dtype)


def natten(px, pc, col_q, gq, gk, bias, with_ctx_out):
    bsz, l, _ = px.shape
    lc = pc.shape[1]
    h, kr, _, nk = bias.shape
    dh = gq.shape[-1]
    cb = col_q // dh
    assert (l // GRID_W) % NA_ROW_GROUP == 0

    def head_spec(rows, which):
        return pl.BlockSpec((None, rows, dh), lambda b, hh: (b, 0, cb + which * h + hh))

    in_specs = [head_spec(l, 0), head_spec(l, 1), head_spec(l, 2), head_spec(lc, 1), head_spec(lc, 2),
                pl.BlockSpec((1, dh), lambda b, hh: (0, 0)), pl.BlockSpec((1, dh), lambda b, hh: (0, 0)),
                pl.BlockSpec((None, kr, GRID_W, nk), lambda b, hh: (hh, 0, 0, 0))]
    args = [px, px, px, pc, pc, gq, gk, bias]
    out_specs = [pl.BlockSpec((None, l, dh), lambda b, hh: (b, 0, hh))]
    out_shape = [jax.ShapeDtypeStruct((bsz, l, h * dh), BF16)]
    if with_ctx_out:
        in_specs.append(head_spec(lc, 0))
        args.append(pc)
        out_specs.append(pl.BlockSpec((None, lc, dh), lambda b, hh: (b, 0, hh)))
        out_shape.append(jax.ShapeDtypeStruct((bsz, lc, h * dh), BF16))
    outs = pl.pallas_call(
        functools.partial(_natten_kernel, n_rows=l // GRID_W, with_ctx_out=with_ctx_out),
        grid=(bsz, h),
        in_specs=in_specs, out_specs=out_specs, out_shape=out_shape,
        scratch_shapes=[pltpu.VMEM((l, dh), BF16)] * 3,
        compiler_params=_cparams(("arbitrary", "arbitrary")),
        name="natten",
    )(*args)
    return outs if with_ctx_out else (outs[0], None)


def _cos_sin(n, period):
    k = np.arange(n, dtype=np.int64)
    ang = (2.0 * np.pi / period) * ((k[:, None] * k[None, :]) % period)
    return np.cos(ang), np.sin(ang)


def _fourier_kernel(u_ref, w_ref, cd_ref, sd_ref, cs_ref, o_ref, wc_ref, ws_ref, v_ref, *, scale):
    l, c = u_ref.shape
    dg = cd_ref.shape[0]

    @pl.when(pl.program_id(0) == 0)
    def _():
        wc_ref[...] = jnp.zeros_like(wc_ref)
        ws_ref[...] = jnp.zeros_like(ws_ref)
        for g in range(c // dg):
            sl = slice(g * dg, (g + 1) * dg)
            wg = w_ref[g]
            wc_ref[sl, sl] = jnp.dot(cd_ref[...], wg, precision=lax.Precision.HIGHEST,
                                     preferred_element_type=F32).astype(BF16)
            ws_ref[sl, sl] = jnp.dot(sd_ref[...], wg, precision=lax.Precision.HIGHEST,
                                     preferred_element_type=F32).astype(BF16)

    u = u_ref[...].astype(BF16)
    v_ref[0:l, :] = jnp.dot(u, wc_ref[...], preferred_element_type=F32).astype(BF16)
    v_ref[l:2 * l, :] = jnp.dot(u, ws_ref[...], preferred_element_type=F32).astype(BF16)
    y = jnp.dot(cs_ref[...], v_ref[...], preferred_element_type=F32)
    o_ref[...] = (y * scale).astype(o_ref.dtype)


def fourier_mix(p, col, w):
    bsz, l, _ = p.shape
    g, dg, _ = w.shape
    c = g * dg
    cl, sl = _cos_sin(l, l)
    cd, sd = _cos_sin(dg, dg)
    cs = jnp.asarray(np.concatenate([cl, -sl], axis=1), BF16)
    return pl.pallas_call(
        functools.partial(_fourier_kernel, scale=float((l * dg) ** -0.5)),
        grid=(bsz,),
        in_specs=[pl.BlockSpec((None, l, c), lambda b: (b, 0, col // c)),
                  pl.BlockSpec((g, dg, dg), lambda b: (0, 0, 0)),
                  pl.BlockSpec((dg, dg), lambda b: (0, 0)),
                  pl.BlockSpec((dg, dg), lambda b: (0, 0)),
                  pl.BlockSpec((l, 2 * l), lambda b: (0, 0), pipeline_mode=pl.Buffered(1))],
        out_specs=pl.BlockSpec((None, l, c), lambda b: (b, 0, 0)),
        out_shape=jax.ShapeDtypeStruct((bsz, l, c), BF16),
        scratch_shapes=[pltpu.VMEM((c, c), BF16), pltpu.VMEM((c, c), BF16), pltpu.VMEM((2 * l, c), BF16)],
        compiler_params=_cparams(("arbitrary",)),
        name="fourier_mix",
    )(p, w, jnp.asarray(cd, F32), jnp.asarray(sd, F32), cs)


HY_CBLK = 256
HY_FBLK = 512


def _hyena_dft_matrix(l):
    k = np.arange(l, dtype=np.int64)
    ang = (np.pi / l) * ((k[:, None] * k[None, :]) % (2 * l))
    sn = np.sin(ang)
    sn[0, :] = 1.0 - 2.0 * (k % 2)
    return jnp.asarray(np.concatenate([np.cos(ang), sn], axis=0), BF16)


def _hyena_pos_features(l):
    t = np.linspace(0.0, 1.0, l)[:, None]
    w = (2.0 * np.pi / l) * np.arange(l)[:, None]
    f = np.linspace(1e-4, HY_BANDS - 1, HY_BANDS)[None, :]
    z = np.concatenate([t, np.cos(f * w), -np.sin(f * w)], axis=-1)
    return np.pad(z, ((0, 0), (0, LANES - z.shape[1])))


def _split_bf16(x):
    hi = x.astype(BF16)
    return hi, (x - hi.astype(F32)).astype(BF16)


def _hyena_filter_kernel(z_ref, w1_ref, b1_ref, w2_ref, b2_ref, fr_ref, w3f_ref, w3b_ref, dl_ref, m_ref,
                         p_ref, q_ref):
    l = z_ref.shape[0]
    hp = lax.Precision.HIGHEST
    z = z_ref[...]
    h = jnp.sin(fr_ref[0:1, :] * (jnp.dot(z, w1_ref[...], precision=hp, preferred_element_type=F32)
                                  + b1_ref[...]))
    h = jnp.sin(fr_ref[1:2, :] * (jnp.dot(h, w2_ref[...], precision=hp, preferred_element_type=F32)
                                  + b2_ref[...]))
    window = jnp.exp(-z[:, 0:1] * dl_ref[...]) + HY_DECAY_SHIFT
    hf = jnp.dot(h, w3f_ref[...], precision=hp, preferred_element_type=F32) * window
    hb = jnp.dot(h, w3b_ref[...], precision=hp, preferred_element_type=F32) * window
    norm = jnp.sum(jnp.abs(hf) + jnp.abs(hb), axis=0, keepdims=True) + EPS
    hf = hf / norm
    hb = hb / norm
    g1h, g1l = _split_bf16(hf + hb)
    g2h, g2l = _split_bf16(hb - hf)
    f1 = (jnp.dot(m_ref[...], g1h, preferred_element_type=F32)
          + jnp.dot(m_ref[...], g1l, preferred_element_type=F32))
    f2 = (jnp.dot(m_ref[l:2 * l, :], g2h, preferred_element_type=F32)
          + jnp.dot(m_ref[l:2 * l, :], g2l, preferred_element_type=F32))
    p_ref[...] = f1[0:l]
    row = lax.broadcasted_iota(jnp.int32, f2.shape, 0)
    q_ref[...] = jnp.where(row == 0, f1[l:l + 1], f2)


def hyena_filter_spectrum(l, w1, b1, w2, b2, w3, freq, m):
    c = w3.shape[1] // 2
    hid = w1.shape[1]
    z = jnp.asarray(_hyena_pos_features(l), F32)
    w1p = jnp.pad(w1, ((0, z.shape[1] - w1.shape[0]), (0, 0)))
    deltas = np.abs(np.linspace(math.log(HY_DECAY_TARGET) / HY_SLOW_PCT,
                                math.log(HY_DECAY_TARGET) / HY_FAST_PCT, c))[None, :]
    nb = c // HY_CBLK
    full = lambda shape: pl.BlockSpec(shape, lambda j: (0,) * len(shape))
    return pl.pallas_call(
        _hyena_filter_kernel,
        grid=(nb,),
        in_specs=[full(z.shape), full(w1p.shape), full((1, hid)), full(w2.shape), full((1, hid)),
                  full((2, hid)),
                  pl.BlockSpec((hid, HY_CBLK), lambda j: (0, j)),
                  pl.BlockSpec((hid, HY_CBLK), lambda j: (0, nb + j)),
                  pl.BlockSpec((1, HY_CBLK), lambda j: (0, j)),
                  pl.BlockSpec(m.shape, lambda j: (0, 0), pipeline_mode=pl.Buffered(1))],
        out_specs=[pl.BlockSpec((l, HY_CBLK), lambda j: (0, j))] * 2,
        out_shape=[jax.ShapeDtypeStruct((l, c), F32)] * 2,
        compiler_params=_cparams(("arbitrary",)),
        name="hyena_filter",
    )(z, w1p, b1.reshape(1, hid), w2, b2.reshape(1, hid), freq, w3, w3, jnp.asarray(deltas, F32), m)


def _shift_rows(u, down):
    l = u.shape[0]
    row = lax.broadcasted_iota(jnp.int32, u.shape, 0)
    if down:
        return jnp.where(row == 0, 0.0, pltpu.roll(u, 1, 0))
    return jnp.where(row == l - 1, 0.0, pltpu.roll(u, l - 1, 0))


def _dwconv3(u, taps):
    return (taps[3:4] + _shift_rows(u, True) * taps[0:1] + u * taps[1:2] + _shift_rows(u, False) * taps[2:3])


def _hyena_conv_kernel(x0_ref, x1_ref, v_ref, taps_ref, bias_ref, p_ref, q_ref, m_ref, o_ref):
    l, c = x0_ref.shape
    x1 = _dwconv3(x1_ref[...], taps_ref[1])
    s = _dwconv3(v_ref[...], taps_ref[2]) * x1
    sb = s.astype(BF16)
    fb = min(HY_FBLK, l)
    nblk = l // fb
    inv_l = 1.0 / l

    def forward(i):
        return (jnp.dot(m_ref[i * fb:(i + 1) * fb, :], sb, preferred_element_type=F32),
                jnp.dot(m_ref[l + i * fb:l + (i + 1) * fb, :], sb, preferred_element_type=F32))

    y_cos_acc = y_sin_acc = nyquist = None
    ab = forward(0)
    x0 = _dwconv3(x0_ref[...], taps_ref[0])
    for i in range(nblk):
        a, b = ab
        if i + 1 < nblk:
            ab = forward(i + 1)
        p, q = p_ref[i * fb:(i + 1) * fb, :], q_ref[i * fb:(i + 1) * fb, :]
        bq = b * q
        if i == 0:
            first = lax.broadcasted_iota(jnp.int32, (fb, c), 0) == 0
            nyquist = bq[0:1] * (0.5 * inv_l)
            y_cos = (a * p + jnp.where(first, 0.0, bq)) * jnp.where(first, 0.5 * inv_l, inv_l)
            y_sin = jnp.where(first, 0.0, (b * p - a * q) * inv_l)
        else:
            y_cos = (a * p + bq) * inv_l
            y_sin = (b * p - a * q) * inv_l
        d_cos = jnp.dot(m_ref[0:l, i * fb:(i + 1) * fb], y_cos.astype(BF16), preferred_element_type=F32)
        d_sin = jnp.dot(m_ref[l:2 * l, i * fb:(i + 1) * fb], y_sin.astype(BF16), preferred_element_type=F32)
        y_cos_acc = d_cos if y_cos_acc is None else y_cos_acc + d_cos
        y_sin_acc = d_sin if y_sin_acc is None else y_sin_acc + d_sin
    row = lax.broadcasted_iota(jnp.int32, (l, c), 0)
    y = (y_cos_acc + jnp.where(row == 0, 0.0, y_sin_acc)
         + jnp.where(row % 2 == 0, 1.0, -1.0) * nyquist)
    o_ref[...] = ((y + s * bias_ref[...]) * x0).astype(o_ref.dtype)


def hyena_conv(p, col, conv_w, conv_b, bias, spec_p, spec_q, m):
    bsz, l, _ = p.shape
    c = bias.shape[-1]
    nb = c // HY_CBLK
    cb0 = col // HY_CBLK
    taps = jnp.concatenate([conv_w, conv_b[None]], axis=0)
    taps = taps.reshape(4, 3, nb, HY_CBLK).transpose(2, 1, 0, 3)

    def part(k):
        return pl.BlockSpec((None, l, HY_CBLK), lambda j, b: (b, 0, cb0 + k * nb + j))

    return pl.pallas_call(
        _hyena_conv_kernel,
        grid=(nb, bsz),
        in_specs=[part(0), part(1), part(2),
                  pl.BlockSpec((None, 3, 4, HY_CBLK), lambda j, b: (j, 0, 0, 0)),
                  pl.BlockSpec((1, HY_CBLK), lambda j, b: (0, j)),
                  pl.BlockSpec((l, HY_CBLK), lambda j, b: (0, j)),
                  pl.BlockSpec((l, HY_CBLK), lambda j, b: (0, j)),
                  pl.BlockSpec(m.shape, lambda j, b: (0, 0), pipeline_mode=pl.Buffered(1))],
        out_specs=pl.BlockSpec((None, l, HY_CBLK), lambda j, b: (b, 0, j)),
        out_shape=jax.ShapeDtypeStruct((bsz, l, c), BF16),
        compiler_params=_cparams(("arbitrary", "arbitrary")),
        name="hyena_conv",
    )(p, p, p, taps, bias.reshape(1, c), spec_p, spec_q, m)


GLA_HP = 2
GLA_ROPE_PAIR = 16
GLA_SCAN_UNROLL = 16


def _gla_rope_tables(l, dk):
    half = dk // 2
    nf = half // 2
    assert nf == GLA_ROPE_PAIR
    inv = ROPE_THETA ** (-np.arange(nf, dtype=np.float64) / nf)
    t = np.arange(l)
    ang_r = (t // GRID_W)[:, None] * inv
    ang_c = (t % GRID_W)[:, None] * inv
    cos = np.concatenate([np.cos(ang_r)] * 2 + [np.cos(ang_c)] * 2, axis=1)
    sin = np.concatenate([-np.sin(ang_r), np.sin(ang_r), -np.sin(ang_c), np.sin(ang_c)], axis=1)
    return (jnp.asarray(np.tile(cos, (1, GLA_HP)), F32), jnp.asarray(np.tile(sin, (1, GLA_HP)), F32))


def _rope(x, cos, sin):
    lane = lax.broadcasted_iota(jnp.int32, x.shape, 1)
    lanes = x.shape[1]
    partner = jnp.where(lane % (2 * GLA_ROPE_PAIR) < GLA_ROPE_PAIR,
                        pltpu.roll(x, lanes - GLA_ROPE_PAIR, 1), pltpu.roll(x, GLA_ROPE_PAIR, 1))
    return x * cos + partner * sin


def _log_sigmoid(x):
    return jnp.minimum(x, 0.0) - jnp.log(1.0 + jnp.exp(-jnp.abs(x)))


def _gla_kernel(*refs, with_ctx_out):
    (q_ref, k_ref, v_ref, r_ref, z_ref, cq_ref, ck_ref, cv_ref, cr_ref, cz_ref,
     wz_ref, bz_ref, g_ref, cos_ref, sin_ref) = refs[:15]
    if with_ctx_out:
        o_ref, oc_ref = refs[15:17]
        scratch = refs[17:]
    else:
        o_ref, oc_ref = refs[15], None
        scratch = refs[16:]
    qs, ks, las, ofs, obs, cqs, cks, clas, cofs, cobs, st_f, st_b = scratch
    dk2 = q_ref.shape[1]
    dv2 = v_ref.shape[1]
    dk, dv = dk2 // GLA_HP, dv2 // GLA_HP
    ch = GLA_CHUNK
    hp = lax.Precision.HIGHEST

    def gates(z):
        pre = jnp.dot(z, wz_ref[...], precision=hp, preferred_element_type=F32) + bz_ref[...]
        return _log_sigmoid(pre) * (1.0 / GLA_TAU)

    qs[...] = _rope(q_ref[...] * dk ** -0.5, cos_ref[...], sin_ref[...])
    ks[...] = _rope(k_ref[...], cos_ref[...], sin_ref[...])
    las[...] = gates(z_ref[...])
    cqs[...] = cq_ref[...] * dk ** -0.5
    cks[...] = ck_ref[...]
    clas[...] = gates(cz_ref[...])
    st_f[...] = jnp.zeros_like(st_f)
    st_b[...] = jnp.zeros_like(st_b)

    ri = lax.broadcasted_iota(jnp.int32, (ch, ch), 0)
    ci = lax.broadcasted_iota(jnp.int32, (ch, ch), 1)
    tri = {False: ri >= ci, True: ri <= ci}
    tri_b16 = {d: jnp.where(m, 1.0, 0.0).astype(BF16) for d, m in tri.items()}
    tri2 = {d: jnp.concatenate([m] * GLA_HP, axis=0) for d, m in tri.items()}
    lane_head = lax.broadcasted_iota(jnp.int32, (ch, dk2), 1) // dk
    st_r = lax.broadcasted_iota(jnp.int32, (dv2, dk2), 0) // dv
    st_c = lax.broadcasted_iota(jnp.int32, (dv2, dk2), 1) // dk
    st_diag = st_r == st_c

    def scan(q_s, k_s, la_s, v_in, of_s, ob_s, n):
        unroll = min(GLA_SCAN_UNROLL, n)

        def body(trip, carry):
            items = []
            for u in range(unroll):
                c = trip * unroll + u
                items.append((False, pl.ds(pl.multiple_of(c * ch, ch), ch)))
                items.append((True, pl.ds(pl.multiple_of((n - 1 - c) * ch, ch), ch)))
            v_c = [v_in[rows, :].astype(BF16) for _, rows in items]
            cum = []
            for bw, rows in items:
                la_hi, la_lo = _split_bf16(la_s[rows, dk2:2 * dk2] if bw else la_s[rows, 0:dk2])
                cum.append(jnp.dot(tri_b16[bw], la_hi, preferred_element_type=F32)
                           + jnp.dot(tri_b16[bw], la_lo, preferred_element_type=F32))
            q_dec, k_end, decay, sc = [], [], [], []
            for i, (bw, rows) in enumerate(items):
                q_c, k_c = q_s[rows, :], k_s[rows, :]
                tot = cum[i][0:1] if bw else cum[i][ch - 1:ch]
                qd = q_c * jnp.exp(cum[i])
                k_inv = (k_c * jnp.exp(-cum[i])).astype(BF16)
                k_end.append((k_c * jnp.exp(tot - cum[i])).astype(BF16))
                decay.append(jnp.exp(tot))
                q_heads = jnp.concatenate([jnp.where(lane_head == h, qd, 0.0) for h in range(GLA_HP)], axis=0)
                sc.append(_dot_nt(q_heads.astype(BF16), k_inv))
                q_dec.append(qd.astype(BF16))
            o_intra, ds_t = [], []
            for i, (bw, rows) in enumerate(items):
                pv = jnp.dot(jnp.where(tri2[bw], sc[i], 0.0).astype(BF16), v_c[i],
                             preferred_element_type=F32)
                o_intra.append(jnp.concatenate(
                    [pv[h * ch:(h + 1) * ch, h * dv:(h + 1) * dv] for h in range(GLA_HP)], axis=1))
                ds_t.append(lax.dot_general(v_c[i], k_end[i], (((0,), (0,)), ((), ())),
                                            preferred_element_type=F32))
            for i, (bw, rows) in enumerate(items):
                st, o_s = (st_b, ob_s) if bw else (st_f, of_s)
                s_t = st[...]
                o_s[rows, :] = o_intra[i] + _dot_nt(q_dec[i], s_t.astype(BF16))
                st[...] = s_t * decay[i] + jnp.where(st_diag, ds_t[i], 0.0)
            return carry

        lax.fori_loop(0, n // unroll, body, 0)

    def finish(of_s, ob_s, gate_ref, out_ref, n_blocks, blk):
        def body(i, carry):
            rows = pl.ds(pl.multiple_of(i * blk, blk), blk)
            o = of_s[rows, :] + ob_s[rows, :]
            gate = gate_ref[rows, :]
            parts = []
            for h in range(GLA_HP):
                oh = o[:, h * dv:(h + 1) * dv]
                gh = gate[:, h * dv:(h + 1) * dv]
                parts.append(_rms(oh, g_ref[...]) * (gh * jax.nn.sigmoid(gh)))
            out_ref[rows, :] = jnp.concatenate(parts, axis=1).astype(out_ref.dtype)
            return carry
        lax.fori_loop(0, n_blocks, body, 0)

    l, lc = q_ref.shape[0], cq_ref.shape[0]
    scan(cqs, cks, clas, cv_ref, cofs, cobs, lc // ch)
    if with_ctx_out:
        finish(cofs, cobs, cr_ref, oc_ref, 1, lc)
    scan(qs, ks, las, v_ref, ofs, obs, l // ch)
    finish(ofs, obs, r_ref, o_ref, l // lc, lc)


def gla(px, pxz, pc, pcz, w_gate, b_gate, out_gain, with_ctx_out):
    bsz, l, _ = px.shape
    lc = pc.shape[1]
    dv = out_gain.shape[-1]
    hdk = w_gate.shape[-1]
    dk = hdk // GLA_HEADS
    dk2, dv2 = GLA_HP * dk, GLA_HP * dv
    nhp = GLA_HEADS // GLA_HP
    zw = pxz.shape[-1]
    rank = w_gate.shape[1]
    wz = jnp.zeros((nhp, zw, 2 * dk2), F32)
    for u in range(2):
        blk = w_gate[u].reshape(rank, nhp, dk2).transpose(1, 0, 2)
        wz = wz.at[:, u * rank:(u + 1) * rank, u * dk2:(u + 1) * dk2].set(blk)
    bz = b_gate.reshape(2, nhp, dk2).transpose(1, 0, 2).reshape(nhp, 1, 2 * dk2)
    cos, sin = _gla_rope_tables(l, dk)
    k_cb, v_cb, r_cb = hdk // dk2, 2 * hdk // dv2, (2 * hdk + GLA_HEADS * dv) // dv2

    def col(rows, width, cb):
        return pl.BlockSpec((None, rows, width), lambda b, j: (b, 0, cb + j))

    def whole(rows, width):
        return pl.BlockSpec((None, rows, width), lambda b, j: (b, 0, 0))

    const = lambda shape: pl.BlockSpec(shape, lambda b, j: (0,) * len(shape))
    in_specs = [col(l, dk2, 0), col(l, dk2, k_cb), col(l, dv2, v_cb), col(l, dv2, r_cb), whole(l, zw),
                col(lc, dk2, 0), col(lc, dk2, k_cb), col(lc, dv2, v_cb), col(lc, dv2, r_cb), whole(lc, zw),
                pl.BlockSpec((None, zw, 2 * dk2), lambda b, j: (j, 0, 0)),
                pl.BlockSpec((None, 1, 2 * dk2), lambda b, j: (j, 0, 0)),
                const((1, dv)), const((l, dk2)), const((l, dk2))]
    out_specs = [pl.BlockSpec((None, l, dv2), lambda b, j: (b, 0, j))]
    out_shape = [jax.ShapeDtypeStruct((bsz, l, GLA_HEADS * dv), BF16)]
    if with_ctx_out:
        out_specs.append(pl.BlockSpec((None, lc, dv2), lambda b, j: (b, 0, j)))
        out_shape.append(jax.ShapeDtypeStruct((bsz, lc, GLA_HEADS * dv), BF16))
    scratch = [pltpu.VMEM((l, dk2), F32), pltpu.VMEM((l, dk2), F32), pltpu.VMEM((l, 2 * dk2), F32),
               pltpu.VMEM((l, dv2), F32), pltpu.VMEM((l, dv2), F32),
               pltpu.VMEM((lc, dk2), F32), pltpu.VMEM((lc, dk2), F32), pltpu.VMEM((lc, 2 * dk2), F32),
               pltpu.VMEM((lc, dv2), F32), pltpu.VMEM((lc, dv2), F32),
               pltpu.VMEM((dv2, dk2), F32), pltpu.VMEM((dv2, dk2), F32)]
    outs = pl.pallas_call(
        functools.partial(_gla_kernel, with_ctx_out=with_ctx_out),
        grid=(bsz, nhp),
        in_specs=in_specs, out_specs=out_specs, out_shape=out_shape, scratch_shapes=scratch,
        compiler_params=_cparams(("arbitrary", "arbitrary")),
        name="gla",
    )(px, px, px, px, pxz, pc, pc, pc, pc, pcz, wz, bz, out_gain.reshape(1, dv), cos, sin)
    return outs if with_ctx_out else (outs[0], None)


def kernel(x, c, ctx, c_ctx, w_mod, b_mod, g_mix, w_in, gla_gate_w, gla_gate_b, gla_out_g,
           hy_conv_w, hy_conv_b, hy_w1, hy_b1, hy_w2, hy_b2, hy_w3, hy_freq, hy_bias, fn_w,
           na_q_g, na_k_g, na_rpb, w_out, g_ffn, ffn_w_up, ffn_conv_w, ffn_conv_b, ffn_w_down):
    bsz, l_lat, d = x.shape
    l_ctx = ctx.shape[1]
    depth = w_mod.shape[0]
    w = d // 4

    n_cond = -(-(bsz + 1) // SUBLANES) * SUBLANES
    cond = jnp.zeros((n_cond, d), F32).at[:bsz].set(c).at[bsz].set(c_ctx)
    mods = mod_vectors(cond, w_mod, b_mod).reshape(depth, n_cond, N_MOD, d)
    mods = jnp.pad(mods, ((0, 0), (0, 0), (0, MOD_ROWS - N_MOD), (0, 0)))

    z0 = 3 * w
    zw = 2 * GLA_GATE_RANK
    w_main = jnp.concatenate([w_in[:, :, :z0], w_in[:, :, z0 + zw:]], axis=-1).astype(BF16)
    w_z = jnp.pad(w_in[:, :, z0:z0 + zw], ((0, 0), (0, 0), (0, LANES - zw))).astype(BF16)
    w_out_b = w_out.astype(BF16)
    w_up_b = ffn_w_up.astype(BF16)
    w_down_b = ffn_w_down.astype(BF16)

    m_lat = _hyena_dft_matrix(l_lat)
    m_ctx = _hyena_dft_matrix(l_ctx)
    ctx_flat = ctx.reshape(1, bsz * l_ctx, d)
    for layer in range(depth):
        last = layer == depth - 1
        mod_x = mods[layer, :bsz]
        mod_c = mods[layer, bsz:bsz + 1]
        g_mix_l = g_mix[layer].reshape(1, d)
        g_ffn_l = g_ffn[layer].reshape(1, d)

        px, pxz = in_proj(x, mod_x, g_mix_l, w_main, w_z, layer, tm=1024)
        pc, pcz = in_proj(ctx_flat, mod_c, g_mix_l, w_main, w_z, layer, tm=1024)
        pc = pc.reshape(bsz, l_ctx, -1)
        pcz = pcz.reshape(bsz, l_ctx, -1)

        y_a, yc_a = gla(px, pxz, pc, pcz, gla_gate_w[layer], gla_gate_b[layer], gla_out_g[layer],
                        with_ctx_out=not last)
        hy_filt = (hy_w1[layer], hy_b1[layer], hy_w2[layer], hy_b2[layer], hy_w3[layer], hy_freq[layer])
        hy_p, hy_q = hyena_filter_spectrum(l_lat, *hy_filt, m_lat)
        y_b = hyena_conv(px, 3 * w, hy_conv_w[layer], hy_conv_b[layer], hy_bias[layer], hy_p, hy_q, m_lat)
        y_c = fourier_mix(px, 6 * w, fn_w[layer])
        na_bias = natten_bias_table(na_rpb[layer], l_lat // GRID_W)
        y_d, yc_d = natten(px, pc, 7 * w, na_q_g[layer].reshape(1, -1), na_k_g[layer].reshape(1, -1),
                           na_bias, with_ctx_out=not last)
        x = out_proj([y_a, y_b, y_c, y_d], w_out_b, layer, x, mod_x, tm=1024)
        cw = ffn_conv_w[layer]
        cb = ffn_conv_b[layer].reshape(1, -1)
        x = conv_ffn(x, mod_x, g_ffn_l, w_up_b, cw, cb, w_down_b, layer, tm=1024)

        if not last:
            hc_p, hc_q = hyena_filter_spectrum(l_ctx, *hy_filt, m_ctx)
            yc_b = hyena_conv(pc, 3 * w, hy_conv_w[layer], hy_conv_b[layer], hy_bias[layer], hc_p, hc_q, m_ctx)
            yc_c = fourier_mix(pc, 6 * w, fn_w[layer])
            ycs = [y.reshape(1, bsz * l_ctx, w) for y in (yc_a, yc_b, yc_c, yc_d)]
            ctx_flat = out_proj(ycs, w_out_b, layer, ctx_flat, mod_c, tm=1024)
            ctx_flat = conv_ffn(ctx_flat, mod_c, g_ffn_l, w_up_b, cw, cb, w_down_b, layer, tm=1024,
                                seg_len=l_ctx)
    return x
```
